```python
import jax, jax.numpy as jnp
from jax import lax
import numpy as np

D_MODEL = 2048
BATCH = 2
SEQ = 8192
DEPTH = 1
DEC_BATCH = 8
DEC_SEQ = 64
PAST_LEN = 2048

CHUNK = 64
POOL_WINDOWS = (2, 4, 8, 16)
POOL_GROUPS = len(POOL_WINDOWS)
POOL_WIDTH = D_MODEL // 2
POOL_GROUP_DIM = POOL_WIDTH // POOL_GROUPS
POOL_BUF = max(POOL_WINDOWS) - 1
SB_HEADS = 8
SB_HEAD_DIM = 128
SB_WIDTH = SB_HEADS * SB_HEAD_DIM
SB_BLOCK = 128
IN_WIDTH = POOL_WIDTH + 3 * SB_WIDTH + 2 * D_MODEL
N_MEM = 256
XA_HEADS = 4
XA_HEAD_DIM = 128
XA_WIDTH = XA_HEADS * XA_HEAD_DIM
D_FF = 5632
CONV_W = 3
EPS = 1e-6

kernel_name = "pool_stickbreak_gated_hybrid_step"


def rmsnorm(x, g):
    xf = x.astype(jnp.float32)
    y = xf * lax.rsqrt(jnp.mean(xf * xf, axis=-1, keepdims=True) + EPS)
    return (y * g.astype(jnp.float32)).astype(x.dtype)


def pool_mixer(u, buf, pos, pool_w, pool_scale):
    B, L, P = u.shape
    u_cat = jnp.concatenate([buf.astype(u.dtype), u], axis=1)
    c = jnp.cumsum(u_cat.astype(jnp.float32), axis=1)
    c = jnp.concatenate([jnp.zeros((B, 1, P), jnp.float32), c], axis=1)
    end = c[:, POOL_BUF + 1:]
    means = []
    for g, w in enumerate(POOL_WINDOWS):
        sl = slice(g * POOL_GROUP_DIM, (g + 1) * POOL_GROUP_DIM)
        start = c[:, POOL_BUF + 1 - w: POOL_BUF + 1 - w + L, sl]
        cnt = jnp.minimum(w, pos + 1).astype(jnp.float32)[None, :, None]
        means.append((end[..., sl] - start) / cnt)
    mean = jnp.concatenate(means, axis=-1)
    d = (mean - u.astype(jnp.float32)).astype(u.dtype).reshape(B, L, POOL_GROUPS, POOL_GROUP_DIM)
    y = jnp.einsum('blgc,gcd->blgd', d, pool_w).reshape(B, L, P) * pool_scale
    return y, u_cat[:, -POOL_BUF:]


def stick_breaking(q, k, v, q_pos, k_pos):
    B, Lq, H, dh = q.shape
    blk = min(SB_BLOCK, Lq)
    nb = Lq // blk
    scale = 1.0 / np.sqrt(dh)
    qb = q.reshape(B, nb, blk, H, dh).transpose(1, 0, 2, 3, 4)
    pb = q_pos.reshape(nb, blk)

    def one_block(args):
        qi, pi = args
        z = jnp.einsum('bqhd,bkhd->bhqk', qi, k).astype(jnp.float32) * scale
        mask = (k_pos[None, :] < pi[:, None])[None, None]
        lsn = jnp.where(mask, jax.nn.log_sigmoid(-z), 0.0)
        between = lax.cumsum(lsn, axis=3, reverse=True) - lsn
        a = jnp.where(mask, jnp.exp(jax.nn.log_sigmoid(z) + between), 0.0)
        return jnp.einsum('bhqk,bkhd->bqhd', a.astype(v.dtype), v)

    out = lax.map(one_block, (qb, pb))
    return out.transpose(1, 0, 2, 3, 4).reshape(B, Lq, H, dh)


def mem_kv(mem, g, w_kv):
    B, M, _ = mem.shape
    kv = rmsnorm(mem, g) @ w_kv
    k, v = jnp.split(kv, 2, axis=-1)
    return k.reshape(B, M, XA_HEADS, XA_HEAD_DIM), v.reshape(B, M, XA_HEADS, XA_HEAD_DIM)


def cross_attn(xn, mk, mv, wq, wo):
    B, L, _ = xn.shape
    q = (xn @ wq).reshape(B, L, XA_HEADS, XA_HEAD_DIM)
    s = jnp.einsum('bqhd,bmhd->bhqm', q, mk).astype(jnp.float32) / np.sqrt(XA_HEAD_DIM)
    p = jax.nn.softmax(s, axis=-1).astype(mv.dtype)
    o = jnp.einsum('bhqm,bmhd->bqhd', p, mv).reshape(B, L, XA_WIDTH)
    return o @ wo


def conv_ffn(xn, buf, w_up, conv_w, conv_b, w_down):
    L = xn.shape[1]
    h = xn @ w_up
    h_cat = jnp.concatenate([buf.astype(h.dtype), h], axis=1)
    hc = conv_b + sum(conv_w[i] * h_cat[:, i:i + L] for i in range(CONV_W))
    gate, val = jnp.split(hc, 2, axis=-1)
    return (jax.nn.silu(gate) * val) @ w_down, h_cat[:, -(CONV_W - 1):]


def trunk_layer(x, pos, pool_buf, sb_k_past, sb_v_past, mk, mv, conv_buf, p):
    B, L, _ = x.shape
    xn = rmsnorm(x, p['norm_mix_g'])
    proj = xn @ p['w_in']
    o = 0
    u = proj[..., o:o + POOL_WIDTH]; o += POOL_WIDTH
    q = proj[..., o:o + SB_WIDTH].reshape(B, L, SB_HEADS, SB_HEAD_DIM); o += SB_WIDTH
    k = proj[..., o:o + SB_WIDTH].reshape(B, L, SB_HEADS, SB_HEAD_DIM); o += SB_WIDTH
    v = proj[..., o:o + SB_WIDTH].reshape(B, L, SB_HEADS, SB_HEAD_DIM); o += SB_WIDTH
    gates = jax.nn.sigmoid((proj[..., o:] + p['b_gate']).astype(jnp.float32)).astype(x.dtype)
    g_pool, g_sb = jnp.split(gates, 2, axis=-1)

    y_pool, new_pool_buf = pool_mixer(u, pool_buf, pos, p['pool_w'], p['pool_scale'])

    if sb_k_past is None:
        k_all, v_all, k_pos = k, v, pos
    else:
        P = sb_k_past.shape[1]
        k_all = jnp.concatenate([sb_k_past.astype(k.dtype), k], axis=1)
        v_all = jnp.concatenate([sb_v_past.astype(v.dtype), v], axis=1)
        k_pos = jnp.arange(P + L, dtype=jnp.int32)
    y_sb = stick_breaking(q, k_all, v_all, pos, k_pos).reshape(B, L, SB_WIDTH)

    merged = g_pool * (y_pool @ p['w_branch_pool']) + g_sb * (y_sb @ p['w_branch_sb'])
    x = x + merged @ p['w_out']

    x = x + cross_attn(rmsnorm(x, p['norm_xa_g']), mk.astype(x.dtype), mv.astype(x.dtype), p['xa_wq'], p['xa_wo'])

    f, new_conv = conv_ffn(rmsnorm(x, p['norm_ffn_g']), conv_buf, p['ffn_w_up'], p['ffn_conv_w'], p['ffn_conv_b'], p['ffn_w_down'])
    x = x + f
    return x, new_pool_buf, k, v, new_conv


def setup_inputs(seed: int = 0) -> dict:
    key = jax.random.key(seed)
    ks = iter(jax.random.split(key, 40))
    f32 = jnp.float32

    def nrm(shape, scale=1.0):
        return jax.random.normal(next(ks), shape, f32) * scale

    def gain(shape):
        return 1.0 + 0.02 * jax.random.normal(next(ks), shape, f32)

    D = D_MODEL
    return {
        "x_prompt": nrm((BATCH, SEQ, D)),
        "x_sample": nrm((DEC_BATCH, DEC_SEQ, D)),
        "mem_prompt": nrm((BATCH, N_MEM, D)),
        "state_pool": nrm((DEPTH, DEC_BATCH, POOL_BUF, POOL_WIDTH)),
        "cache_sb_k": nrm((DEPTH, DEC_BATCH, PAST_LEN, SB_HEADS, SB_HEAD_DIM)),
        "cache_sb_v": nrm((DEPTH, DEC_BATCH, PAST_LEN, SB_HEADS, SB_HEAD_DIM)),
        "cache_mem_k": nrm((DEPTH, DEC_BATCH, N_MEM, XA_HEADS, XA_HEAD_DIM)),
        "cache_mem_v": nrm((DEPTH, DEC_BATCH, N_MEM, XA_HEADS, XA_HEAD_DIM)),
        "state_ffn_conv": nrm((DEPTH, DEC_BATCH, CONV_W - 1, 2 * D_FF)),
        "norm_mix_g": gain((DEPTH, D)),
        "w_in": nrm((DEPTH, D, IN_WIDTH), D ** -0.5),
        "b_gate": nrm((DEPTH, 2 * D), 0.02),
        "pool_w": nrm((DEPTH, POOL_GROUPS, POOL_GROUP_DIM, POOL_GROUP_DIM), POOL_GROUP_DIM ** -0.5),
        "pool_scale": gain((DEPTH, POOL_WIDTH)),
        "w_branch_pool": nrm((DEPTH, POOL_WIDTH, D), POOL_WIDTH ** -0.5),
        "w_branch_sb": nrm((DEPTH, SB_WIDTH, D), SB_WIDTH ** -0.5),
        "w_out": nrm((DEPTH, D, D), D ** -0.5),
        "norm_xa_g": gain((DEPTH, D)),
        "norm_mem_g": gain((DEPTH, D)),
        "xa_wq": nrm((DEPTH, D, XA_WIDTH), D ** -0.5),
        "xa_wkv": nrm((DEPTH, D, 2 * XA_WIDTH), D ** -0.5),
        "xa_wo": nrm((DEPTH, XA_WIDTH, D), XA_WIDTH ** -0.5),
        "norm_ffn_g": gain((DEPTH, D)),
        "ffn_w_up": nrm((DEPTH, D, 2 * D_FF), D ** -0.5),
        "ffn_conv_w": nrm((DEPTH, CONV_W, 2 * D_FF), CONV_W ** -0.5),
        "ffn_conv_b": nrm((DEPTH, 2 * D_FF), 0.02),
        "ffn_w_down": nrm((DEPTH, D_FF, D), D_FF ** -0.5),
        "norm_final_g": gain((D,)),
    }


def reference(x_prompt, x_sample, mem_prompt, state_pool, cache_sb_k, cache_sb_v, cache_mem_k, cache_mem_v,
              state_ffn_conv, norm_mix_g, w_in, b_gate, pool_w, pool_scale, w_branch_pool, w_branch_sb, w_out,
              norm_xa_g, norm_mem_g, xa_wq, xa_wkv, xa_wo, norm_ffn_g, ffn_w_up, ffn_conv_w, ffn_conv_b,
              ffn_w_down, norm_final_g):
    Bp, Lp, _ = x_prompt.shape
    Bs, Ls, _ = x_sample.shape
    past = cache_sb_k.shape[2]
    pos_p = jnp.arange(Lp, dtype=jnp.int32)
    pos_s = past + jnp.arange(Ls, dtype=jnp.int32)

    hp, hs = x_prompt, x_sample
    npool_p, nk_p, nv_p, nmk_p, nmv_p, nconv_p = [], [], [], [], [], []
    npool_s, nk_s, nv_s, nconv_s = [], [], [], []
    for l in range(DEPTH):
        p = {
            'norm_mix_g': norm_mix_g[l], 'w_in': w_in[l], 'b_gate': b_gate[l], 'pool_w': pool_w[l],
            'pool_scale': pool_scale[l], 'w_branch_pool': w_branch_pool[l], 'w_branch_sb': w_branch_sb[l],
            'w_out': w_out[l], 'norm_xa_g': norm_xa_g[l], 'xa_wq': xa_wq[l], 'xa_wo': xa_wo[l],
            'norm_ffn_g': norm_ffn_g[l], 'ffn_w_up': ffn_w_up[l], 'ffn_conv_w': ffn_conv_w[l],
            'ffn_conv_b': ffn_conv_b[l], 'ffn_w_down': ffn_w_down[l],
        }
        mk_p, mv_p = mem_kv(mem_prompt, norm_mem_g[l], xa_wkv[l])
        pool0 = jnp.zeros((Bp, POOL_BUF, POOL_WIDTH), hp.dtype)
        conv0 = jnp.zeros((Bp, CONV_W - 1, 2 * D_FF), hp.dtype)
        hp, pb, kp, vp, cp = trunk_layer(hp, pos_p, pool0, None, None, mk_p, mv_p, conv0, p)
        npool_p.append(pb); nk_p.append(kp); nv_p.append(vp); nmk_p.append(mk_p); nmv_p.append(mv_p); nconv_p.append(cp)
        hs, pbs, ks_, vs_, cs = trunk_layer(hs, pos_s, state_pool[l], cache_sb_k[l], cache_sb_v[l],
                                            cache_mem_k[l], cache_mem_v[l], state_ffn_conv[l], p)
        npool_s.append(pbs); nk_s.append(ks_); nv_s.append(vs_); nconv_s.append(cs)

    y_prompt = rmsnorm(hp, norm_final_g)
    y_sample = rmsnorm(hs, norm_final_g)
    return (y_prompt, y_sample,
            jnp.stack(npool_p), jnp.stack(nk_p), jnp.stack(nv_p), jnp.stack(nmk_p), jnp.stack(nmv_p), jnp.stack(nconv_p),
            jnp.stack(npool_s), jnp.stack(nk_s), jnp.stack(nv_s), jnp.stack(nconv_s))
```

```python
import functools

import jax
import jax.numpy as jnp
import numpy as np
from jax import lax
from jax.experimental import pallas as pl
from jax.experimental.pallas import tpu as pltpu

F32 = jnp.float32
BF16 = jnp.bfloat16

EPS = 1e-6
POOL_WINDOWS = (2, 4, 8, 16)
POOL_HIST = 16
SB_HEADS = 8
HEAD_DIM = 128
XA_HEADS = 4
CONV_W = 3
CONV_HIST = 8
LANES = 128

SB_DEAD_LOG = -88.0

VMEM_LIMIT = 56 * 1024 * 1024


def _rmsnorm(xf, g):
    ms = jnp.mean(xf * xf, axis=-1, keepdims=True)
    return xf * lax.rsqrt(ms + EPS) * g


def _const_spec(shape):
    zeros = (0,) * len(shape)
    return pl.BlockSpec(shape, lambda *_: zeros, pipeline_mode=pl.Buffered(1))


def _params(semantics):
    return pltpu.CompilerParams(dimension_semantics=semantics, vmem_limit_bytes=VMEM_LIMIT)


def _in_proj_kernel(x_ref, g_ref, w_ref, b_ref, u_ref, q_ref, kf_ref, kb_ref, vf_ref, vb_ref, gate_ref,
                    xn_ref, *, q_scale):
    j = pl.program_id(1)

    @pl.when(j == 0)
    def _():
        xn_ref[...] = _rmsnorm(x_ref[...], g_ref[...]).astype(BF16)

    acc = jnp.dot(xn_ref[...], w_ref[...], preferred_element_type=F32)

    @pl.when(j == 0)
    def _():
        u_ref[...] = acc

    @pl.when(j == 1)
    def _():
        q_ref[...] = (acc * q_scale).astype(BF16)

    @pl.when(j == 2)
    def _():
        kf_ref[...] = acc
        kb_ref[...] = acc.astype(BF16)

    @pl.when(j == 3)
    def _():
        vf_ref[...] = acc
        vb_ref[...] = acc.astype(BF16)

    @pl.when(j >= 4)
    def _():
        gate_ref[...] = jax.nn.sigmoid(acc + b_ref[...]).astype(BF16)


def _in_proj(x, g, w_in, b_gate):
    n, d = x.shape
    width = d // 2
    n_col = w_in.shape[1] // width
    tm = min(512, n)
    grid = (n // tm, n_col)
    row = lambda i, j: (i, 0)
    gate_col = lambda i, j: (i, jnp.maximum(j - 4, 0))
    out_shape = (
        jax.ShapeDtypeStruct((n, width), F32),
        jax.ShapeDtypeStruct((n, width), BF16),
        jax.ShapeDtypeStruct((n, width), F32),
        jax.ShapeDtypeStruct((n, width), BF16),
        jax.ShapeDtypeStruct((n, width), F32),
        jax.ShapeDtypeStruct((n, width), BF16),
        jax.ShapeDtypeStruct((n, 2 * d), BF16),
    )
    out_specs = tuple([pl.BlockSpec((tm, width), row)] * 6 + [pl.BlockSpec((tm, width), gate_col)])
    return pl.pallas_call(
        functools.partial(_in_proj_kernel, q_scale=1.0 / np.sqrt(HEAD_DIM)),
        grid=grid,
        in_specs=[
            pl.BlockSpec((tm, d), row),
            pl.BlockSpec((1, d), lambda i, j: (0, 0)),
            pl.BlockSpec((d, width), lambda i, j: (0, j)),
            pl.BlockSpec((1, width), lambda i, j: (0, jnp.maximum(j - 4, 0))),
        ],
        out_specs=out_specs,
        out_shape=out_shape,
        scratch_shapes=[pltpu.VMEM((tm, d), BF16)],
        compiler_params=_params(("arbitrary", "arbitrary")),
        name="in_proj",
    )(x, g, w_in, b_gate)


def _norm_matmul_kernel(x_ref, g_ref, w_ref, o_ref):
    xn = _rmsnorm(x_ref[...], g_ref[...]).astype(BF16)
    o_ref[...] = jnp.dot(xn, w_ref[...], preferred_element_type=F32)


def _norm_matmul(x, g, w):
    n, d = x.shape
    tm = min(256, n)
    return pl.pallas_call(
        _norm_matmul_kernel,
        grid=(n // tm,),
        in_specs=[pl.BlockSpec((tm, d), lambda i: (i, 0)), _const_spec((1, d)), _const_spec(w.shape)],
        out_specs=pl.BlockSpec((tm, w.shape[1]), lambda i: (i, 0)),
        out_shape=jax.ShapeDtypeStruct((n, w.shape[1]), F32),
        compiler_params=_params(("arbitrary",)),
        name="mem_kv",
    )(x, g, w)


def _sb_kernel(q_ref, k_ref, v_ref, tri_ref, o_ref, *, blk, n_sub, key_block_offset):
    step = pl.program_id(2)
    tri = tri_ref[...]
    row = lax.broadcasted_iota(jnp.int32, (blk, blk), 0)
    col = lax.broadcasted_iota(jnp.int32, (blk, blk), 1)
    causal = col < row

    for sub in range(n_sub):
        q = q_ref[0, sub * blk:(sub + 1) * blk, :]

        def visit(kb, acc, run, masked, q=q):
            start = pl.multiple_of(kb * blk, blk)
            k = k_ref[0, pl.ds(start, blk), :]
            v = v_ref[0, pl.ds(start, blk), :]
            z = lax.dot_general(q, k, (((1,), (1,)), ((), ())), preferred_element_type=F32)
            t = jnp.log1p(jnp.exp(-jnp.abs(z)))
            log_not = -(jnp.maximum(z, 0.0) + t)
            log_beta = jnp.minimum(z, 0.0) - t
            if masked:
                log_not = jnp.where(causal, log_not, 0.0)
            hi = log_not.astype(BF16)
            lo = (log_not - hi.astype(F32)).astype(BF16)
            sums = (jnp.dot(hi, tri, preferred_element_type=F32)
                    + jnp.dot(lo, tri, preferred_element_type=F32))
            between = sums[:, :blk]
            total = sums[:, blk:]
            a = jnp.exp(log_beta + between + run[:, :blk])
            if masked:
                a = jnp.where(causal, a, 0.0)
            acc = acc + jnp.dot(a.astype(BF16), v, preferred_element_type=F32)
            return acc, run + total

        diag = step * n_sub + sub + key_block_offset
        zeros = jnp.zeros((blk, LANES), F32)
        acc, run = visit(diag, zeros, zeros, True)

        def alive(carry):
            kb, _, run = carry
            return jnp.logical_and(kb >= 0, jnp.max(run) > SB_DEAD_LOG)

        def older(carry):
            kb, acc, run = carry
            acc, run = visit(kb, acc, run, False)
            return kb - 1, acc, run

        _, acc, _ = lax.while_loop(alive, older, (diag - 1, acc, run))
        o_ref[0, sub * blk:(sub + 1) * blk, :] = acc.astype(BF16)


def _sb_attention(q, k, v):
    b, lq, _ = q.shape
    lk = k.shape[1]
    blk = min(128, lq)
    assert lq % blk == 0 and lk % blk == 0
    n_q = lq // blk
    n_sub = min(8, n_q)
    assert n_q % n_sub == 0
    jj = np.arange(blk)
    tri_np = np.concatenate([(jj[:, None] > jj[None, :]), np.ones((blk, LANES), bool)], axis=1)
    tri = jnp.asarray(tri_np, BF16)
    return pl.pallas_call(
        functools.partial(_sb_kernel, blk=blk, n_sub=n_sub, key_block_offset=(lk - lq) // blk),
        grid=(b, SB_HEADS, n_q // n_sub),
        in_specs=[
            pl.BlockSpec((1, n_sub * blk, HEAD_DIM), lambda bi, h, i: (bi, i, h)),
            pl.BlockSpec((1, lk, HEAD_DIM), lambda bi, h, i: (bi, 0, h)),
            pl.BlockSpec((1, lk, HEAD_DIM), lambda bi, h, i: (bi, 0, h)),
            _const_spec(tri.shape),
        ],
        out_specs=pl.BlockSpec((1, n_sub * blk, HEAD_DIM), lambda bi, h, i: (bi, i, h)),
        out_shape=jax.ShapeDtypeStruct(q.shape, BF16),
        compiler_params=_params(("arbitrary", "arbitrary", "arbitrary")),
        name="sb_attn",
    )(q, k, v, tri)


def _mix_kernel(x_ref, u_ref, ysb_ref, gp_ref, gs_ref, hist_ref, mk_ref, mv_ref,
                pool_w_ref, pool_scale_ref, wbp_ref, wbs_ref, wout_ref, gxa_ref, wq_ref, wo_ref,
                o_ref, ubuf_ref, *, tm, pos0):
    l = pl.program_id(1)
    group = u_ref.shape[2] // len(POOL_WINDOWS)

    @pl.when(l == 0)
    def _():
        ubuf_ref[0:POOL_HIST, :] = hist_ref[0]

    u = u_ref[0]
    ubuf_ref[POOL_HIST:POOL_HIST + tm, :] = u
    pos = pos0 + l * tm + lax.broadcasted_iota(jnp.int32, (tm, 1), 0)

    pooled = []
    for g, w in enumerate(POOL_WINDOWS):
        cols = slice(g * group, (g + 1) * group)
        window_sum = u[:, cols]
        for back in range(1, w):
            window_sum = window_sum + ubuf_ref[POOL_HIST - back:POOL_HIST - back + tm, cols]
        count = jnp.minimum(w, pos + 1).astype(F32)
        delta = (window_sum / count - u[:, cols]).astype(BF16)
        pooled.append(jnp.dot(delta, pool_w_ref[g], preferred_element_type=F32))
    y_pool = (jnp.concatenate(pooled, axis=-1) * pool_scale_ref[...]).astype(BF16)

    ubuf_ref[0:POOL_HIST, :] = ubuf_ref[tm:tm + POOL_HIST, :]

    branch_pool = jnp.dot(y_pool, wbp_ref[...], preferred_element_type=F32)
    branch_sb = jnp.dot(ysb_ref[0], wbs_ref[...], preferred_element_type=F32)
    merged = gp_ref[0].astype(F32) * branch_pool + gs_ref[0].astype(F32) * branch_sb
    x1 = x_ref[0] + jnp.dot(merged.astype(BF16), wout_ref[...], preferred_element_type=F32)

    xn = _rmsnorm(x1, gxa_ref[...]).astype(BF16)
    q = (jnp.dot(xn, wq_ref[...], preferred_element_type=F32) * (1.0 / np.sqrt(HEAD_DIM))).astype(BF16)
    heads = []
    for h in range(XA_HEADS):
        cols = slice(h * HEAD_DIM, (h + 1) * HEAD_DIM)
        s = lax.dot_general(q[:, cols], mk_ref[0, :, cols], (((1,), (1,)), ((), ())),
                            preferred_element_type=F32)
        p = jnp.exp(s - jnp.max(s, axis=-1, keepdims=True))
        p = p / jnp.sum(p, axis=-1, keepdims=True)
        heads.append(jnp.dot(p.astype(BF16), mv_ref[0, :, cols], preferred_element_type=F32))
    attn = jnp.concatenate(heads, axis=-1).astype(BF16)
    o_ref[0] = x1 + jnp.dot(attn, wo_ref[...], preferred_element_type=F32)


def _mix(x, u, ysb, gates, hist, mk, mv, pool_w, pool_scale, wbp, wbs, wout, gxa, wq, wo, pos0):
    b, l, d = x.shape
    tm = min(256, l)
    tile = lambda bi, li: (bi, li, 0)
    per_batch = lambda bi, li: (bi, 0, 0)
    return pl.pallas_call(
        functools.partial(_mix_kernel, tm=tm, pos0=pos0),
        grid=(b, l // tm),
        in_specs=[
            pl.BlockSpec((1, tm, d), tile),
            pl.BlockSpec((1, tm, u.shape[2]), tile),
            pl.BlockSpec((1, tm, ysb.shape[2]), tile),
            pl.BlockSpec((1, tm, d), tile),
            pl.BlockSpec((1, tm, d), lambda bi, li: (bi, li, 1)),
            pl.BlockSpec((1,) + hist.shape[1:], per_batch),
            pl.BlockSpec((1,) + mk.shape[1:], per_batch),
            pl.BlockSpec((1,) + mv.shape[1:], per_batch),
            _const_spec(pool_w.shape), _const_spec(pool_scale.shape), _const_spec(wbp.shape),
            _const_spec(wbs.shape), _const_spec(wout.shape), _const_spec(gxa.shape),
            _const_spec(wq.shape), _const_spec(wo.shape),
        ],
        out_specs=pl.BlockSpec((1, tm, d), tile),
        out_shape=jax.ShapeDtypeStruct(x.shape, F32),
        scratch_shapes=[pltpu.VMEM((POOL_HIST + tm, u.shape[2]), F32)],
        compiler_params=_params(("arbitrary", "arbitrary")),
        name="mix",
    )(x, u, ysb, gates, gates, hist, mk, mv, pool_w, pool_scale, wbp, wbs, wout, gxa, wq, wo)


def _ffn_kernel(x_ref, g_ref, wg_ref, wv_ref, cwg_ref, cwv_ref, cbg_ref, cbv_ref, wd_ref, sg_ref, sv_ref,
                gf_ref, y_ref, ng_ref, nv_ref, xn_ref, acc_ref, hg_ref, hv_ref, carry_ref,
                *, tm, tiles_per_seq):
    i = pl.program_id(0)
    j = pl.program_id(1)
    first_of_seq = (i % tiles_per_seq) == 0

    @pl.when(j == 0)
    def _():
        xn_ref[...] = _rmsnorm(x_ref[...], g_ref[...]).astype(BF16)
        acc_ref[...] = jnp.zeros_like(acc_ref)

    def conv_half(w_ref, cw_ref, cb_ref, state_ref, new_ref, h_ref, half):
        h = jnp.dot(xn_ref[...], w_ref[...], preferred_element_type=F32)
        h_ref[CONV_HIST:CONV_HIST + tm, :] = h

        @pl.when(first_of_seq)
        def _():
            h_ref[CONV_HIST - (CONV_W - 1):CONV_HIST, :] = state_ref[0]

        @pl.when(jnp.logical_not(first_of_seq))
        def _():
            h_ref[0:CONV_HIST, :] = carry_ref[half, j]

        out = cw_ref[2:3, :] * h
        for tap in range(CONV_W - 1):
            lag = CONV_W - 1 - tap
            out = out + cw_ref[tap:tap + 1, :] * h_ref[CONV_HIST - lag:CONV_HIST - lag + tm, :]
        carry_ref[half, j] = h_ref[tm:tm + CONV_HIST, :]
        new_ref[0] = h[tm - (CONV_W - 1):, :]
        return out + cb_ref[...]

    gate = conv_half(wg_ref, cwg_ref, cbg_ref, sg_ref, ng_ref, hg_ref, 0)
    val = conv_half(wv_ref, cwv_ref, cbv_ref, sv_ref, nv_ref, hv_ref, 1)
    act = (gate * jax.nn.sigmoid(gate) * val).astype(BF16)
    acc_ref[...] += jnp.dot(act, wd_ref[...], preferred_element_type=F32)

    @pl.when(j == pl.num_programs(1) - 1)
    def _():
        y_ref[...] = _rmsnorm(x_ref[...] + acc_ref[...], gf_ref[...])


def _conv_ffn(x, seq_len, g, w_up, conv_w, conv_b, w_down, state, g_final):
    n, d = x.shape
    f = w_down.shape[0]
    b = n // seq_len
    tm = min(512, seq_len)
    tf = 512
    assert f % tf == 0 and seq_len % tm == 0
    n_f = f // tf
    tiles_per_seq = seq_len // tm
    lo = lambda i, j: (0, j)
    hi = lambda i, j: (0, n_f + j)
    st_lo = lambda i, j: (i // tiles_per_seq, 0, j)
    st_hi = lambda i, j: (i // tiles_per_seq, 0, n_f + j)
    return pl.pallas_call(
        functools.partial(_ffn_kernel, tm=tm, tiles_per_seq=tiles_per_seq),
        grid=(n // tm, n_f),
        in_specs=[
            pl.BlockSpec((tm, d), lambda i, j: (i, 0)),
            pl.BlockSpec((1, d), lambda i, j: (0, 0)),
            pl.BlockSpec((d, tf), lo), pl.BlockSpec((d, tf), hi),
            pl.BlockSpec((CONV_W, tf), lo), pl.BlockSpec((CONV_W, tf), hi),
            pl.BlockSpec((1, tf), lo), pl.BlockSpec((1, tf), hi),
            pl.BlockSpec((tf, d), lambda i, j: (j, 0)),
            pl.BlockSpec((1, CONV_W - 1, tf), st_lo), pl.BlockSpec((1, CONV_W - 1, tf), st_hi),
            pl.BlockSpec((1, d), lambda i, j: (0, 0)),
        ],
        out_specs=(
            pl.BlockSpec((tm, d), lambda i, j: (i, 0)),
            pl.BlockSpec((1, CONV_W - 1, tf), st_lo),
            pl.BlockSpec((1, CONV_W - 1, tf), st_lo),
        ),
        out_shape=(
            jax.ShapeDtypeStruct((n, d), F32),
            jax.ShapeDtypeStruct((b, CONV_W - 1, f), F32),
            jax.ShapeDtypeStruct((b, CONV_W - 1, f), F32),
        ),
        scratch_shapes=[
            pltpu.VMEM((tm, d), BF16),
            pltpu.VMEM((tm, d), F32),
            pltpu.VMEM((CONV_HIST + tm, tf), F32),
            pltpu.VMEM((CONV_HIST + tm, tf), F32),
            pltpu.VMEM((2, n_f, CONV_HIST, tf), F32),
        ],
        compiler_params=_params(("arbitrary", "arbitrary")),
        name="conv_ffn",
    )(x, g, w_up, w_up, conv_w, conv_w, conv_b, conv_b, w_down, state, state, g_final)


def _layer(x, pool_state, past_k, past_v, mk, mv, conv_state, p):
    b, l, d = x.shape
    n = b * l
    assert l >= POOL_HIST and l >= CONV_W - 1
    u, q, kf, kb, vf, vb, gates = _in_proj(x.reshape(n, d), p["norm_mix_g"], p["w_in"], p["b_gate"])
    width = u.shape[1]
    q = q.reshape(b, l, width)
    kb = kb.reshape(b, l, width)
    vb = vb.reshape(b, l, width)
    if past_k is None:
        past = 0
        k_all, v_all = kb, vb
        hist = jnp.zeros((b, POOL_HIST, width), F32)
    else:
        past = past_k.shape[1]
        k_all = jnp.concatenate([past_k.reshape(b, past, width).astype(BF16), kb], axis=1)
        v_all = jnp.concatenate([past_v.reshape(b, past, width).astype(BF16), vb], axis=1)
        hist = jnp.pad(pool_state, ((0, 0), (POOL_HIST - pool_state.shape[1], 0), (0, 0)))
    y_sb = _sb_attention(q, k_all, v_all)

    x2 = _mix(x, u.reshape(b, l, width), y_sb, gates.reshape(b, l, 2 * d), hist, mk, mv,
              p["pool_w"], p["pool_scale"], p["w_branch_pool"], p["w_branch_sb"], p["w_out"],
              p["norm_xa_g"], p["xa_wq"], p["xa_wo"], past)

    y, new_g, new_v = _conv_ffn(x2.reshape(n, d), l, p["norm_ffn_g"], p["ffn_w_up"], p["ffn_conv_w"],
                                p["ffn_conv_b"], p["ffn_w_down"], conv_state, p["norm_final_g"])
    new_pool = u.reshape(b, l, width)[:, l - (POOL_HIST - 1):, :]
    new_conv = jnp.concatenate([new_g, new_v], axis=-1)
    return (y.reshape(b, l, d), new_pool, kf.reshape(b, l, SB_HEADS, HEAD_DIM),
            vf.reshape(b, l, SB_HEADS, HEAD_DIM), new_conv)


def kernel(x_prompt, x_sample, mem_prompt, state_pool, cache_sb_k, cache_sb_v, cache_mem_k, cache_mem_v, state_ffn_conv, norm_mix_g, w_in, b_gate, pool_w, pool_scale, w_branch_pool, w_branch_sb, w_out, norm_xa_g, norm_mem_g, xa_wq, xa_wkv, xa_wo, norm_ffn_g, ffn_w_up, ffn_conv_w, ffn_conv_b, ffn_w_down, norm_final_g):
    assert norm_mix_g.shape[0] == 1, "single-layer step"
    bp, lp, d = x_prompt.shape
    n_mem = mem_prompt.shape[1]
    xa_width = xa_wq.shape[2]
    row = lambda a: a.reshape(1, -1)
    p = {
        "norm_mix_g": row(norm_mix_g[0]), "w_in": w_in[0].astype(BF16), "b_gate": row(b_gate[0]),
        "pool_w": pool_w[0].astype(BF16), "pool_scale": row(pool_scale[0]),
        "w_branch_pool": w_branch_pool[0].astype(BF16), "w_branch_sb": w_branch_sb[0].astype(BF16),
        "w_out": w_out[0].astype(BF16), "norm_xa_g": row(norm_xa_g[0]),
        "xa_wq": xa_wq[0].astype(BF16), "xa_wo": xa_wo[0].astype(BF16),
        "norm_ffn_g": row(norm_ffn_g[0]), "ffn_w_up": ffn_w_up[0].astype(BF16),
        "ffn_conv_w": ffn_conv_w[0], "ffn_conv_b": row(ffn_conv_b[0]),
        "ffn_w_down": ffn_w_down[0].astype(BF16), "norm_final_g": row(norm_final_g),
    }

    mem_kv = _norm_matmul(mem_prompt.reshape(bp * n_mem, d), row(norm_mem_g[0]), xa_wkv[0].astype(BF16))
    mem_kv = mem_kv.reshape(bp, n_mem, 2 * xa_width)
    mk_p, mv_p = mem_kv[..., :xa_width], mem_kv[..., xa_width:]

    conv0 = jnp.zeros((bp, CONV_W - 1, ffn_w_up.shape[2]), F32)
    y_p, pool_p, k_p, v_p, conv_p = _layer(x_prompt, None, None, None, mk_p.astype(BF16), mv_p.astype(BF16),
                                           conv0, p)

    bs = x_sample.shape[0]
    mk_s = cache_mem_k[0].reshape(bs, n_mem, xa_width).astype(BF16)
    mv_s = cache_mem_v[0].reshape(bs, n_mem, xa_width).astype(BF16)
    y_s, pool_s, k_s, v_s, conv_s = _layer(x_sample, state_pool[0], cache_sb_k[0], cache_sb_v[0], mk_s, mv_s,
                                           state_ffn_conv[0], p)

    mem_shape = (1, bp, n_mem, XA_HEADS, HEAD_DIM)
    return (y_p, y_s,
            pool_p[None], k_p[None], v_p[None], mk_p.reshape(mem_shape), mv_p.reshape(mem_shape), conv_p[None],
            pool_s[None], k_s[None], v_s[None], conv_s[None])
```

```python
import functools

import jax
import jax.numpy as jnp
import numpy as np
from jax import lax
from jax.experimental import pallas as pl
from jax.experimental.pallas import tpu as pltpu

F32 = jnp.float32
BF16 = jnp.bfloat16

EPS = 1e-6
POOL_WINDOWS = (2, 4, 8, 16)
POOL_HIST = 16
SB_HEADS = 8
HEAD_DIM = 128
XA_HEADS = 4
CONV_W = 3
CONV_HIST = 8
LANES = 128

SB_DEAD_LOG = -88.0

VMEM_LIMIT = 56 * 1024 * 1024


def _rmsnorm(xf, g):
    ms = jnp.mean(xf * xf, axis=-1, keepdims=True)
    return xf * lax.rsqrt(ms + EPS) * g


def _const_spec(shape):
    zeros = (0,) * len(shape)
    return pl.BlockSpec(shape, lambda *_: zeros, pipeline_mode=pl.Buffered(1))


def _params(semantics):
    return pltpu.CompilerParams(dimension_semantics=semantics, vmem_limit_bytes=VMEM_LIMIT)


def _in_proj_kernel(x_ref, g_ref, w_ref, b_ref, u_ref, q_ref, kf_ref, kb_ref, vf_ref, vb_ref, gate_ref,
                    xn_ref, *, q_scale):
    j = pl.program_id(1)

    @pl.when(j == 0)
    def _():
        xn_ref[...] = _rmsnorm(x_ref[...], g_ref[...]).astype(BF16)

    acc = jnp.dot(xn_ref[...], w_ref[...], preferred_element_type=F32)

    @pl.when(j == 0)
    def _():
        u_ref[...] = acc

    @pl.when(j == 1)
    def _():
        q_ref[...] = (acc * q_scale).astype(BF16)

    @pl.when(j == 2)
    def _():
        kf_ref[...] = acc
        kb_ref[...] = acc.astype(BF16)

    @pl.when(j == 3)
    def _():
        vf_ref[...] = acc
        vb_ref[...] = acc.astype(BF16)

    @pl.when(j >= 4)
    def _():
        gate_ref[...] = jax.nn.sigmoid(acc + b_ref[...]).astype(BF16)


def _in_proj(x, g, w_in, b_gate):
    n, d = x.shape
    width = d // 2
    n_col = w_in.shape[1] // width
    tm = min(512, n)
    grid = (n // tm, n_col)
    row = lambda i, j: (i, 0)
    gate_col = lambda i, j: (i, jnp.maximum(j - 4, 0))
    out_shape = (
        jax.ShapeDtypeStruct((n, width), F32),
        jax.ShapeDtypeStruct((n, width), BF16),
        jax.ShapeDtypeStruct((n, width), F32),
        jax.ShapeDtypeStruct((n, width), BF16),
        jax.ShapeDtypeStruct((n, width), F32),
        jax.ShapeDtypeStruct((n, width), BF16),
        jax.ShapeDtypeStruct((n, 2 * d), BF16),
    )
    out_specs = tuple([pl.BlockSpec((tm, width), row)] * 6 + [pl.BlockSpec((tm, width), gate_col)])
    return pl.pallas_call(
        functools.partial(_in_proj_kernel, q_scale=1.0 / np.sqrt(HEAD_DIM)),
        grid=grid,
        in_specs=[
            pl.BlockSpec((tm, d), row),
            pl.BlockSpec((1, d), lambda i, j: (0, 0)),
            pl.BlockSpec((d, width), lambda i, j: (0, j)),
            pl.BlockSpec((1, width), lambda i, j: (0, jnp.maximum(j - 4, 0))),
        ],
        out_specs=out_specs,
        out_shape=out_shape,
        scratch_shapes=[pltpu.VMEM((tm, d), BF16)],
        compiler_params=_params(("arbitrary", "arbitrary")),
        name="in_proj",
    )(x, g, w_in, b_gate)


def _norm_matmul_kernel(x_ref, g_ref, w_ref, o_ref):
    xn = _rmsnorm(x_ref[...], g_ref[...]).astype(BF16)
    o_ref[...] = jnp.dot(xn, w_ref[...], preferred_element_type=F32)


def _norm_matmul(x, g, w):
    n, d = x.shape
    tm = min(256, n)
    return pl.pallas_call(
        _norm_matmul_kernel,
        grid=(n // tm,),
        in_specs=[pl.BlockSpec((tm, d), lambda i: (i, 0)), _const_spec((1, d)), _const_spec(w.shape)],
        out_specs=pl.BlockSpec((tm, w.shape[1]), lambda i: (i, 0)),
        out_shape=jax.ShapeDtypeStruct((n, w.shape[1]), F32),
        compiler_params=_params(("arbitrary",)),
        name="mem_kv",
    )(x, g, w)


def _sb_kernel(q_ref, k_ref, v_ref, tri_ref, o_ref, *, blk, n_sub, n_heads, key_block_offset):
    step = pl.program_id(2)
    tri = tri_ref[...]
    row = lax.broadcasted_iota(jnp.int32, (blk, blk), 0)
    col = lax.broadcasted_iota(jnp.int32, (blk, blk), 1)
    causal = col < row
    chains = [(sub, h) for sub in range(n_sub) for h in range(n_heads)]
    rows = lambda sub: slice(sub * blk, (sub + 1) * blk)
    lanes = lambda h: slice(h * HEAD_DIM, (h + 1) * HEAD_DIM)
    qs = [q_ref[0, rows(sub), lanes(h)] for sub, h in chains]
    first_diag = step * n_sub + key_block_offset

    def visit(back, accs, runs, masked):
        log_nots, log_betas, vs = [], [], []
        runs = list(runs)
        for c, (sub, h) in enumerate(chains):
            kb = first_diag + sub - back
            if not masked:
                runs[c] = jnp.where(kb >= 0, runs[c], -1e30)
                kb = jnp.maximum(kb, 0)
            start = pl.multiple_of(kb * blk, blk)
            k = k_ref[0, pl.ds(start, blk), lanes(h)]
            vs.append(v_ref[0, pl.ds(start, blk), lanes(h)])
            z = lax.dot_general(qs[c], k, (((1,), (1,)), ((), ())), preferred_element_type=F32)
            t = jnp.log1p(jnp.exp(-jnp.abs(z)))
            log_not = -(jnp.maximum(z, 0.0) + t)
            if masked:
                log_not = jnp.where(causal, log_not, 0.0)
            log_nots.append(log_not)
            log_betas.append(jnp.minimum(z, 0.0) - t)
        stacked = jnp.concatenate(log_nots, axis=0)
        hi = stacked.astype(BF16)
        lo = (stacked - hi.astype(F32)).astype(BF16)
        sums = (jnp.dot(hi, tri, preferred_element_type=F32)
                + jnp.dot(lo, tri, preferred_element_type=F32))
        new_accs, new_runs = [], []
        for c in range(len(chains)):
            part = sums[c * blk:(c + 1) * blk]
            between = part[:, :blk]
            total = part[:, blk:]
            a = jnp.exp(log_betas[c] + between + runs[c][:, :blk])
            if masked:
                a = jnp.where(causal, a, 0.0)
            new_accs.append(accs[c] + jnp.dot(a.astype(BF16), vs[c], preferred_element_type=F32))
            new_runs.append(runs[c] + total)
        return tuple(new_accs), tuple(new_runs)

    zeros = tuple(jnp.zeros((blk, LANES), F32) for _ in chains)
    accs, runs = visit(0, zeros, zeros, True)

    def alive(carry):
        back, _, runs = carry
        live = functools.reduce(jnp.maximum, runs)
        return jnp.logical_and(first_diag + n_sub - 1 - back >= 0, jnp.max(live) > SB_DEAD_LOG)

    def older(carry):
        back, accs, runs = carry
        accs, runs = visit(back, accs, runs, False)
        return back + 1, accs, runs

    _, accs, _ = lax.while_loop(alive, older, (jnp.int32(1), accs, runs))
    for c, (sub, h) in enumerate(chains):
        o_ref[0, rows(sub), lanes(h)] = accs[c].astype(BF16)


def _sb_attention(q, k, v):
    b, lq, width = q.shape
    lk = k.shape[1]
    blk = min(128, lq)
    assert lq % blk == 0 and lk % blk == 0
    n_q = lq // blk
    n_chains = 8
    n_sub = min(n_chains, n_q)
    n_heads = n_chains // n_sub
    assert n_q % n_sub == 0 and SB_HEADS % n_heads == 0
    jj = np.arange(blk)
    tri_np = np.concatenate([(jj[:, None] > jj[None, :]), np.ones((blk, LANES), bool)], axis=1)
    tri = jnp.asarray(tri_np, BF16)
    lane_w = n_heads * HEAD_DIM
    return pl.pallas_call(
        functools.partial(_sb_kernel, blk=blk, n_sub=n_sub, n_heads=n_heads,
                          key_block_offset=(lk - lq) // blk),
        grid=(b, SB_HEADS // n_heads, n_q // n_sub),
        in_specs=[
            pl.BlockSpec((1, n_sub * blk, lane_w), lambda bi, h, i: (bi, i, h)),
            pl.BlockSpec((1, lk, lane_w), lambda bi, h, i: (bi, 0, h)),
            pl.BlockSpec((1, lk, lane_w), lambda bi, h, i: (bi, 0, h)),
            _const_spec(tri.shape),
        ],
        out_specs=pl.BlockSpec((1, n_sub * blk, lane_w), lambda bi, h, i: (bi, i, h)),
        out_shape=jax.ShapeDtypeStruct(q.shape, BF16),
        compiler_params=_params(("arbitrary", "arbitrary", "arbitrary")),
        name="sb_attn",
    )(q, k, v, tri)


def _mix_kernel(x_ref, u_ref, ysb_ref, gp_ref, gs_ref, hist_ref, mk_ref, mv_ref,
                pool_w_ref, pool_scale_ref, wbp_ref, wbs_ref, wout_ref, gxa_ref, wq_ref, wo_ref,
                o_ref, ubuf_ref, *, tm, pos0):
    l = pl.program_id(1)
    group = u_ref.shape[2] // len(POOL_WINDOWS)

    @pl.when(l == 0)
    def _():
        ubuf_ref[0:POOL_HIST, :] = hist_ref[0]

    u = u_ref[0]
    ubuf_ref[POOL_HIST:POOL_HIST + tm, :] = u
    pos = pos0 + l * tm + lax.broadcasted_iota(jnp.int32, (tm, 1), 0)

    pooled = []
    for g, w in enumerate(POOL_WINDOWS):
        cols = slice(g * group, (g + 1) * group)
        window_sum = u[:, cols]
        for back in range(1, w):
            window_sum = window_sum + ubuf_ref[POOL_HIST - back:POOL_HIST - back + tm, cols]
        count = jnp.minimum(w, pos + 1).astype(F32)
        delta = (window_sum / count - u[:, cols]).astype(BF16)
        pooled.append(jnp.dot(delta, pool_w_ref[g], preferred_element_type=F32))
    y_pool = (jnp.concatenate(pooled, axis=-1) * pool_scale_ref[...]).astype(BF16)

    ubuf_ref[0:POOL_HIST, :] = ubuf_ref[tm:tm + POOL_HIST, :]

    branch_pool = jnp.dot(y_pool, wbp_ref[...], preferred_element_type=F32)
    branch_sb = jnp.dot(ysb_ref[0], wbs_ref[...], preferred_element_type=F32)
    merged = gp_ref[0].astype(F32) * branch_pool + gs_ref[0].astype(F32) * branch_sb
    x1 = x_ref[0] + jnp.dot(merged.astype(BF16), wout_ref[...], preferred_element_type=F32)

    xn = _rmsnorm(x1, gxa_ref[...]).astype(BF16)
    q = (jnp.dot(xn, wq_ref[...], preferred_element_type=F32) * (1.0 / np.sqrt(HEAD_DIM))).astype(BF16)
    heads = []
    for h in range(XA_HEADS):
        cols = slice(h * HEAD_DIM, (h + 1) * HEAD_DIM)
        s = lax.dot_general(q[:, cols], mk_ref[0, :, cols], (((1,), (1,)), ((), ())),
                            preferred_element_type=F32)
        p = jnp.exp(s - jnp.max(s, axis=-1, keepdims=True))
        p = p / jnp.sum(p, axis=-1, keepdims=True)
        heads.append(jnp.dot(p.astype(BF16), mv_ref[0, :, cols], preferred_element_type=F32))
    attn = jnp.concatenate(heads, axis=-1).astype(BF16)
    o_ref[0] = x1 + jnp.dot(attn, wo_ref[...], preferred_element_type=F32)


def _mix(x, u, ysb, gates, hist, mk, mv, pool_w, pool_scale, wbp, wbs, wout, gxa, wq, wo, pos0):
    b, l, d = x.shape
    tm = min(256, l)
    tile = lambda bi, li: (bi, li, 0)
    per_batch = lambda bi, li: (bi, 0, 0)
    return pl.pallas_call(
        functools.partial(_mix_kernel, tm=tm, pos0=pos0),
        grid=(b, l // tm),
        in_specs=[
            pl.BlockSpec((1, tm, d), tile),
            pl.BlockSpec((1, tm, u.shape[2]), tile),
            pl.BlockSpec((1, tm, ysb.shape[2]), tile),
            pl.BlockSpec((1, tm, d), tile),
            pl.BlockSpec((1, tm, d), lambda bi, li: (bi, li, 1)),
            pl.BlockSpec((1,) + hist.shape[1:], per_batch),
            pl.BlockSpec((1,) + mk.shape[1:], per_batch),
            pl.BlockSpec((1,) + mv.shape[1:], per_batch),
            _const_spec(pool_w.shape), _const_spec(pool_scale.shape), _const_spec(wbp.shape),
            _const_spec(wbs.shape), _const_spec(wout.shape), _const_spec(gxa.shape),
            _const_spec(wq.shape), _const_spec(wo.shape),
        ],
        out_specs=pl.BlockSpec((1, tm, d), tile),
        out_shape=jax.ShapeDtypeStruct(x.shape, F32),
        scratch_shapes=[pltpu.VMEM((POOL_HIST + tm, u.shape[2]), F32)],
        compiler_params=_params(("arbitrary", "arbitrary")),
        name="mix",
    )(x, u, ysb, gates, gates, hist, mk, mv, pool_w, pool_scale, wbp, wbs, wout, gxa, wq, wo)


def _ffn_kernel(x_ref, g_ref, wg_ref, wv_ref, cwg_ref, cwv_ref, cbg_ref, cbv_ref, wd_ref, sg_ref, sv_ref,
                gf_ref, y_ref, ng_ref, nv_ref, xn_ref, acc_ref, hg_ref, hv_ref, carry_ref,
                *, tm, tiles_per_seq):
    i = pl.program_id(0)
    j = pl.program_id(1)
    first_of_seq = (i % tiles_per_seq) == 0

    @pl.when(j == 0)
    def _():
        xn_ref[...] = _rmsnorm(x_ref[...], g_ref[...]).astype(BF16)
        acc_ref[...] = jnp.zeros_like(acc_ref)

    def conv_half(w_ref, cw_ref, cb_ref, state_ref, new_ref, h_ref, half):
        h = jnp.dot(xn_ref[...], w_ref[...], preferred_element_type=F32)
        h_ref[CONV_HIST:CONV_HIST + tm, :] = h

        @pl.when(first_of_seq)
        def _():
            h_ref[CONV_HIST - (CONV_W - 1):CONV_HIST, :] = state_ref[0]

        @pl.when(jnp.logical_not(first_of_seq))
        def _():
            h_ref[0:CONV_HIST, :] = carry_ref[half, j]

        out = cw_ref[2:3, :] * h
        for tap in range(CONV_W - 1):
            lag = CONV_W - 1 - tap
            out = out + cw_ref[tap:tap + 1, :] * h_ref[CONV_HIST - lag:CONV_HIST - lag + tm, :]
        carry_ref[half, j] = h_ref[tm:tm + CONV_HIST, :]
        new_ref[0] = h[tm - (CONV_W - 1):, :]
        return out + cb_ref[...]

    gate = conv_half(wg_ref, cwg_ref, cbg_ref, sg_ref, ng_ref, hg_ref, 0)
    val = conv_half(wv_ref, cwv_ref, cbv_ref, sv_ref, nv_ref, hv_ref, 1)
    act = (gate * jax.nn.sigmoid(gate) * val).astype(BF16)
    acc_ref[...] += jnp.dot(act, wd_ref[...], preferred_element_type=F32)

    @pl.when(j == pl.num_programs(1) - 1)
    def _():
        y_ref[...] = _rmsnorm(x_ref[...] + acc_ref[...], gf_ref[...])


def _conv_ffn(x, seq_len, g, w_up, conv_w, conv_b, w_down, state, g_final):
    n, d = x.shape
    f = w_down.shape[0]
    b = n // seq_len
    tm = min(512, seq_len)
    tf = 512
    assert f % tf == 0 and seq_len % tm == 0
    n_f = f // tf
    tiles_per_seq = seq_len // tm
    lo = lambda i, j: (0, j)
    hi = lambda i, j: (0, n_f + j)
    st_lo = lambda i, j: (i // tiles_per_seq, 0, j)
    st_hi = lambda i, j: (i // tiles_per_seq, 0, n_f + j)
    y, tail_g, tail_v = pl.pallas_call(
        functools.partial(_ffn_kernel, tm=tm, tiles_per_seq=tiles_per_seq),
        grid=(n // tm, n_f),
        in_specs=[
            pl.BlockSpec((tm, d), lambda i, j: (i, 0)),
            pl.BlockSpec((1, d), lambda i, j: (0, 0)),
            pl.BlockSpec((d, tf), lo), pl.BlockSpec((d, tf), hi),
            pl.BlockSpec((CONV_W, tf), lo), pl.BlockSpec((CONV_W, tf), hi),
            pl.BlockSpec((1, tf), lo), pl.BlockSpec((1, tf), hi),
            pl.BlockSpec((tf, d), lambda i, j: (j, 0)),
            pl.BlockSpec((1, CONV_W - 1, tf), st_lo), pl.BlockSpec((1, CONV_W - 1, tf), st_hi),
            pl.BlockSpec((1, d), lambda i, j: (0, 0)),
        ],
        out_specs=(
            pl.BlockSpec((tm, d), lambda i, j: (i, 0)),
            pl.BlockSpec((1, CONV_W - 1, tf), lambda i, j: (i, 0, j)),
            pl.BlockSpec((1, CONV_W - 1, tf), lambda i, j: (i, 0, j)),
        ),
        out_shape=(
            jax.ShapeDtypeStruct((n, d), F32),
            jax.ShapeDtypeStruct((n // tm, CONV_W - 1, f), F32),
            jax.ShapeDtypeStruct((n // tm, CONV_W - 1, f), F32),
        ),
        scratch_shapes=[
            pltpu.VMEM((tm, d), BF16),
            pltpu.VMEM((tm, d), F32),
            pltpu.VMEM((CONV_HIST + tm, tf), F32),
            pltpu.VMEM((CONV_HIST + tm, tf), F32),
            pltpu.VMEM((2, n_f, CONV_HIST, tf), F32),
        ],
        compiler_params=_params(("arbitrary", "arbitrary")),
        name="conv_ffn",
    )(x, g, w_up, w_up, conv_w, conv_w, conv_b, conv_b, w_down, state, state, g_final)
    last_tile = slice(tiles_per_seq - 1, None, tiles_per_seq)
    return y, tail_g[last_tile], tail_v[last_tile]


def _layer(x, pool_state, past_k, past_v, mk, mv, conv_state, p):
    b, l, d = x.shape
    n = b * l
    assert l >= POOL_HIST and l >= CONV_W - 1
    u, q, kf, kb, vf, vb, gates = _in_proj(x.reshape(n, d), p["norm_mix_g"], p["w_in"], p["b_gate"])
    width = u.shape[1]
    q = q.reshape(b, l, width)
    kb = kb.reshape(b, l, width)
    vb = vb.reshape(b, l, width)
    if past_k is None:
        past = 0
        k_all, v_all = kb, vb
        hist = jnp.zeros((b, POOL_HIST, width), F32)
    else:
        past = past_k.shape[1]
        k_all = jnp.concatenate([past_k.reshape(b, past, width).astype(BF16), kb], axis=1)
        v_all = jnp.concatenate([past_v.reshape(b, past, width).astype(BF16), vb], axis=1)
        hist = jnp.pad(pool_state, ((0, 0), (POOL_HIST - pool_state.shape[1], 0), (0, 0)))
    y_sb = _sb_attention(q, k_all, v_all)

    x2 = _mix(x, u.reshape(b, l, width), y_sb, gates.reshape(b, l, 2 * d), hist, mk, mv,
              p["pool_w"], p["pool_scale"], p["w_branch_pool"], p["w_branch_sb"], p["w_out"],
              p["norm_xa_g"], p["xa_wq"], p["xa_wo"], past)

    y, new_g, new_v = _conv_ffn(x2.reshape(n, d), l, p["norm_ffn_g"], p["ffn_w_up"], p["ffn_conv_w"],
                                p["ffn_conv_b"], p["ffn_w_down"], conv_state, p["norm_final_g"])
    new_pool = u.reshape(b, l, width)[:, l - (POOL_HIST - 1):, :]
    new_conv = jnp.concatenate([new_g, new_v], axis=-1)
    return (y.reshape(b, l, d), new_pool, kf.reshape(b, l, SB_HEADS, HEAD_DIM),
            vf.reshape(b, l, SB_HEADS, HEAD_DIM), new_conv)


def kernel(x_prompt, x_sample, mem_prompt, state_pool, cache_sb_k, cache_sb_v, cache_mem_k, cache_mem_v, state_ffn_conv, norm_mix_g, w_in, b_gate, pool_w, pool_scale, w_branch_pool, w_branch_sb, w_out, norm_xa_g, norm_mem_g, xa_wq, xa_wkv, xa_wo, norm_ffn_g, ffn_w_up, ffn_conv_w, ffn_conv_b, ffn_w_down, norm_final_g):
    assert norm_mix_g.shape[0] == 1, "single-layer step"
    bp, lp, d = x_prompt.shape
    n_mem = mem_prompt.shape[1]
    xa_width = xa_wq.shape[2]
    row = lambda a: a.reshape(1, -1)
    p = {
        "norm_mix_g": row(norm_mix_g[0]), "w_in": w_in[0].astype(BF16), "b_gate": row(b_gate[0]),
        "pool_w": pool_w[0].astype(BF16), "pool_scale": row(pool_scale[0]),
        "w_branch_pool": w_branch_pool[0].astype(BF16), "w_branch_sb": w_branch_sb[0].astype(BF16),
        "w_out": w_out[0].astype(BF16), "norm_xa_g": row(norm_xa_g[0]),
        "xa_wq": xa_wq[0].astype(BF16), "xa_wo": xa_wo[0].astype(BF16),
        "norm_ffn_g": row(norm_ffn_g[0]), "ffn_w_up": ffn_w_up[0].astype(BF16),
        "ffn_conv_w": ffn_conv_w[0], "ffn_conv_b": row(ffn_conv_b[0]),
        "ffn_w_down": ffn_w_down[0].astype(BF16), "norm_final_g": row(norm_final_g),
    }

    mem_kv = _norm_matmul(mem_prompt.reshape(bp * n_mem, d), row(norm_mem_g[0]), xa_wkv[0].astype(BF16))
    mem_kv = mem_kv.reshape(bp, n_mem, 2 * xa_width)
    mk_p, mv_p = mem_kv[..., :xa_width], mem_kv[..., xa_width:]

    conv0 = jnp.zeros((bp, CONV_W - 1, ffn_w_up.shape[2]), F32)
    y_p, pool_p, k_p, v_p, conv_p = _layer(x_prompt, None, None, None, mk_p.astype(BF16), mv_p.astype(BF16),
                                           conv0, p)

    bs = x_sample.shape[0]
    mk_s = cache_mem_k[0].reshape(bs, n_mem, xa_width).astype(BF16)
    mv_s = cache_mem_v[0].reshape(bs, n_mem, xa_width).astype(BF16)
    y_s, pool_s, k_s, v_s, conv_s = _layer(x_sample, state_pool[0], cache_sb_k[0], cache_sb_v[0], mk_s, mv_s,
                                           state_ffn_conv[0], p)

    mem_shape = (1, bp, n_mem, XA_HEADS, HEAD_DIM)
    return (y_p, y_s,
            pool_p[None], k_p[None], v_p[None], mk_p.reshape(mem_shape), mv_p.reshape(mem_shape), conv_p[None],
            pool_s[None], k_s[None], v_s[None], conv_s[None])
```

```python
import functools

import jax
import jax.numpy as jnp
import numpy as np
from jax import lax
from jax.experimental import pallas as pl
from jax.experimental.pallas import tpu as pltpu

F32 = jnp.float32
BF16 = jnp.bfloat16

EPS = 1e-6
POOL_WINDOWS = (2, 4, 8, 16)
POOL_HIST = 16
SB_HEADS = 8
HEAD_DIM = 128
XA_HEADS = 4
CONV_W = 3
CONV_HIST = 8
LANES = 128
ROW_CHUNK = 256

SB_DEAD_LOG = -88.0

VMEM_LIMIT = 56 * 1024 * 1024


def _rmsnorm(xf, g):
    ms = jnp.mean(xf * xf, axis=-1, keepdims=True)
    return xf * lax.rsqrt(ms + EPS) * g


def _const_spec(shape):
    zeros = (0,) * len(shape)
    return pl.BlockSpec(shape, lambda *_: zeros, pipeline_mode=pl.Buffered(1))


def _params(semantics):
    return pltpu.CompilerParams(dimension_semantics=semantics, vmem_limit_bytes=VMEM_LIMIT)


def _in_proj_kernel(x_ref, g_ref, w_ref, b_ref, u_ref, q_ref, kf_ref, kb_ref, vf_ref, vb_ref, gate_ref,
                    xn_ref, *, q_scale, chunk):
    j = pl.program_id(1)
    tm = x_ref.shape[0]

    @pl.when(j == 0)
    def _():
        xn_ref[...] = _rmsnorm(x_ref[...], g_ref[...]).astype(BF16)

    def project(epilogue):
        for r0 in range(0, tm, chunk):
            rows = slice(r0, r0 + chunk)
            epilogue(rows, jnp.dot(xn_ref[rows, :], w_ref[...], preferred_element_type=F32))

    def put_u(rows, acc):
        u_ref[rows, :] = acc

    def put_q(rows, acc):
        q_ref[rows, :] = (acc * q_scale).astype(BF16)

    def put_k(rows, acc):
        kf_ref[rows, :] = acc
        kb_ref[rows, :] = acc.astype(BF16)

    def put_v(rows, acc):
        vf_ref[rows, :] = acc
        vb_ref[rows, :] = acc.astype(BF16)

    def put_gate(rows, acc):
        gate_ref[rows, :] = jax.nn.sigmoid(acc + b_ref[...]).astype(BF16)

    pl.when(j == 0)(functools.partial(project, put_u))
    pl.when(j == 1)(functools.partial(project, put_q))
    pl.when(j == 2)(functools.partial(project, put_k))
    pl.when(j == 3)(functools.partial(project, put_v))
    pl.when(j >= 4)(functools.partial(project, put_gate))


def _in_proj(x, g, w_in, b_gate):
    n, d = x.shape
    width = d // 2
    n_col = w_in.shape[1] // width
    tm = min(512, n)
    grid = (n // tm, n_col)
    row = lambda i, j: (i, 0)
    gate_col = lambda i, j: (i, jnp.maximum(j - 4, 0))
    out_shape = (
        jax.ShapeDtypeStruct((n, width), F32),
        jax.ShapeDtypeStruct((n, width), BF16),
        jax.ShapeDtypeStruct((n, width), F32),
        jax.ShapeDtypeStruct((n, width), BF16),
        jax.ShapeDtypeStruct((n, width), F32),
        jax.ShapeDtypeStruct((n, width), BF16),
        jax.ShapeDtypeStruct((n, 2 * d), BF16),
    )
    out_specs = tuple([pl.BlockSpec((tm, width), row)] * 6 + [pl.BlockSpec((tm, width), gate_col)])
    return pl.pallas_call(
        functools.partial(_in_proj_kernel, q_scale=1.0 / np.sqrt(HEAD_DIM), chunk=min(ROW_CHUNK, tm)),
        grid=grid,
        in_specs=[
            pl.BlockSpec((tm, d), row),
            pl.BlockSpec((1, d), lambda i, j: (0, 0)),
            pl.BlockSpec((d, width), lambda i, j: (0, j)),
            pl.BlockSpec((1, width), lambda i, j: (0, jnp.maximum(j - 4, 0))),
        ],
        out_specs=out_specs,
        out_shape=out_shape,
        scratch_shapes=[pltpu.VMEM((tm, d), BF16)],
        compiler_params=_params(("arbitrary", "arbitrary")),
        name="in_proj",
    )(x, g, w_in, b_gate)


def _norm_matmul_kernel(x_ref, g_ref, w_ref, o_ref):
    xn = _rmsnorm(x_ref[...], g_ref[...]).astype(BF16)
    o_ref[...] = jnp.dot(xn, w_ref[...], preferred_element_type=F32)


def _norm_matmul(x, g, w):
    n, d = x.shape
    tm = min(256, n)
    return pl.pallas_call(
        _norm_matmul_kernel,
        grid=(n // tm,),
        in_specs=[pl.BlockSpec((tm, d), lambda i: (i, 0)), _const_spec((1, d)), _const_spec(w.shape)],
        out_specs=pl.BlockSpec((tm, w.shape[1]), lambda i: (i, 0)),
        out_shape=jax.ShapeDtypeStruct((n, w.shape[1]), F32),
        compiler_params=_params(("arbitrary",)),
        name="mem_kv",
    )(x, g, w)


def _sb_kernel(q_ref, k_ref, v_ref, tri_ref, o_ref, *, blk, n_sub, n_heads, key_block_offset):
    step = pl.program_id(2)
    tri = tri_ref[...]
    row = lax.broadcasted_iota(jnp.int32, (blk, blk), 0)
    col = lax.broadcasted_iota(jnp.int32, (blk, blk), 1)
    causal = col < row
    chains = [(sub, h) for sub in range(n_sub) for h in range(n_heads)]
    rows = lambda sub: slice(sub * blk, (sub + 1) * blk)
    lanes = lambda h: slice(h * HEAD_DIM, (h + 1) * HEAD_DIM)
    qs = [q_ref[0, rows(sub), lanes(h)] for sub, h in chains]
    first_diag = step * n_sub + key_block_offset

    def visit(back, accs, runs, masked):
        log_nots, log_betas, vs = [], [], []
        runs = list(runs)
        for c, (sub, h) in enumerate(chains):
            kb = first_diag + sub - back
            if not masked:
                runs[c] = jnp.where(kb >= 0, runs[c], -1e30)
                kb = jnp.maximum(kb, 0)
            start = pl.multiple_of(kb * blk, blk)
            k = k_ref[0, pl.ds(start, blk), lanes(h)]
            vs.append(v_ref[0, pl.ds(start, blk), lanes(h)])
            z = lax.dot_general(qs[c], k, (((1,), (1,)), ((), ())), preferred_element_type=F32)
            t = jnp.log1p(jnp.exp(-jnp.abs(z)))
            log_not = -(jnp.maximum(z, 0.0) + t)
            if masked:
                log_not = jnp.where(causal, log_not, 0.0)
            log_nots.append(log_not)
            log_betas.append(jnp.minimum(z, 0.0) - t)
        stacked = jnp.concatenate(log_nots, axis=0)
        hi = stacked.astype(BF16)
        lo = (stacked - hi.astype(F32)).astype(BF16)
        sums = (jnp.dot(hi, tri, preferred_element_type=F32)
                + jnp.dot(lo, tri, preferred_element_type=F32))
        new_accs, new_runs = [], []
        for c in range(len(chains)):
            part = sums[c * blk:(c + 1) * blk]
            between = part[:, :blk]
            total = part[:, blk:]
            a = jnp.exp(log_betas[c] + between + runs[c][:, :blk])
            if masked:
                a = jnp.where(causal, a, 0.0)
            new_accs.append(accs[c] + jnp.dot(a.astype(BF16), vs[c], preferred_element_type=F32))
            new_runs.append(runs[c] + total)
        return tuple(new_accs), tuple(new_runs)

    zeros = tuple(jnp.zeros((blk, LANES), F32) for _ in chains)
    accs, runs = visit(0, zeros, zeros, True)

    def alive(carry):
        back, _, runs = carry
        live = functools.reduce(jnp.maximum, runs)
        return jnp.logical_and(first_diag + n_sub - 1 - back >= 0, jnp.max(live) > SB_DEAD_LOG)

    def older(carry):
        back, accs, runs = carry
        accs, runs = visit(back, accs, runs, False)
        return back + 1, accs, runs

    _, accs, _ = lax.while_loop(alive, older, (jnp.int32(1), accs, runs))
    for c, (sub, h) in enumerate(chains):
        o_ref[0, rows(sub), lanes(h)] = accs[c].astype(BF16)


def _sb_attention(q, k, v):
    b, lq, width = q.shape
    lk = k.shape[1]
    blk = min(128, lq)
    assert lq % blk == 0 and lk % blk == 0
    n_q = lq // blk
    n_chains = 8
    n_sub = min(n_chains, n_q)
    n_heads = n_chains // n_sub
    assert n_q % n_sub == 0 and SB_HEADS % n_heads == 0
    jj = np.arange(blk)
    tri_np = np.concatenate([(jj[:, None] > jj[None, :]), np.ones((blk, LANES), bool)], axis=1)
    tri = jnp.asarray(tri_np, BF16)
    lane_w = n_heads * HEAD_DIM
    return pl.pallas_call(
        functools.partial(_sb_kernel, blk=blk, n_sub=n_sub, n_heads=n_heads,
                          key_block_offset=(lk - lq) // blk),
        grid=(b, SB_HEADS // n_heads, n_q // n_sub),
        in_specs=[
            pl.BlockSpec((1, n_sub * blk, lane_w), lambda bi, h, i: (bi, i, h)),
            pl.BlockSpec((1, lk, lane_w), lambda bi, h, i: (bi, 0, h)),
            pl.BlockSpec((1, lk, lane_w), lambda bi, h, i: (bi, 0, h)),
            _const_spec(tri.shape),
        ],
        out_specs=pl.BlockSpec((1, n_sub * blk, lane_w), lambda bi, h, i: (bi, i, h)),
        out_shape=jax.ShapeDtypeStruct(q.shape, BF16),
        compiler_params=_params(("arbitrary", "arbitrary", "arbitrary")),
        name="sb_attn",
    )(q, k, v, tri)


def _mix_kernel(x_ref, u_ref, ysb_ref, gp_ref, gs_ref, hist_ref, mk_ref, mv_ref,
                pool_w_ref, pool_scale_ref, wbp_ref, wbs_ref, wout_ref, gxa_ref, wq_ref, wo_ref,
                o_ref, ubuf_ref, *, tm, pos0):
    l = pl.program_id(1)
    group = u_ref.shape[2] // len(POOL_WINDOWS)

    @pl.when(l == 0)
    def _():
        ubuf_ref[0:POOL_HIST, :] = hist_ref[0]

    u = u_ref[0]
    ubuf_ref[POOL_HIST:POOL_HIST + tm, :] = u
    pos = pos0 + l * tm + lax.broadcasted_iota(jnp.int32, (tm, 1), 0)

    pooled = []
    for g, w in enumerate(POOL_WINDOWS):
        cols = slice(g * group, (g + 1) * group)
        window_sum = u[:, cols]
        for back in range(1, w):
            window_sum = window_sum + ubuf_ref[POOL_HIST - back:POOL_HIST - back + tm, cols]
        count = jnp.minimum(w, pos + 1).astype(F32)
        delta = (window_sum / count - u[:, cols]).astype(BF16)
        pooled.append(jnp.dot(delta, pool_w_ref[g], preferred_element_type=F32))
    y_pool = (jnp.concatenate(pooled, axis=-1) * pool_scale_ref[...]).astype(BF16)

    ubuf_ref[0:POOL_HIST, :] = ubuf_ref[tm:tm + POOL_HIST, :]

    branch_pool = jnp.dot(y_pool, wbp_ref[...], preferred_element_type=F32)
    branch_sb = jnp.dot(ysb_ref[0], wbs_ref[...], preferred_element_type=F32)
    merged = gp_ref[0].astype(F32) * branch_pool + gs_ref[0].astype(F32) * branch_sb
    x1 = x_ref[0] + jnp.dot(merged.astype(BF16), wout_ref[...], preferred_element_type=F32)

    xn = _rmsnorm(x1, gxa_ref[...]).astype(BF16)
    q = (jnp.dot(xn, wq_ref[...], preferred_element_type=F32) * (1.0 / np.sqrt(HEAD_DIM))).astype(BF16)
    heads = []
    for h in range(XA_HEADS):
        cols = slice(h * HEAD_DIM, (h + 1) * HEAD_DIM)
        s = lax.dot_general(q[:, cols], mk_ref[0, :, cols], (((1,), (1,)), ((), ())),
                            preferred_element_type=F32)
        p = jnp.exp(s - jnp.max(s, axis=-1, keepdims=True))
        p = p / jnp.sum(p, axis=-1, keepdims=True)
        heads.append(jnp.dot(p.astype(BF16), mv_ref[0, :, cols], preferred_element_type=F32))
    attn = jnp.concatenate(heads, axis=-1).astype(BF16)
    o_ref[0] = x1 + jnp.dot(attn, wo_ref[...], preferred_element_type=F32)


def _mix(x, u, ysb, gates, hist, mk, mv, pool_w, pool_scale, wbp, wbs, wout, gxa, wq, wo, pos0):
    b, l, d = x.shape
    tm = min(256, l)
    tile = lambda bi, li: (bi, li, 0)
    per_batch = lambda bi, li: (bi, 0, 0)
    return pl.pallas_call(
        functools.partial(_mix_kernel, tm=tm, pos0=pos0),
        grid=(b, l // tm),
        in_specs=[
            pl.BlockSpec((1, tm, d), tile),
            pl.BlockSpec((1, tm, u.shape[2]), tile),
            pl.BlockSpec((1, tm, ysb.shape[2]), tile),
            pl.BlockSpec((1, tm, d), tile),
            pl.BlockSpec((1, tm, d), lambda bi, li: (bi, li, 1)),
            pl.BlockSpec((1,) + hist.shape[1:], per_batch),
            pl.BlockSpec((1,) + mk.shape[1:], per_batch),
            pl.BlockSpec((1,) + mv.shape[1:], per_batch),
            _const_spec(pool_w.shape), _const_spec(pool_scale.shape), _const_spec(wbp.shape),
            _const_spec(wbs.shape), _const_spec(wout.shape), _const_spec(gxa.shape),
            _const_spec(wq.shape), _const_spec(wo.shape),
        ],
        out_specs=pl.BlockSpec((1, tm, d), tile),
        out_shape=jax.ShapeDtypeStruct(x.shape, F32),
        scratch_shapes=[pltpu.VMEM((POOL_HIST + tm, u.shape[2]), F32)],
        compiler_params=_params(("arbitrary", "arbitrary")),
        name="mix",
    )(x, u, ysb, gates, gates, hist, mk, mv, pool_w, pool_scale, wbp, wbs, wout, gxa, wq, wo)


def _ffn_kernel(x_ref, g_ref, wg_ref, wv_ref, cwg_ref, cwv_ref, cbg_ref, cbv_ref, wd_ref, sg_ref, sv_ref,
                gf_ref, y_ref, ng_ref, nv_ref, xn_ref, acc_ref, hg_ref, hv_ref, carry_ref,
                *, tm, seg, chunk, tiles_per_seq):
    i = pl.program_id(0)
    j = pl.program_id(1)
    piece = min(chunk, seg)
    lags = CONV_W - 1
    halves = ((wg_ref, cwg_ref, cbg_ref, sg_ref, ng_ref, hg_ref),
              (wv_ref, cwv_ref, cbv_ref, sv_ref, nv_ref, hv_ref))
    h_row = lambda r: r + CONV_HIST * (r // seg + 1)

    @pl.when(j == 0)
    def _():
        xn_ref[...] = _rmsnorm(x_ref[...], g_ref[...]).astype(BF16)
        acc_ref[...] = jnp.zeros_like(acc_ref)

    for half, (_, _, _, state_ref, _, h_ref) in enumerate(halves):
        if tiles_per_seq > 1:
            first_of_seq = (i % tiles_per_seq) == 0

            @pl.when(first_of_seq)
            def _(state_ref=state_ref, h_ref=h_ref):
                h_ref[CONV_HIST - lags:CONV_HIST, :] = state_ref[0]

            @pl.when(jnp.logical_not(first_of_seq))
            def _(h_ref=h_ref, half=half):
                h_ref[0:CONV_HIST, :] = carry_ref[half, j]
        else:
            for s in range(tm // seg):
                top = h_row(s * seg)
                h_ref[top - lags:top, :] = state_ref[s]

    for r0 in range(0, tm, chunk):
        conv = []
        for w_ref, cw_ref, cb_ref, _, new_ref, h_ref in halves:
            h = jnp.dot(xn_ref[r0:r0 + chunk, :], w_ref[...], preferred_element_type=F32)
            outs = []
            for p0 in range(0, chunk, piece):
                top = h_row(r0 + p0)
                hp = h[p0:p0 + piece]
                h_ref[top:top + piece, :] = hp
                out = cw_ref[lags:lags + 1, :] * hp
                for tap in range(lags):
                    lag = lags - tap
                    out = out + cw_ref[tap:tap + 1, :] * h_ref[top - lag:top - lag + piece, :]
                outs.append(out + cb_ref[...])
                seg_end = r0 + p0 + piece
                if seg_end % seg == 0:
                    new_ref[seg_end // seg - 1] = hp[piece - lags:, :]
            conv.append(outs[0] if len(outs) == 1 else jnp.concatenate(outs, axis=0))
        gate, val = conv
        act = (gate * jax.nn.sigmoid(gate) * val).astype(BF16)
        acc_ref[r0:r0 + chunk, :] += jnp.dot(act, wd_ref[...], preferred_element_type=F32)

    if tiles_per_seq > 1:
        for half, (_, _, _, _, _, h_ref) in enumerate(halves):
            carry_ref[half, j] = h_ref[tm:tm + CONV_HIST, :]

    @pl.when(j == pl.num_programs(1) - 1)
    def _():
        y_ref[...] = _rmsnorm(x_ref[...] + acc_ref[...], gf_ref[...])


def _conv_ffn(x, seq_len, g, w_up, conv_w, conv_b, w_down, state, g_final):
    n, d = x.shape
    f = w_down.shape[0]
    tm = min(512, n)
    tf = 512
    seg = min(seq_len, tm)
    n_seg = tm // seg
    tiles_per_seq = seq_len // seg
    assert f % tf == 0 and n % tm == 0 and tm % seg == 0 and seq_len % seg == 0
    n_f = f // tf
    h_rows = tm + CONV_HIST * n_seg
    lo = lambda i, j: (0, j)
    hi = lambda i, j: (0, n_f + j)
    st_lo = lambda i, j: (i // tiles_per_seq, 0, j)
    st_hi = lambda i, j: (i // tiles_per_seq, 0, n_f + j)
    y, tail_g, tail_v = pl.pallas_call(
        functools.partial(_ffn_kernel, tm=tm, seg=seg, chunk=min(ROW_CHUNK, tm), tiles_per_seq=tiles_per_seq),
        grid=(n // tm, n_f),
        in_specs=[
            pl.BlockSpec((tm, d), lambda i, j: (i, 0)),
            pl.BlockSpec((1, d), lambda i, j: (0, 0)),
            pl.BlockSpec((d, tf), lo), pl.BlockSpec((d, tf), hi),
            pl.BlockSpec((CONV_W, tf), lo), pl.BlockSpec((CONV_W, tf), hi),
            pl.BlockSpec((1, tf), lo), pl.BlockSpec((1, tf), hi),
            pl.BlockSpec((tf, d), lambda i, j: (j, 0)),
            pl.BlockSpec((n_seg, CONV_W - 1, tf), st_lo), pl.BlockSpec((n_seg, CONV_W - 1, tf), st_hi),
            pl.BlockSpec((1, d), lambda i, j: (0, 0)),
        ],
        out_specs=(
            pl.BlockSpec((tm, d), lambda i, j: (i, 0)),
            pl.BlockSpec((n_seg, CONV_W - 1, tf), lambda i, j: (i, 0, j)),
            pl.BlockSpec((n_seg, CONV_W - 1, tf), lambda i, j: (i, 0, j)),
        ),
        out_shape=(
            jax.ShapeDtypeStruct((n, d), F32),
            jax.ShapeDtypeStruct((n // seg, CONV_W - 1, f), F32),
            jax.ShapeDtypeStruct((n // seg, CONV_W - 1, f), F32),
        ),
        scratch_shapes=[
            pltpu.VMEM((tm, d), BF16),
            pltpu.VMEM((tm, d), F32),
            pltpu.VMEM((h_rows, tf), F32),
            pltpu.VMEM((h_rows, tf), F32),
            pltpu.VMEM((2, n_f, CONV_HIST, tf), F32),
        ],
        compiler_params=_params(("arbitrary", "arbitrary")),
        name="conv_ffn",
    )(x, g, w_up, w_up, conv_w, conv_w, conv_b, conv_b, w_down, state, state, g_final)
    last_tile = slice(tiles_per_seq - 1, None, tiles_per_seq)
    return y, tail_g[last_tile], tail_v[last_tile]


def _layer(x, pool_state, past_k, past_v, mk, mv, conv_state, p):
    b, l, d = x.shape
    n = b * l
    assert l >= POOL_HIST and l >= CONV_W - 1
    u, q, kf, kb, vf, vb, gates = _in_proj(x.reshape(n, d), p["norm_mix_g"], p["w_in"], p["b_gate"])
    width = u.shape[1]
    q = q.reshape(b, l, width)
    kb = kb.reshape(b, l, width)
    vb = vb.reshape(b, l, width)
    if past_k is None:
        past = 0
        k_all, v_all = kb, vb
        hist = jnp.zeros((b, POOL_HIST, width), F32)
    else:
        past = past_k.shape[1]
        k_all = jnp.concatenate([past_k.reshape(b, past, width).astype(BF16), kb], axis=1)
        v_all = jnp.concatenate([past_v.reshape(b, past, width).astype(BF16), vb], axis=1)
        hist = jnp.pad(pool_state, ((0, 0), (POOL_HIST - pool_state.shape[1], 0), (0, 0)))
    y_sb = _sb_attention(q, k_all, v_all)

    x2 = _mix(x, u.reshape(b, l, width), y_sb, gates.reshape(b, l, 2 * d), hist, mk, mv,
              p["pool_w"], p["pool_scale"], p["w_branch_pool"], p["w_branch_sb"], p["w_out"],
              p["norm_xa_g"], p["xa_wq"], p["xa_wo"], past)

    y, new_g, new_v = _conv_ffn(x2.reshape(n, d), l, p["norm_ffn_g"], p["ffn_w_up"], p["ffn_conv_w"],
                                p["ffn_conv_b"], p["ffn_w_down"], conv_state, p["norm_final_g"])
    new_pool = u.reshape(b, l, width)[:, l - (POOL_HIST - 1):, :]
    new_conv = jnp.concatenate([new_g, new_v], axis=-1)
    return (y.reshape(b, l, d), new_pool, kf.reshape(b, l, SB_HEADS, HEAD_DIM),
            vf.reshape(b, l, SB_HEADS, HEAD_DIM), new_conv)


def kernel(x_prompt, x_sample, mem_prompt, state_pool, cache_sb_k, cache_sb_v, cache_mem_k, cache_mem_v, state_ffn_conv, norm_mix_g, w_in, b_gate, pool_w, pool_scale, w_branch_pool, w_branch_sb, w_out, norm_xa_g, norm_mem_g, xa_wq, xa_wkv, xa_wo, norm_ffn_g, ffn_w_up, ffn_conv_w, ffn_conv_b, ffn_w_down, norm_final_g):
    assert norm_mix_g.shape[0] == 1, "single-layer step"
    bp, lp, d = x_prompt.shape
    n_mem = mem_prompt.shape[1]
    xa_width = xa_wq.shape[2]
    row = lambda a: a.reshape(1, -1)
    p = {
        "norm_mix_g": row(norm_mix_g[0]), "w_in": w_in[0].astype(BF16), "b_gate": row(b_gate[0]),
        "pool_w": pool_w[0].astype(BF16), "pool_scale": row(pool_scale[0]),
        "w_branch_pool": w_branch_pool[0].astype(BF16), "w_branch_sb": w_branch_sb[0].astype(BF16),
        "w_out": w_out[0].astype(BF16), "norm_xa_g": row(norm_xa_g[0]),
        "xa_wq": xa_wq[0].astype(BF16), "xa_wo": xa_wo[0].astype(BF16),
        "norm_ffn_g": row(norm_ffn_g[0]), "ffn_w_up": ffn_w_up[0].astype(BF16),
        "ffn_conv_w": ffn_conv_w[0], "ffn_conv_b": row(ffn_conv_b[0]),
        "ffn_w_down": ffn_w_down[0].astype(BF16), "norm_final_g": row(norm_final_g),
    }

    mem_kv = _norm_matmul(mem_prompt.reshape(bp * n_mem, d), row(norm_mem_g[0]), xa_wkv[0].astype(BF16))
    mem_kv = mem_kv.reshape(bp, n_mem, 2 * xa_width)
    mk_p, mv_p = mem_kv[..., :xa_width], mem_kv[..., xa_width:]

    conv0 = jnp.zeros((bp, CONV_W - 1, ffn_w_up.shape[2]), F32)
    y_p, pool_p, k_p, v_p, conv_p = _layer(x_prompt, None, None, None, mk_p.astype(BF16), mv_p.astype(BF16),
                                           conv0, p)

    bs = x_sample.shape[0]
    mk_s = cache_mem_k[0].reshape(bs, n_mem, xa_width).astype(BF16)
    mv_s = cache_mem_v[0].reshape(bs, n_mem, xa_width).astype(BF16)
    y_s, pool_s, k_s, v_s, conv_s = _layer(x_sample, state_pool[0], cache_sb_k[0], cache_sb_v[0], mk_s, mv_s,
                                           state_ffn_conv[0], p)

    mem_shape = (1, bp, n_mem, XA_HEADS, HEAD_DIM)
    return (y_p, y_s,
            pool_p[None], k_p[None], v_p[None], mk_p.reshape(mem_shape), mv_p.reshape(mem_shape), conv_p[None],
            pool_s[None], k_s[None], v_s[None], conv_s[None])
```

```python
import functools

import jax
import jax.numpy as jnp
import numpy as np
from jax import lax
from jax.experimental import pallas as pl
from jax.experimental.pallas import tpu as pltpu

F32 = jnp.float32
BF16 = jnp.bfloat16

EPS = 1e-6
POOL_WINDOWS = (2, 4, 8, 16)
POOL_HIST = 16
SB_HEADS = 8
HEAD_DIM = 128
XA_HEADS = 4
CONV_W = 3
CONV_HIST = 8
LANES = 128
ROW_CHUNK = 128

SB_DEAD_LOG = -88.0

VMEM_LIMIT = 56 * 1024 * 1024


def _rmsnorm(xf, g):
    ms = jnp.mean(xf * xf, axis=-1, keepdims=True)
    return xf * lax.rsqrt(ms + EPS) * g


def _const_spec(shape):
    zeros = (0,) * len(shape)
    return pl.BlockSpec(shape, lambda *_: zeros, pipeline_mode=pl.Buffered(1))


def _params(semantics):
    return pltpu.CompilerParams(dimension_semantics=semantics, vmem_limit_bytes=VMEM_LIMIT)


def _in_proj_kernel(x_ref, g_ref, w_ref, b_ref, u_ref, q_ref, kf_ref, kb_ref, vf_ref, vb_ref, gate_ref,
                    xn_ref, *, q_scale, chunk):
    j = pl.program_id(1)
    tm = x_ref.shape[0]

    @pl.when(j == 0)
    def _():
        xn_ref[...] = _rmsnorm(x_ref[...], g_ref[...]).astype(BF16)

    def project(epilogue):
        for r0 in range(0, tm, chunk):
            rows = slice(r0, r0 + chunk)
            epilogue(rows, jnp.dot(xn_ref[rows, :], w_ref[...], preferred_element_type=F32))

    def put_u(rows, acc):
        u_ref[rows, :] = acc

    def put_q(rows, acc):
        q_ref[rows, :] = (acc * q_scale).astype(BF16)

    def put_heads(ref, rows, acc):
        ref[rows, :, :] = acc.reshape(acc.shape[0], ref.shape[1], ref.shape[2])

    def put_k(rows, acc):
        put_heads(kf_ref, rows, acc)
        kb_ref[rows, :] = acc.astype(BF16)

    def put_v(rows, acc):
        put_heads(vf_ref, rows, acc)
        vb_ref[rows, :] = acc.astype(BF16)

    def put_gate(rows, acc):
        gate_ref[rows, :] = jax.nn.sigmoid(acc + b_ref[...]).astype(BF16)

    pl.when(j == 0)(functools.partial(project, put_u))
    pl.when(j == 1)(functools.partial(project, put_q))
    pl.when(j == 2)(functools.partial(project, put_k))
    pl.when(j == 3)(functools.partial(project, put_v))
    pl.when(j >= 4)(functools.partial(project, put_gate))


def _in_proj(x, g, w_in, b_gate):
    n, d = x.shape
    width = d // 2
    n_col = w_in.shape[1] // width
    heads = width // HEAD_DIM
    tm = min(512, n)
    grid = (n // tm, n_col)
    row = lambda i, j: (i, 0)
    gate_col = lambda i, j: (i, jnp.maximum(j - 4, 0))
    out_shape = (
        jax.ShapeDtypeStruct((n, width), F32),
        jax.ShapeDtypeStruct((n, width), BF16),
        jax.ShapeDtypeStruct((n, heads, HEAD_DIM), F32),
        jax.ShapeDtypeStruct((n, width), BF16),
        jax.ShapeDtypeStruct((n, heads, HEAD_DIM), F32),
        jax.ShapeDtypeStruct((n, width), BF16),
        jax.ShapeDtypeStruct((n, 2 * d), BF16),
    )
    flat = pl.BlockSpec((tm, width), row)
    by_head = pl.BlockSpec((tm, heads, HEAD_DIM), lambda i, j: (i, 0, 0))
    out_specs = (flat, flat, by_head, flat, by_head, flat, pl.BlockSpec((tm, width), gate_col))
    return pl.pallas_call(
        functools.partial(_in_proj_kernel, q_scale=1.0 / np.sqrt(HEAD_DIM), chunk=min(ROW_CHUNK, tm)),
        grid=grid,
        in_specs=[
            pl.BlockSpec((tm, d), row),
            pl.BlockSpec((1, d), lambda i, j: (0, 0)),
            pl.BlockSpec((d, width), lambda i, j: (0, j)),
            pl.BlockSpec((1, width), lambda i, j: (0, jnp.maximum(j - 4, 0))),
        ],
        out_specs=out_specs,
        out_shape=out_shape,
        scratch_shapes=[pltpu.VMEM((tm, d), BF16)],
        compiler_params=_params(("arbitrary", "arbitrary")),
        name="in_proj",
    )(x, g, w_in, b_gate)


def _norm_matmul_kernel(x_ref, g_ref, w_ref, o_ref):
    xn = _rmsnorm(x_ref[...], g_ref[...]).astype(BF16)
    o_ref[...] = jnp.dot(xn, w_ref[...], preferred_element_type=F32)


def _norm_matmul(x, g, w):
    n, d = x.shape
    tm = min(256, n)
    return pl.pallas_call(
        _norm_matmul_kernel,
        grid=(n // tm,),
        in_specs=[pl.BlockSpec((tm, d), lambda i: (i, 0)), _const_spec((1, d)), _const_spec(w.shape)],
        out_specs=pl.BlockSpec((tm, w.shape[1]), lambda i: (i, 0)),
        out_shape=jax.ShapeDtypeStruct((n, w.shape[1]), F32),
        compiler_params=_params(("arbitrary",)),
        name="mem_kv",
    )(x, g, w)


def _sb_kernel(q_ref, k_ref, v_ref, tri1_ref, tri2_ref, o_ref, *, blk, n_sub, n_heads, key_block_offset):
    step = pl.program_id(2)
    first_span = 2
    wide = first_span * blk
    col_minus_row = (lax.broadcasted_iota(jnp.int32, (blk, wide), 1)
                     - lax.broadcasted_iota(jnp.int32, (blk, wide), 0))
    chains = [(sub, h) for sub in range(n_sub) for h in range(n_heads)]
    rows = lambda sub: slice(sub * blk, (sub + 1) * blk)
    lanes = lambda h: slice(h * HEAD_DIM, (h + 1) * HEAD_DIM)
    qs = [q_ref[0, rows(sub), lanes(h)] for sub, h in chains]
    first_diag = step * n_sub + key_block_offset

    def visit(newest, span, accs, runs, first):
        width = span * blk
        tri = (tri1_ref if span == 1 else tri2_ref)[...]
        log_nots, log_betas, vs, masks = [], [], [], []
        runs = list(runs)
        for c, (sub, h) in enumerate(chains):
            kb_new = first_diag + sub - newest
            kb_old = kb_new - (span - 1)
            start_blk = jnp.maximum(kb_old, 0)
            if first:
                masks.append(col_minus_row[:, :width] < (kb_new - start_blk) * blk)
            else:
                runs[c] = jnp.where(kb_old >= 0, runs[c], -1e30)
            start = pl.multiple_of(start_blk * blk, blk)
            k = k_ref[0, pl.ds(start, width), lanes(h)]
            vs.append(v_ref[0, pl.ds(start, width), lanes(h)])
            z = lax.dot_general(qs[c], k, (((1,), (1,)), ((), ())), preferred_element_type=F32)
            t = jnp.log(1.0 + jnp.exp(-jnp.abs(z)))
            log_not = -(jnp.maximum(z, 0.0) + t)
            if first:
                log_not = jnp.where(masks[c], log_not, 0.0)
            log_nots.append(log_not)
            log_betas.append(jnp.minimum(z, 0.0) - t)
        stacked = jnp.concatenate(log_nots, axis=0)
        hi = stacked.astype(BF16)
        lo = (stacked - hi.astype(F32)).astype(BF16)
        sums = (jnp.dot(hi, tri, preferred_element_type=F32)
                + jnp.dot(lo, tri, preferred_element_type=F32))
        new_accs, new_runs = [], []
        for c in range(len(chains)):
            part = sums[c * blk:(c + 1) * blk]
            between = part[:, :width]
            total = part[:, width:]
            if first:
                a = jnp.where(masks[c], jnp.exp(log_betas[c] + between), 0.0)
            else:
                a = jnp.exp(log_betas[c] + between + runs[c][:, :width])
            new_accs.append(accs[c] + jnp.dot(a.astype(BF16), vs[c], preferred_element_type=F32))
            new_runs.append(runs[c] + total)
        return tuple(new_accs), tuple(new_runs)

    zeros = tuple(jnp.zeros((blk, LANES), F32) for _ in chains)
    accs, runs = visit(0, first_span, zeros, zeros, True)

    def alive(carry):
        newest, _, runs = carry
        live = functools.reduce(jnp.maximum, runs)
        return jnp.logical_and(first_diag + n_sub - 1 - newest >= 0, jnp.max(live) > SB_DEAD_LOG)

    def older(carry):
        newest, accs, runs = carry
        accs, runs = visit(newest, 1, accs, runs, False)
        return newest + 1, accs, runs

    _, accs, _ = lax.while_loop(alive, older, (jnp.int32(first_span), accs, runs))
    for c, (sub, h) in enumerate(chains):
        o_ref[0, rows(sub), lanes(h)] = accs[c].astype(BF16)


def _suffix_sum_matrix(width):
    jj = np.arange(width)
    return jnp.asarray(np.concatenate([jj[:, None] > jj[None, :], np.ones((width, LANES), bool)], axis=1), BF16)


def _sb_attention(q, k, v):
    b, lq, width = q.shape
    lk = k.shape[1]
    blk = min(128, lq)
    assert lq % blk == 0 and lk % blk == 0 and lk >= 2 * blk
    n_q = lq // blk
    n_chains = 8
    n_sub = min(n_chains, n_q)
    n_heads = n_chains // n_sub
    assert n_q % n_sub == 0 and SB_HEADS % n_heads == 0
    tri1, tri2 = _suffix_sum_matrix(blk), _suffix_sum_matrix(2 * blk)
    lane_w = n_heads * HEAD_DIM
    return pl.pallas_call(
        functools.partial(_sb_kernel, blk=blk, n_sub=n_sub, n_heads=n_heads,
                          key_block_offset=(lk - lq) // blk),
        grid=(b, SB_HEADS // n_heads, n_q // n_sub),
        in_specs=[
            pl.BlockSpec((1, n_sub * blk, lane_w), lambda bi, h, i: (bi, i, h)),
            pl.BlockSpec((1, lk, lane_w), lambda bi, h, i: (bi, 0, h)),
            pl.BlockSpec((1, lk, lane_w), lambda bi, h, i: (bi, 0, h)),
            _const_spec(tri1.shape), _const_spec(tri2.shape),
        ],
        out_specs=pl.BlockSpec((1, n_sub * blk, lane_w), lambda bi, h, i: (bi, i, h)),
        out_shape=jax.ShapeDtypeStruct(q.shape, BF16),
        compiler_params=_params(("arbitrary", "arbitrary", "arbitrary")),
        name="sb_attn",
    )(q, k, v, tri1, tri2)


def _mix_kernel(x_ref, u_ref, ysb_ref, gp_ref, gs_ref, hist_ref, mk_ref, mv_ref,
                pool_w_ref, pool_scale_ref, wbp_ref, wbs_ref, wout_ref, gxa_ref, wq_ref, wo_ref,
                o_ref, ubuf_ref, *, tm, pos0):
    l = pl.program_id(1)
    group = u_ref.shape[2] // len(POOL_WINDOWS)

    @pl.when(l == 0)
    def _():
        ubuf_ref[0:POOL_HIST, :] = hist_ref[0]

    u = u_ref[0]
    ubuf_ref[POOL_HIST:POOL_HIST + tm, :] = u
    pos = pos0 + l * tm + lax.broadcasted_iota(jnp.int32, (tm, 1), 0)

    pooled = []
    for g, w in enumerate(POOL_WINDOWS):
        cols = slice(g * group, (g + 1) * group)
        window_sum = u[:, cols]
        for back in range(1, w):
            window_sum = window_sum + ubuf_ref[POOL_HIST - back:POOL_HIST - back + tm, cols]
        count = jnp.minimum(w, pos + 1).astype(F32)
        delta = (window_sum / count - u[:, cols]).astype(BF16)
        pooled.append(jnp.dot(delta, pool_w_ref[g], preferred_element_type=F32))
    y_pool = (jnp.concatenate(pooled, axis=-1) * pool_scale_ref[...]).astype(BF16)

    ubuf_ref[0:POOL_HIST, :] = ubuf_ref[tm:tm + POOL_HIST, :]

    branch_pool = jnp.dot(y_pool, wbp_ref[...], preferred_element_type=F32)
    branch_sb = jnp.dot(ysb_ref[0], wbs_ref[...], preferred_element_type=F32)
    merged = gp_ref[0].astype(F32) * branch_pool + gs_ref[0].astype(F32) * branch_sb
    x1 = x_ref[0] + jnp.dot(merged.astype(BF16), wout_ref[...], preferred_element_type=F32)

    xn = _rmsnorm(x1, gxa_ref[...]).astype(BF16)
    q = (jnp.dot(xn, wq_ref[...], preferred_element_type=F32) * (1.0 / np.sqrt(HEAD_DIM))).astype(BF16)
    heads = []
    for h in range(XA_HEADS):
        cols = slice(h * HEAD_DIM, (h + 1) * HEAD_DIM)
        s = lax.dot_general(q[:, cols], mk_ref[0, :, cols], (((1,), (1,)), ((), ())),
                            preferred_element_type=F32)
        p = jnp.exp(s - jnp.max(s, axis=-1, keepdims=True))
        p = p / jnp.sum(p, axis=-1, keepdims=True)
        heads.append(jnp.dot(p.astype(BF16), mv_ref[0, :, cols], preferred_element_type=F32))
    attn = jnp.concatenate(heads, axis=-1).astype(BF16)
    o_ref[0] = x1 + jnp.dot(attn, wo_ref[...], preferred_element_type=F32)


def _mix(x, u, ysb, gates, hist, mk, mv, pool_w, pool_scale, wbp, wbs, wout, gxa, wq, wo, pos0):
    b, l, d = x.shape
    tm = min(256, l)
    tile = lambda bi, li: (bi, li, 0)
    per_batch = lambda bi, li: (bi, 0, 0)
    return pl.pallas_call(
        functools.partial(_mix_kernel, tm=tm, pos0=pos0),
        grid=(b, l // tm),
        in_specs=[
            pl.BlockSpec((1, tm, d), tile),
            pl.BlockSpec((1, tm, u.shape[2]), tile),
            pl.BlockSpec((1, tm, ysb.shape[2]), tile),
            pl.BlockSpec((1, tm, d), tile),
            pl.BlockSpec((1, tm, d), lambda bi, li: (bi, li, 1)),
            pl.BlockSpec((1,) + hist.shape[1:], per_batch),
            pl.BlockSpec((1,) + mk.shape[1:], per_batch),
            pl.BlockSpec((1,) + mv.shape[1:], per_batch),
            _const_spec(pool_w.shape), _const_spec(pool_scale.shape), _const_spec(wbp.shape),
            _const_spec(wbs.shape), _const_spec(wout.shape), _const_spec(gxa.shape),
            _const_spec(wq.shape), _const_spec(wo.shape),
        ],
        out_specs=pl.BlockSpec((1, tm, d), tile),
        out_shape=jax.ShapeDtypeStruct(x.shape, F32),
        scratch_shapes=[pltpu.VMEM((POOL_HIST + tm, u.shape[2]), F32)],
        compiler_params=_params(("arbitrary", "arbitrary")),
        name="mix",
    )(x, u, ysb, gates, gates, hist, mk, mv, pool_w, pool_scale, wbp, wbs, wout, gxa, wq, wo)


def _ffn_kernel(x_ref, g_ref, wg_ref, wv_ref, cwg_ref, cwv_ref, cbg_ref, cbv_ref, wd_ref, sg_ref, sv_ref,
                gf_ref, y_ref, ng_ref, nv_ref, xn_ref, hg_ref, hv_ref, carry_ref,
                *, tm, seg, chunk, tiles_per_seq):
    i = pl.program_id(0)
    j = pl.program_id(1)
    piece = min(chunk, seg)
    lags = CONV_W - 1
    halves = ((wg_ref, cwg_ref, cbg_ref, sg_ref, ng_ref, hg_ref),
              (wv_ref, cwv_ref, cbv_ref, sv_ref, nv_ref, hv_ref))
    h_row = lambda r: r + CONV_HIST * (r // seg + 1)

    @pl.when(j == 0)
    def _():
        xn_ref[...] = _rmsnorm(x_ref[...], g_ref[...]).astype(BF16)
        y_ref[...] = jnp.zeros_like(y_ref)

    for half, (_, _, _, state_ref, _, h_ref) in enumerate(halves):
        if tiles_per_seq > 1:
            first_of_seq = (i % tiles_per_seq) == 0

            @pl.when(first_of_seq)
            def _(state_ref=state_ref, h_ref=h_ref):
                h_ref[CONV_HIST - lags:CONV_HIST, :] = state_ref[0]

            @pl.when(jnp.logical_not(first_of_seq))
            def _(h_ref=h_ref, half=half):
                h_ref[0:CONV_HIST, :] = carry_ref[half, j]
        else:
            for s in range(tm // seg):
                top = h_row(s * seg)
                h_ref[top - lags:top, :] = state_ref[s]

    acts = []
    for r0 in range(0, tm, chunk):
        conv = []
        for w_ref, cw_ref, cb_ref, _, new_ref, h_ref in halves:
            h = jnp.dot(xn_ref[r0:r0 + chunk, :], w_ref[...], preferred_element_type=F32)
            outs = []
            for p0 in range(0, chunk, piece):
                top = h_row(r0 + p0)
                hp = h[p0:p0 + piece]
                h_ref[top:top + piece, :] = hp
                out = cw_ref[lags:lags + 1, :] * hp
                for tap in range(lags):
                    lag = lags - tap
                    out = out + cw_ref[tap:tap + 1, :] * h_ref[top - lag:top - lag + piece, :]
                outs.append(out + cb_ref[...])
                seg_end = r0 + p0 + piece
                if seg_end % seg == 0:
                    new_ref[seg_end // seg - 1] = hp[piece - lags:, :]
            conv.append(outs[0] if len(outs) == 1 else jnp.concatenate(outs, axis=0))
        gate, val = conv
        acts.append((gate * jax.nn.sigmoid(gate) * val).astype(BF16))
    for c, act in enumerate(acts):
        rows = slice(c * chunk, (c + 1) * chunk)
        y_ref[rows, :] += jnp.dot(act, wd_ref[...], preferred_element_type=F32)

    if tiles_per_seq > 1:
        for half, (_, _, _, _, _, h_ref) in enumerate(halves):
            carry_ref[half, j] = h_ref[tm:tm + CONV_HIST, :]

    @pl.when(j == pl.num_programs(1) - 1)
    def _():
        y_ref[...] = _rmsnorm(x_ref[...] + y_ref[...], gf_ref[...])


def _conv_ffn(x, seq_len, g, w_up, conv_w, conv_b, w_down, state, g_final):
    n, d = x.shape
    f = w_down.shape[0]
    tm = min(1024, n)
    tf = 256
    seg = min(seq_len, tm)
    n_seg = tm // seg
    tiles_per_seq = seq_len // seg
    assert f % tf == 0 and n % tm == 0 and tm % seg == 0 and seq_len % seg == 0
    n_f = f // tf
    h_rows = tm + CONV_HIST * n_seg
    lo = lambda i, j: (0, j)
    hi = lambda i, j: (0, n_f + j)
    st_lo = lambda i, j: (i // tiles_per_seq, 0, j)
    st_hi = lambda i, j: (i // tiles_per_seq, 0, n_f + j)
    y, tail_g, tail_v = pl.pallas_call(
        functools.partial(_ffn_kernel, tm=tm, seg=seg, chunk=min(ROW_CHUNK, tm), tiles_per_seq=tiles_per_seq),
        grid=(n // tm, n_f),
        in_specs=[
            pl.BlockSpec((tm, d), lambda i, j: (i, 0)),
            pl.BlockSpec((1, d), lambda i, j: (0, 0)),
            pl.BlockSpec((d, tf), lo), pl.BlockSpec((d, tf), hi),
            pl.BlockSpec((CONV_W, tf), lo), pl.BlockSpec((CONV_W, tf), hi),
            pl.BlockSpec((1, tf), lo), pl.BlockSpec((1, tf), hi),
            pl.BlockSpec((tf, d), lambda i, j: (j, 0)),
            pl.BlockSpec((n_seg, CONV_W - 1, tf), st_lo), pl.BlockSpec((n_seg, CONV_W - 1, tf), st_hi),
            pl.BlockSpec((1, d), lambda i, j: (0, 0)),
        ],
        out_specs=(
            pl.BlockSpec((tm, d), lambda i, j: (i, 0)),
            pl.BlockSpec((n_seg, CONV_W - 1, tf), lambda i, j: (i, 0, j)),
            pl.BlockSpec((n_seg, CONV_W - 1, tf), lambda i, j: (i, 0, j)),
        ),
        out_shape=(
            jax.ShapeDtypeStruct((n, d), F32),
            jax.ShapeDtypeStruct((n // seg, CONV_W - 1, f), F32),
            jax.ShapeDtypeStruct((n // seg, CONV_W - 1, f), F32),
        ),
        scratch_shapes=[
            pltpu.VMEM((tm, d), BF16),
            pltpu.VMEM((h_rows, tf), F32),
            pltpu.VMEM((h_rows, tf), F32),
            pltpu.VMEM((2, n_f, CONV_HIST, tf), F32),
        ],
        compiler_params=_params(("arbitrary", "arbitrary")),
        name="conv_ffn",
    )(x, g, w_up, w_up, conv_w, conv_w, conv_b, conv_b, w_down, state, state, g_final)
    last_tile = slice(tiles_per_seq - 1, None, tiles_per_seq)
    return y, tail_g[last_tile], tail_v[last_tile]


def _layer(x, pool_state, past_k, past_v, mk, mv, conv_state, p):
    b, l, d = x.shape
    n = b * l
    assert l >= POOL_HIST and l >= CONV_W - 1
    u, q, kf, kb, vf, vb, gates = _in_proj(x.reshape(n, d), p["norm_mix_g"], p["w_in"], p["b_gate"])
    width = u.shape[1]
    q = q.reshape(b, l, width)
    kb = kb.reshape(b, l, width)
    vb = vb.reshape(b, l, width)
    if past_k is None:
        past = 0
        k_all, v_all = kb, vb
        hist = jnp.zeros((b, POOL_HIST, width), F32)
    else:
        past = past_k.shape[1]
        k_all = jnp.concatenate([past_k.reshape(b, past, width).astype(BF16), kb], axis=1)
        v_all = jnp.concatenate([past_v.reshape(b, past, width).astype(BF16), vb], axis=1)
        hist = jnp.pad(pool_state, ((0, 0), (POOL_HIST - pool_state.shape[1], 0), (0, 0)))
    y_sb = _sb_attention(q, k_all, v_all)

    x2 = _mix(x, u.reshape(b, l, width), y_sb, gates.reshape(b, l, 2 * d), hist, mk, mv,
              p["pool_w"], p["pool_scale"], p["w_branch_pool"], p["w_branch_sb"], p["w_out"],
              p["norm_xa_g"], p["xa_wq"], p["xa_wo"], past)

    y, new_g, new_v = _conv_ffn(x2.reshape(n, d), l, p["norm_ffn_g"], p["ffn_w_up"], p["ffn_conv_w"],
                                p["ffn_conv_b"], p["ffn_w_down"], conv_state, p["norm_final_g"])
    new_pool = u.reshape(b, l, width)[:, l - (POOL_HIST - 1):, :]
    new_conv = jnp.concatenate([new_g, new_v], axis=-1)
    return (y.reshape(b, l, d), new_pool, kf.reshape(b, l, SB_HEADS, HEAD_DIM),
            vf.reshape(b, l, SB_HEADS, HEAD_DIM), new_conv)


def kernel(x_prompt, x_sample, mem_prompt, state_pool, cache_sb_k, cache_sb_v, cache_mem_k, cache_mem_v, state_ffn_conv, norm_mix_g, w_in, b_gate, pool_w, pool_scale, w_branch_pool, w_branch_sb, w_out, norm_xa_g, norm_mem_g, xa_wq, xa_wkv, xa_wo, norm_ffn_g, ffn_w_up, ffn_conv_w, ffn_conv_b, ffn_w_down, norm_final_g):
    assert norm_mix_g.shape[0] == 1, "single-layer step"
    bp, lp, d = x_prompt.shape
    n_mem = mem_prompt.shape[1]
    xa_width = xa_wq.shape[2]
    row = lambda a: a.reshape(1, -1)
    p = {
        "norm_mix_g": row(norm_mix_g[0]), "w_in": w_in[0].astype(BF16), "b_gate": row(b_gate[0]),
        "pool_w": pool_w[0].astype(BF16), "pool_scale": row(pool_scale[0]),
        "w_branch_pool": w_branch_pool[0].astype(BF16), "w_branch_sb": w_branch_sb[0].astype(BF16),
        "w_out": w_out[0].astype(BF16), "norm_xa_g": row(norm_xa_g[0]),
        "xa_wq": xa_wq[0].astype(BF16), "xa_wo": xa_wo[0].astype(BF16),
        "norm_ffn_g": row(norm_ffn_g[0]), "ffn_w_up": ffn_w_up[0].astype(BF16),
        "ffn_conv_w": ffn_conv_w[0], "ffn_conv_b": row(ffn_conv_b[0]),
        "ffn_w_down": ffn_w_down[0].astype(BF16), "norm_final_g": row(norm_final_g),
    }

    mem_kv = _norm_matmul(mem_prompt.reshape(bp * n_mem, d), row(norm_mem_g[0]), xa_wkv[0].astype(BF16))
    mem_kv = mem_kv.reshape(bp, n_mem, 2 * xa_width)
    mk_p, mv_p = mem_kv[..., :xa_width], mem_kv[..., xa_width:]

    conv0 = jnp.zeros((bp, CONV_W - 1, ffn_w_up.shape[2]), F32)
    y_p, pool_p, k_p, v_p, conv_p = _layer(x_prompt, None, None, None, mk_p.astype(BF16), mv_p.astype(BF16),
                                           conv0, p)

    bs = x_sample.shape[0]
    mk_s = cache_mem_k[0].reshape(bs, n_mem, xa_width).astype(BF16)
    mv_s = cache_mem_v[0].reshape(bs, n_mem, xa_width).astype(BF16)
    y_s, pool_s, k_s, v_s, conv_s = _layer(x_sample, state_pool[0], cache_sb_k[0], cache_sb_v[0], mk_s, mv_s,
                                           state_ffn_conv[0], p)

    mem_shape = (1, bp, n_mem, XA_HEADS, HEAD_DIM)
    return (y_p, y_s,
            pool_p[None], k_p[None], v_p[None], mk_p.reshape(mem_shape), mv_p.reshape(mem_shape), conv_p[None],
            pool_s[None], k_s[None], v_s[None], conv_s[None])
```

```python
import functools

import jax
import jax.numpy as jnp
import numpy as np
from jax import lax
from jax.experimental import pallas as pl
from jax.experimental.pallas import tpu as pltpu

F32 = jnp.float32
BF16 = jnp.bfloat16

EPS = 1e-6
POOL_WINDOWS = (2, 4, 8, 16)
POOL_HIST = 16
SB_HEADS = 8
HEAD_DIM = 128
XA_HEADS = 4
CONV_W = 3
CONV_HIST = 8
LANES = 128
FFN_TILE = 512
ROW_CHUNK = 128

SB_DEAD_LOG = -88.0

VMEM_LIMIT = 56 * 1024 * 1024


def _rmsnorm(xf, g):
    ms = jnp.mean(xf * xf, axis=-1, keepdims=True)
    return xf * lax.rsqrt(ms + EPS) * g


def _const_spec(shape):
    zeros = (0,) * len(shape)
    return pl.BlockSpec(shape, lambda *_: zeros, pipeline_mode=pl.Buffered(1))


def _column_tiles(w, tile):
    k, n = w.shape
    return w.astype(BF16).reshape(k, n // tile, tile).transpose(1, 0, 2)


def _params(semantics):
    return pltpu.CompilerParams(dimension_semantics=semantics, vmem_limit_bytes=VMEM_LIMIT)


def _in_proj_kernel(x_ref, g_ref, w_ref, b_ref, u_ref, q_ref, kf_ref, kb_ref, vf_ref, vb_ref, gate_ref,
                    xn_ref, *, q_scale, chunk):
    j = pl.program_id(1)
    tm = x_ref.shape[0]

    @pl.when(j == 0)
    def _():
        xn_ref[...] = _rmsnorm(x_ref[...], g_ref[...]).astype(BF16)

    def project(epilogue):
        for r0 in range(0, tm, chunk):
            rows = slice(r0, r0 + chunk)
            epilogue(rows, jnp.dot(xn_ref[rows, :], w_ref[...], preferred_element_type=F32))

    def put_u(rows, acc):
        u_ref[rows, :] = acc

    def put_q(rows, acc):
        q_ref[rows, :] = (acc * q_scale).astype(BF16)

    def put_heads(ref, rows, acc):
        ref[rows, :, :] = acc.reshape(acc.shape[0], ref.shape[1], ref.shape[2])

    def put_k(rows, acc):
        put_heads(kf_ref, rows, acc)
        kb_ref[rows, :] = acc.astype(BF16)

    def put_v(rows, acc):
        put_heads(vf_ref, rows, acc)
        vb_ref[rows, :] = acc.astype(BF16)

    def put_gate(rows, acc):
        gate_ref[rows, :] = jax.nn.sigmoid(acc + b_ref[...]).astype(BF16)

    pl.when(j == 0)(functools.partial(project, put_u))
    pl.when(j == 1)(functools.partial(project, put_q))
    pl.when(j == 2)(functools.partial(project, put_k))
    pl.when(j == 3)(functools.partial(project, put_v))
    pl.when(j >= 4)(functools.partial(project, put_gate))


def _in_proj(x, g, w_in, b_gate):
    n, d = x.shape
    n_col, _, width = w_in.shape
    heads = width // HEAD_DIM
    tm = min(512, n)
    grid = (n // tm, n_col)
    row = lambda i, j: (i, 0)
    gate_col = lambda i, j: (i, jnp.maximum(j - 4, 0))
    out_shape = (
        jax.ShapeDtypeStruct((n, width), F32),
        jax.ShapeDtypeStruct((n, width), BF16),
        jax.ShapeDtypeStruct((n, heads, HEAD_DIM), F32),
        jax.ShapeDtypeStruct((n, width), BF16),
        jax.ShapeDtypeStruct((n, heads, HEAD_DIM), F32),
        jax.ShapeDtypeStruct((n, width), BF16),
        jax.ShapeDtypeStruct((n, 2 * d), BF16),
    )
    flat = pl.BlockSpec((tm, width), row)
    by_head = pl.BlockSpec((tm, heads, HEAD_DIM), lambda i, j: (i, 0, 0))
    out_specs = (flat, flat, by_head, flat, by_head, flat, pl.BlockSpec((tm, width), gate_col))
    return pl.pallas_call(
        functools.partial(_in_proj_kernel, q_scale=1.0 / np.sqrt(HEAD_DIM), chunk=min(ROW_CHUNK, tm)),
        grid=grid,
        in_specs=[
            pl.BlockSpec((tm, d), row),
            pl.BlockSpec((1, d), lambda i, j: (0, 0)),
            pl.BlockSpec((None, d, width), lambda i, j: (j, 0, 0)),
            pl.BlockSpec((1, width), lambda i, j: (0, jnp.maximum(j - 4, 0))),
        ],
        out_specs=out_specs,
        out_shape=out_shape,
        scratch_shapes=[pltpu.VMEM((tm, d), BF16)],
        compiler_params=_params(("arbitrary", "arbitrary")),
        name="in_proj",
    )(x, g, w_in, b_gate)


def _norm_matmul_kernel(x_ref, g_ref, w_ref, o_ref):
    xn = _rmsnorm(x_ref[...], g_ref[...]).astype(BF16)
    o_ref[...] = jnp.dot(xn, w_ref[...], preferred_element_type=F32)


def _norm_matmul(x, g, w):
    n, d = x.shape
    tm = min(256, n)
    return pl.pallas_call(
        _norm_matmul_kernel,
        grid=(n // tm,),
        in_specs=[pl.BlockSpec((tm, d), lambda i: (i, 0)), _const_spec((1, d)), _const_spec(w.shape)],
        out_specs=pl.BlockSpec((tm, w.shape[1]), lambda i: (i, 0)),
        out_shape=jax.ShapeDtypeStruct((n, w.shape[1]), F32),
        compiler_params=_params(("arbitrary",)),
        name="mem_kv",
    )(x, g, w)


def _sb_kernel(*refs, blk, n_sub, n_heads, key_block_offset, has_past):
    if has_past:
        q_ref, k_ref, v_ref, past_k_ref, past_v_ref, tri1_ref, tri2_ref, o_ref = refs
    else:
        q_ref, k_ref, v_ref, tri1_ref, tri2_ref, o_ref = refs
    step = pl.program_id(2)
    first_span = 2
    wide = first_span * blk
    col_minus_row = (lax.broadcasted_iota(jnp.int32, (blk, wide), 1)
                     - lax.broadcasted_iota(jnp.int32, (blk, wide), 0))
    chains = [(sub, h) for sub in range(n_sub) for h in range(n_heads)]
    rows = lambda sub: slice(sub * blk, (sub + 1) * blk)
    lanes = lambda h: slice(h * HEAD_DIM, (h + 1) * HEAD_DIM)
    qs = [q_ref[0, rows(sub), lanes(h)] for sub, h in chains]
    first_diag = step * n_sub + key_block_offset

    def visit(newest, span, accs, runs, first):
        width = span * blk
        tri = (tri1_ref if span == 1 else tri2_ref)[...]
        log_nots, log_betas, vs, masks = [], [], [], []
        runs = list(runs)
        for c, (sub, h) in enumerate(chains):
            kb_new = first_diag + sub - newest
            kb_old = kb_new - (span - 1)
            start_blk = jnp.maximum(kb_old, 0)
            if first:
                masks.append(col_minus_row[:, :width] < (kb_new - start_blk) * blk)
            else:
                runs[c] = jnp.where(kb_old >= 0, runs[c], -1e30)
            start = pl.multiple_of(start_blk * blk, blk)
            if not has_past:
                k = k_ref[0, pl.ds(start, width), lanes(h)]
                v = v_ref[0, pl.ds(start, width), lanes(h)]
            elif first:
                newest_past = slice((key_block_offset - 1) * blk, key_block_offset * blk)
                k = jnp.concatenate([past_k_ref[0, newest_past, h, :].astype(BF16), k_ref[0, :, lanes(h)]], axis=0)
                v = jnp.concatenate([past_v_ref[0, newest_past, h, :].astype(BF16), v_ref[0, :, lanes(h)]], axis=0)
            else:
                k = past_k_ref[0, pl.ds(start, width), h, :].astype(BF16)
                v = past_v_ref[0, pl.ds(start, width), h, :].astype(BF16)
            vs.append(v)
            z = lax.dot_general(qs[c], k, (((1,), (1,)), ((), ())), preferred_element_type=F32)
            t = jnp.log(1.0 + jnp.exp(-jnp.abs(z)))
            log_not = -(jnp.maximum(z, 0.0) + t)
            if first:
                log_not = jnp.where(masks[c], log_not, 0.0)
            log_nots.append(log_not)
            log_betas.append(jnp.minimum(z, 0.0) - t)
        stacked = jnp.concatenate(log_nots, axis=0)
        hi = stacked.astype(BF16)
        lo = (stacked - hi.astype(F32)).astype(BF16)
        sums = (jnp.dot(hi, tri, preferred_element_type=F32)
                + jnp.dot(lo, tri, preferred_element_type=F32))
        new_accs, new_runs = [], []
        for c in range(len(chains)):
            part = sums[c * blk:(c + 1) * blk]
            between = part[:, :width]
            total = part[:, width:]
            if first:
                a = jnp.where(masks[c], jnp.exp(log_betas[c] + between), 0.0)
            else:
                a = jnp.exp(log_betas[c] + between + runs[c][:, :width])
            new_accs.append(accs[c] + jnp.dot(a.astype(BF16), vs[c], preferred_element_type=F32))
            new_runs.append(runs[c] + total)
        return tuple(new_accs), tuple(new_runs)

    zeros = tuple(jnp.zeros((blk, LANES), F32) for _ in chains)
    accs, runs = visit(0, first_span, zeros, zeros, True)

    def alive(carry):
        newest, _, runs = carry
        live = functools.reduce(jnp.maximum, runs)
        return jnp.logical_and(first_diag + n_sub - 1 - newest >= 0, jnp.max(live) > SB_DEAD_LOG)

    def older(carry):
        newest, accs, runs = carry
        accs, runs = visit(newest, 1, accs, runs, False)
        return newest + 1, accs, runs

    _, accs, _ = lax.while_loop(alive, older, (jnp.int32(first_span), accs, runs))
    for c, (sub, h) in enumerate(chains):
        o_ref[0, rows(sub), lanes(h)] = accs[c].astype(BF16)


def _suffix_sum_matrix(width):
    jj = np.arange(width)
    return jnp.asarray(np.concatenate([jj[:, None] > jj[None, :], np.ones((width, LANES), bool)], axis=1), BF16)


def _sb_attention(q, k, v, past_k=None, past_v=None):
    b, lq, width = q.shape
    has_past = past_k is not None
    past = past_k.shape[1] if has_past else 0
    lk = past + lq
    blk = min(128, lq)
    assert lq % blk == 0 and past % blk == 0 and lk >= 2 * blk
    assert not has_past or lq == blk, "with a cache the new keys must form one block"
    n_q = lq // blk
    n_chains = 8
    n_sub = min(n_chains, n_q)
    n_heads = n_chains // n_sub
    assert n_q % n_sub == 0 and SB_HEADS % n_heads == 0
    tri1, tri2 = _suffix_sum_matrix(blk), _suffix_sum_matrix(2 * blk)
    lane_w = n_heads * HEAD_DIM
    keys = pl.BlockSpec((1, lq, lane_w), lambda bi, h, i: (bi, 0, h))
    cache = [pl.BlockSpec((1, past, n_heads, HEAD_DIM), lambda bi, h, i: (bi, 0, h, 0))] * 2 if has_past else []
    return pl.pallas_call(
        functools.partial(_sb_kernel, blk=blk, n_sub=n_sub, n_heads=n_heads,
                          key_block_offset=past // blk, has_past=has_past),
        grid=(b, SB_HEADS // n_heads, n_q // n_sub),
        in_specs=[
            pl.BlockSpec((1, n_sub * blk, lane_w), lambda bi, h, i: (bi, i, h)),
            keys, keys, *cache,
            _const_spec(tri1.shape), _const_spec(tri2.shape),
        ],
        out_specs=pl.BlockSpec((1, n_sub * blk, lane_w), lambda bi, h, i: (bi, i, h)),
        out_shape=jax.ShapeDtypeStruct(q.shape, BF16),
        compiler_params=_params(("arbitrary", "arbitrary", "arbitrary")),
        name="sb_attn",
    )(q, k, v, *((past_k, past_v) if has_past else ()), tri1, tri2)


def _mix_kernel(x_ref, u_ref, ysb_ref, gp_ref, gs_ref, hist_ref, mk_ref, mv_ref,
                pool_w_ref, pool_scale_ref, wbp_ref, wbs_ref, wout_ref, gxa_ref, wq_ref, wo_ref,
                o_ref, ubuf_ref, *, tm, pos0):
    l = pl.program_id(1)
    group = u_ref.shape[2] // len(POOL_WINDOWS)

    @pl.when(l == 0)
    def _():
        ubuf_ref[0:POOL_HIST, :] = hist_ref[0]

    u = u_ref[0]
    ubuf_ref[POOL_HIST:POOL_HIST + tm, :] = u
    pos = pos0 + l * tm + lax.broadcasted_iota(jnp.int32, (tm, 1), 0)

    pooled = []
    for g, w in enumerate(POOL_WINDOWS):
        cols = slice(g * group, (g + 1) * group)
        window_sum = u[:, cols]
        for back in range(1, w):
            window_sum = window_sum + ubuf_ref[POOL_HIST - back:POOL_HIST - back + tm, cols]
        count = jnp.minimum(w, pos + 1).astype(F32)
        delta = (window_sum / count - u[:, cols]).astype(BF16)
        pooled.append(jnp.dot(delta, pool_w_ref[g], preferred_element_type=F32))
    y_pool = (jnp.concatenate(pooled, axis=-1) * pool_scale_ref[...]).astype(BF16)

    ubuf_ref[0:POOL_HIST, :] = ubuf_ref[tm:tm + POOL_HIST, :]

    branch_pool = jnp.dot(y_pool, wbp_ref[...], preferred_element_type=F32)
    branch_sb = jnp.dot(ysb_ref[0], wbs_ref[...], preferred_element_type=F32)
    merged = gp_ref[0].astype(F32) * branch_pool + gs_ref[0].astype(F32) * branch_sb
    x1 = x_ref[0] + jnp.dot(merged.astype(BF16), wout_ref[...], preferred_element_type=F32)

    xn = _rmsnorm(x1, gxa_ref[...]).astype(BF16)
    q = (jnp.dot(xn, wq_ref[...], preferred_element_type=F32) * (1.0 / np.sqrt(HEAD_DIM))).astype(BF16)
    heads = []
    for h in range(XA_HEADS):
        cols = slice(h * HEAD_DIM, (h + 1) * HEAD_DIM)
        s = lax.dot_general(q[:, cols], mk_ref[0, :, cols], (((1,), (1,)), ((), ())),
                            preferred_element_type=F32)
        p = jnp.exp(s - jnp.max(s, axis=-1, keepdims=True))
        p = p / jnp.sum(p, axis=-1, keepdims=True)
        heads.append(jnp.dot(p.astype(BF16), mv_ref[0, :, cols], preferred_element_type=F32))
    attn = jnp.concatenate(heads, axis=-1).astype(BF16)
    o_ref[0] = x1 + jnp.dot(attn, wo_ref[...], preferred_element_type=F32)


def _mix(x, u, ysb, gates, hist, mk, mv, pool_w, pool_scale, wbp, wbs, wout, gxa, wq, wo, pos0):
    b, l, d = x.shape
    tm = min(256, l)
    tile = lambda bi, li: (bi, li, 0)
    per_batch = lambda bi, li: (bi, 0, 0)
    return pl.pallas_call(
        functools.partial(_mix_kernel, tm=tm, pos0=pos0),
        grid=(b, l // tm),
        in_specs=[
            pl.BlockSpec((1, tm, d), tile),
            pl.BlockSpec((1, tm, u.shape[2]), tile),
            pl.BlockSpec((1, tm, ysb.shape[2]), tile),
            pl.BlockSpec((1, tm, d), tile),
            pl.BlockSpec((1, tm, d), lambda bi, li: (bi, li, 1)),
            pl.BlockSpec((1,) + hist.shape[1:], per_batch),
            pl.BlockSpec((1,) + mk.shape[1:], per_batch),
            pl.BlockSpec((1,) + mv.shape[1:], per_batch),
            _const_spec(pool_w.shape), _const_spec(pool_scale.shape), _const_spec(wbp.shape),
            _const_spec(wbs.shape), _const_spec(wout.shape), _const_spec(gxa.shape),
            _const_spec(wq.shape), _const_spec(wo.shape),
        ],
        out_specs=pl.BlockSpec((1, tm, d), tile),
        out_shape=jax.ShapeDtypeStruct(x.shape, F32),
        scratch_shapes=[pltpu.VMEM((POOL_HIST + tm, u.shape[2]), F32)],
        compiler_params=_params(("arbitrary", "arbitrary")),
        name="mix",
    )(x, u, ysb, gates, gates, hist, mk, mv, pool_w, pool_scale, wbp, wbs, wout, gxa, wq, wo)


def _ffn_kernel(x_ref, g_ref, wg_ref, wv_ref, cwg_ref, cwv_ref, cbg_ref, cbv_ref, wd_ref, sg_ref, sv_ref,
                gf_ref, y_ref, ng_ref, nv_ref, xn_ref, hg_ref, hv_ref, carry_ref,
                *, tm, seg, chunk, tiles_per_seq):
    i = pl.program_id(0)
    j = pl.program_id(1)
    piece = min(chunk, seg)
    lags = CONV_W - 1
    halves = ((wg_ref, cwg_ref, cbg_ref, sg_ref, ng_ref, hg_ref),
              (wv_ref, cwv_ref, cbv_ref, sv_ref, nv_ref, hv_ref))
    h_row = lambda r: r + CONV_HIST * (r // seg + 1)

    @pl.when(j == 0)
    def _():
        xn_ref[...] = _rmsnorm(x_ref[...], g_ref[...]).astype(BF16)
        y_ref[...] = jnp.zeros_like(y_ref)

    for half, (_, _, _, state_ref, _, h_ref) in enumerate(halves):
        if tiles_per_seq > 1:
            first_of_seq = (i % tiles_per_seq) == 0

            @pl.when(first_of_seq)
            def _(state_ref=state_ref, h_ref=h_ref):
                h_ref[CONV_HIST - lags:CONV_HIST, :] = state_ref[0]

            @pl.when(jnp.logical_not(first_of_seq))
            def _(h_ref=h_ref, half=half):
                h_ref[0:CONV_HIST, :] = carry_ref[half, j]
        else:
            for s in range(tm // seg):
                top = h_row(s * seg)
                h_ref[top - lags:top, :] = state_ref[s]

    acts = []
    for r0 in range(0, tm, chunk):
        conv = []
        for w_ref, cw_ref, cb_ref, _, new_ref, h_ref in halves:
            h = jnp.dot(xn_ref[r0:r0 + chunk, :], w_ref[...], preferred_element_type=F32)
            outs = []
            for p0 in range(0, chunk, piece):
                top = h_row(r0 + p0)
                hp = h[p0:p0 + piece]
                h_ref[top:top + piece, :] = hp
                out = cw_ref[lags:lags + 1, :] * hp
                for tap in range(lags):
                    lag = lags - tap
                    out = out + cw_ref[tap:tap + 1, :] * h_ref[top - lag:top - lag + piece, :]
                outs.append(out + cb_ref[...])
                seg_end = r0 + p0 + piece
                if seg_end % seg == 0:
                    new_ref[seg_end // seg - 1] = hp[piece - lags:, :]
            conv.append(outs[0] if len(outs) == 1 else jnp.concatenate(outs, axis=0))
        gate, val = conv
        acts.append((gate * jax.nn.sigmoid(gate) * val).astype(BF16))
    for c, act in enumerate(acts):
        rows = slice(c * chunk, (c + 1) * chunk)
        y_ref[rows, :] += jnp.dot(act, wd_ref[...], preferred_element_type=F32)

    if tiles_per_seq > 1:
        for half, (_, _, _, _, _, h_ref) in enumerate(halves):
            carry_ref[half, j] = h_ref[tm:tm + CONV_HIST, :]

    @pl.when(j == pl.num_programs(1) - 1)
    def _():
        y_ref[...] = _rmsnorm(x_ref[...] + y_ref[...], gf_ref[...])


def _conv_ffn(x, seq_len, g, w_up, conv_w, conv_b, w_down, state, g_final):
    n, d = x.shape
    f = w_down.shape[0]
    tm = min(512, n)
    tf = w_up.shape[2]
    seg = min(seq_len, tm)
    n_seg = tm // seg
    tiles_per_seq = seq_len // seg
    assert f % tf == 0 and n % tm == 0 and tm % seg == 0 and seq_len % seg == 0
    n_f = f // tf
    h_rows = tm + CONV_HIST * n_seg
    lo = lambda i, j: (0, j)
    hi = lambda i, j: (0, n_f + j)
    w_lo = lambda i, j: (j, 0, 0)
    w_hi = lambda i, j: (n_f + j, 0, 0)
    st_lo = lambda i, j: (i // tiles_per_seq, 0, j)
    st_hi = lambda i, j: (i // tiles_per_seq, 0, n_f + j)
    y, tail_g, tail_v = pl.pallas_call(
        functools.partial(_ffn_kernel, tm=tm, seg=seg, chunk=min(ROW_CHUNK, tm), tiles_per_seq=tiles_per_seq),
        grid=(n // tm, n_f),
        in_specs=[
            pl.BlockSpec((tm, d), lambda i, j: (i, 0)),
            pl.BlockSpec((1, d), lambda i, j: (0, 0)),
            pl.BlockSpec((None, d, tf), w_lo), pl.BlockSpec((None, d, tf), w_hi),
            pl.BlockSpec((CONV_W, tf), lo), pl.BlockSpec((CONV_W, tf), hi),
            pl.BlockSpec((1, tf), lo), pl.BlockSpec((1, tf), hi),
            pl.BlockSpec((tf, d), lambda i, j: (j, 0)),
            pl.BlockSpec((n_seg, CONV_W - 1, tf), st_lo), pl.BlockSpec((n_seg, CONV_W - 1, tf), st_hi),
            pl.BlockSpec((1, d), lambda i, j: (0, 0)),
        ],
        out_specs=(
            pl.BlockSpec((tm, d), lambda i, j: (i, 0)),
            pl.BlockSpec((n_seg, CONV_W - 1, tf), lambda i, j: (i, 0, j)),
            pl.BlockSpec((n_seg, CONV_W - 1, tf), lambda i, j: (i, 0, j)),
        ),
        out_shape=(
            jax.ShapeDtypeStruct((n, d), F32),
            jax.ShapeDtypeStruct((n // seg, CONV_W - 1, f), F32),
            jax.ShapeDtypeStruct((n // seg, CONV_W - 1, f), F32),
        ),
        scratch_shapes=[
            pltpu.VMEM((tm, d), BF16),
            pltpu.VMEM((h_rows, tf), F32),
            pltpu.VMEM((h_rows, tf), F32),
            pltpu.VMEM((2, n_f, CONV_HIST, tf), F32),
        ],
        compiler_params=_params(("arbitrary", "arbitrary")),
        name="conv_ffn",
    )(x, g, w_up, w_up, conv_w, conv_w, conv_b, conv_b, w_down, state, state, g_final)
    last_tile = slice(tiles_per_seq - 1, None, tiles_per_seq)
    return y, tail_g[last_tile], tail_v[last_tile]


def _layer(x, pool_state, past_k, past_v, mk, mv, conv_state, p):
    b, l, d = x.shape
    n = b * l
    assert l >= POOL_HIST and l >= CONV_W - 1
    u, q, kf, kb, vf, vb, gates = _in_proj(x.reshape(n, d), p["norm_mix_g"], p["w_in"], p["b_gate"])
    width = u.shape[1]
    q = q.reshape(b, l, width)
    kb = kb.reshape(b, l, width)
    vb = vb.reshape(b, l, width)
    if past_k is None:
        past = 0
        hist = jnp.zeros((b, POOL_HIST, width), F32)
    else:
        past = past_k.shape[1]
        hist = jnp.pad(pool_state, ((0, 0), (POOL_HIST - pool_state.shape[1], 0), (0, 0)))
    y_sb = _sb_attention(q, kb, vb, past_k, past_v)

    x2 = _mix(x, u.reshape(b, l, width), y_sb, gates.reshape(b, l, 2 * d), hist, mk, mv,
              p["pool_w"], p["pool_scale"], p["w_branch_pool"], p["w_branch_sb"], p["w_out"],
              p["norm_xa_g"], p["xa_wq"], p["xa_wo"], past)

    y, new_g, new_v = _conv_ffn(x2.reshape(n, d), l, p["norm_ffn_g"], p["ffn_w_up"], p["ffn_conv_w"],
                                p["ffn_conv_b"], p["ffn_w_down"], conv_state, p["norm_final_g"])
    new_pool = u.reshape(b, l, width)[:, l - (POOL_HIST - 1):, :]
    new_conv = jnp.concatenate([new_g, new_v], axis=-1)
    return (y.reshape(b, l, d), new_pool, kf.reshape(b, l, SB_HEADS, HEAD_DIM),
            vf.reshape(b, l, SB_HEADS, HEAD_DIM), new_conv)


def kernel(x_prompt, x_sample, mem_prompt, state_pool, cache_sb_k, cache_sb_v, cache_mem_k, cache_mem_v, state_ffn_conv, norm_mix_g, w_in, b_gate, pool_w, pool_scale, w_branch_pool, w_branch_sb, w_out, norm_xa_g, norm_mem_g, xa_wq, xa_wkv, xa_wo, norm_ffn_g, ffn_w_up, ffn_conv_w, ffn_conv_b, ffn_w_down, norm_final_g):
    assert norm_mix_g.shape[0] == 1, "single-layer step"
    bp, lp, d = x_prompt.shape
    n_mem = mem_prompt.shape[1]
    xa_width = xa_wq.shape[2]
    row = lambda a: a.reshape(1, -1)
    p = {
        "norm_mix_g": row(norm_mix_g[0]), "w_in": _column_tiles(w_in[0], d // 2), "b_gate": row(b_gate[0]),
        "pool_w": pool_w[0].astype(BF16), "pool_scale": row(pool_scale[0]),
        "w_branch_pool": w_branch_pool[0].astype(BF16), "w_branch_sb": w_branch_sb[0].astype(BF16),
        "w_out": w_out[0].astype(BF16), "norm_xa_g": row(norm_xa_g[0]),
        "xa_wq": xa_wq[0].astype(BF16), "xa_wo": xa_wo[0].astype(BF16),
        "norm_ffn_g": row(norm_ffn_g[0]), "ffn_w_up": _column_tiles(ffn_w_up[0], FFN_TILE),
        "ffn_conv_w": ffn_conv_w[0], "ffn_conv_b": row(ffn_conv_b[0]),
        "ffn_w_down": ffn_w_down[0].astype(BF16), "norm_final_g": row(norm_final_g),
    }

    mem_kv = _norm_matmul(mem_prompt.reshape(bp * n_mem, d), row(norm_mem_g[0]), xa_wkv[0].astype(BF16))
    mem_kv = mem_kv.reshape(bp, n_mem, 2 * xa_width)
    mk_p, mv_p = mem_kv[..., :xa_width], mem_kv[..., xa_width:]

    conv0 = jnp.zeros((bp, CONV_W - 1, ffn_w_up.shape[2]), F32)
    y_p, pool_p, k_p, v_p, conv_p = _layer(x_prompt, None, None, None, mk_p.astype(BF16), mv_p.astype(BF16),
                                           conv0, p)

    bs = x_sample.shape[0]
    mk_s = cache_mem_k[0].reshape(bs, n_mem, xa_width).astype(BF16)
    mv_s = cache_mem_v[0].reshape(bs, n_mem, xa_width).astype(BF16)
    y_s, pool_s, k_s, v_s, conv_s = _layer(x_sample, state_pool[0], cache_sb_k[0], cache_sb_v[0], mk_s, mv_s,
                                           state_ffn_conv[0], p)

    mem_shape = (1, bp, n_mem, XA_HEADS, HEAD_DIM)
    return (y_p, y_s,
            pool_p[None], k_p[None], v_p[None], mk_p.reshape(mem_shape), mv_p.reshape(mem_shape), conv_p[None],
            pool_s[None], k_s[None], v_s[None], conv_s[None])
```

```python
import functools

import jax
import jax.numpy as jnp
import numpy as np
from jax import lax
from jax.experimental import pallas as pl
from jax.experimental.pallas import tpu as pltpu

F32 = jnp.float32
BF16 = jnp.bfloat16

EPS = 1e-6
POOL_WINDOWS = (2, 4, 8, 16)
POOL_HIST = 16
SB_HEADS = 8
HEAD_DIM = 128
XA_HEADS = 4
CONV_W = 3
CONV_HIST = 8
LANES = 128
FFN_TILE = 512
ROW_CHUNK = 128
FFN_ROW_CHUNK = 256

SB_DEAD_LOG = -88.0

VMEM_LIMIT = 56 * 1024 * 1024


def _rmsnorm(xf, g):
    ms = jnp.mean(xf * xf, axis=-1, keepdims=True)
    return xf * lax.rsqrt(ms + EPS) * g


def _const_spec(shape):
    zeros = (0,) * len(shape)
    return pl.BlockSpec(shape, lambda *_: zeros, pipeline_mode=pl.Buffered(1))


def _params(semantics):
    return pltpu.CompilerParams(dimension_semantics=semantics, vmem_limit_bytes=VMEM_LIMIT)


def _in_proj_kernel(x_ref, g_ref, w_ref, b_ref, u_ref, q_ref, kf_ref, kb_ref, vf_ref, vb_ref, gate_ref,
                    xn_ref, *, q_scale, chunk):
    j = pl.program_id(1)
    tm = x_ref.shape[0]

    @pl.when(j == 0)
    def _():
        xn_ref[...] = _rmsnorm(x_ref[...], g_ref[...]).astype(BF16)

    def project(epilogue):
        for r0 in range(0, tm, chunk):
            rows = slice(r0, r0 + chunk)
            epilogue(rows, jnp.dot(xn_ref[rows, :], w_ref[...], preferred_element_type=F32))

    def put_u(rows, acc):
        u_ref[rows, :] = acc

    def put_q(rows, acc):
        q_ref[rows, :] = (acc * q_scale).astype(BF16)

    def put_heads(ref, rows, acc):
        ref[rows, :, :] = acc.reshape(acc.shape[0], ref.shape[1], ref.shape[2])

    def put_k(rows, acc):
        put_heads(kf_ref, rows, acc)
        kb_ref[rows, :] = acc.astype(BF16)

    def put_v(rows, acc):
        put_heads(vf_ref, rows, acc)
        vb_ref[rows, :] = acc.astype(BF16)

    def put_gate(rows, acc):
        gate_ref[rows, :] = jax.nn.sigmoid(acc + b_ref[...]).astype(BF16)

    pl.when(j == 0)(functools.partial(project, put_u))
    pl.when(j == 1)(functools.partial(project, put_q))
    pl.when(j == 2)(functools.partial(project, put_k))
    pl.when(j == 3)(functools.partial(project, put_v))
    pl.when(j >= 4)(functools.partial(project, put_gate))


def _in_proj(x, g, w_in, b_gate):
    n, d = x.shape
    width = d // 2
    n_col = w_in.shape[1] // width
    heads = width // HEAD_DIM
    tm = min(512, n)
    grid = (n // tm, n_col)
    row = lambda i, j: (i, 0)
    gate_col = lambda i, j: (i, jnp.maximum(j - 4, 0))
    out_shape = (
        jax.ShapeDtypeStruct((n, width), F32),
        jax.ShapeDtypeStruct((n, width), BF16),
        jax.ShapeDtypeStruct((n, heads, HEAD_DIM), F32),
        jax.ShapeDtypeStruct((n, width), BF16),
        jax.ShapeDtypeStruct((n, heads, HEAD_DIM), F32),
        jax.ShapeDtypeStruct((n, width), BF16),
        jax.ShapeDtypeStruct((n, 2 * d), BF16),
    )
    flat = pl.BlockSpec((tm, width), row)
    by_head = pl.BlockSpec((tm, heads, HEAD_DIM), lambda i, j: (i, 0, 0))
    out_specs = (flat, flat, by_head, flat, by_head, flat, pl.BlockSpec((tm, width), gate_col))
    return pl.pallas_call(
        functools.partial(_in_proj_kernel, q_scale=1.0 / np.sqrt(HEAD_DIM), chunk=min(ROW_CHUNK, tm)),
        grid=grid,
        in_specs=[
            pl.BlockSpec((tm, d), row),
            pl.BlockSpec((1, d), lambda i, j: (0, 0)),
            pl.BlockSpec((d, width), lambda i, j: (0, j)),
            pl.BlockSpec((1, width), lambda i, j: (0, jnp.maximum(j - 4, 0))),
        ],
        out_specs=out_specs,
        out_shape=out_shape,
        scratch_shapes=[pltpu.VMEM((tm, d), BF16)],
        compiler_params=_params(("arbitrary", "arbitrary")),
        name="in_proj",
    )(x, g, w_in, b_gate)


def _norm_matmul_kernel(x_ref, g_ref, w_ref, o_ref):
    xn = _rmsnorm(x_ref[...], g_ref[...]).astype(BF16)
    o_ref[...] = jnp.dot(xn, w_ref[...], preferred_element_type=F32)


def _norm_matmul(x, g, w):
    n, d = x.shape
    tm = min(256, n)
    return pl.pallas_call(
        _norm_matmul_kernel,
        grid=(n // tm,),
        in_specs=[pl.BlockSpec((tm, d), lambda i: (i, 0)), _const_spec((1, d)), _const_spec(w.shape)],
        out_specs=pl.BlockSpec((tm, w.shape[1]), lambda i: (i, 0)),
        out_shape=jax.ShapeDtypeStruct((n, w.shape[1]), F32),
        compiler_params=_params(("arbitrary",)),
        name="mem_kv",
    )(x, g, w)


def _sb_kernel(*refs, blk, n_sub, n_heads, key_block_offset, has_past):
    if has_past:
        q_ref, k_ref, v_ref, past_k_ref, past_v_ref, tri1_ref, tri2_ref, o_ref = refs
    else:
        q_ref, k_ref, v_ref, tri1_ref, tri2_ref, o_ref = refs
    step = pl.program_id(2)
    first_span = 2
    wide = first_span * blk
    col_minus_row = (lax.broadcasted_iota(jnp.int32, (blk, wide), 1)
                     - lax.broadcasted_iota(jnp.int32, (blk, wide), 0))
    chains = [(sub, h) for sub in range(n_sub) for h in range(n_heads)]
    rows = lambda sub: slice(sub * blk, (sub + 1) * blk)
    lanes = lambda h: slice(h * HEAD_DIM, (h + 1) * HEAD_DIM)
    qs = [q_ref[0, rows(sub), lanes(h)] for sub, h in chains]
    first_diag = step * n_sub + key_block_offset

    def visit(newest, span, accs, runs, first):
        width = span * blk
        tri = (tri1_ref if span == 1 else tri2_ref)[...]
        log_nots, log_betas, vs, masks = [], [], [], []
        runs = list(runs)
        for c, (sub, h) in enumerate(chains):
            kb_new = first_diag + sub - newest
            kb_old = kb_new - (span - 1)
            start_blk = jnp.maximum(kb_old, 0)
            if first:
                masks.append(col_minus_row[:, :width] < (kb_new - start_blk) * blk)
            else:
                runs[c] = jnp.where(kb_old >= 0, runs[c], -1e30)
            start = pl.multiple_of(start_blk * blk, blk)
            if not has_past:
                k = k_ref[0, pl.ds(start, width), lanes(h)]
                v = v_ref[0, pl.ds(start, width), lanes(h)]
            elif first:
                newest_past = slice((key_block_offset - 1) * blk, key_block_offset * blk)
                k = jnp.concatenate([past_k_ref[0, newest_past, h, :].astype(BF16), k_ref[0, :, lanes(h)]], axis=0)
                v = jnp.concatenate([past_v_ref[0, newest_past, h, :].astype(BF16), v_ref[0, :, lanes(h)]], axis=0)
            else:
                k = past_k_ref[0, pl.ds(start, width), h, :].astype(BF16)
                v = past_v_ref[0, pl.ds(start, width), h, :].astype(BF16)
            vs.append(v)
            z = lax.dot_general(qs[c], k, (((1,), (1,)), ((), ())), preferred_element_type=F32)
            t = jnp.log(1.0 + jnp.exp(-jnp.abs(z)))
            log_not = -(jnp.maximum(z, 0.0) + t)
            if first:
                log_not = jnp.where(masks[c], log_not, 0.0)
            log_nots.append(log_not)
            log_betas.append(jnp.minimum(z, 0.0) - t)
        stacked = jnp.concatenate(log_nots, axis=0)
        hi = stacked.astype(BF16)
        lo = (stacked - hi.astype(F32)).astype(BF16)
        sums = (jnp.dot(hi, tri, preferred_element_type=F32)
                + jnp.dot(lo, tri, preferred_element_type=F32))
        new_accs, new_runs = [], []
        for c in range(len(chains)):
            part = sums[c * blk:(c + 1) * blk]
            between = part[:, :width]
            total = part[:, width:]
            if first:
                a = jnp.where(masks[c], jnp.exp(log_betas[c] + between), 0.0)
            else:
                a = jnp.exp(log_betas[c] + between + runs[c][:, :width])
            new_accs.append(accs[c] + jnp.dot(a.astype(BF16), vs[c], preferred_element_type=F32))
            new_runs.append(runs[c] + total)
        return tuple(new_accs), tuple(new_runs)

    zeros = tuple(jnp.zeros((blk, LANES), F32) for _ in chains)
    accs, runs = visit(0, first_span, zeros, zeros, True)

    def alive(carry):
        newest, _, runs = carry
        live = functools.reduce(jnp.maximum, runs)
        return jnp.logical_and(first_diag + n_sub - 1 - newest >= 0, jnp.max(live) > SB_DEAD_LOG)

    def older(carry):
        newest, accs, runs = carry
        accs, runs = visit(newest, 1, accs, runs, False)
        return newest + 1, accs, runs

    _, accs, _ = lax.while_loop(alive, older, (jnp.int32(first_span), accs, runs))
    for c, (sub, h) in enumerate(chains):
        o_ref[0, rows(sub), lanes(h)] = accs[c].astype(BF16)


def _suffix_sum_matrix(width):
    jj = np.arange(width)
    return jnp.asarray(np.concatenate([jj[:, None] > jj[None, :], np.ones((width, LANES), bool)], axis=1), BF16)


def _sb_attention(q, k, v, past_k=None, past_v=None):
    b, lq, width = q.shape
    has_past = past_k is not None
    past = past_k.shape[1] if has_past else 0
    lk = past + lq
    blk = min(128, lq)
    assert lq % blk == 0 and past % blk == 0 and lk >= 2 * blk
    assert not has_past or lq == blk, "with a cache the new keys must form one block"
    n_q = lq // blk
    n_chains = 8
    n_sub = min(n_chains, n_q)
    n_heads = n_chains // n_sub
    assert n_q % n_sub == 0 and SB_HEADS % n_heads == 0
    tri1, tri2 = _suffix_sum_matrix(blk), _suffix_sum_matrix(2 * blk)
    lane_w = n_heads * HEAD_DIM
    keys = pl.BlockSpec((1, lq, lane_w), lambda bi, h, i: (bi, 0, h))
    cache = [pl.BlockSpec((1, past, n_heads, HEAD_DIM), lambda bi, h, i: (bi, 0, h, 0))] * 2 if has_past else []
    return pl.pallas_call(
        functools.partial(_sb_kernel, blk=blk, n_sub=n_sub, n_heads=n_heads,
                          key_block_offset=past // blk, has_past=has_past),
        grid=(b, SB_HEADS // n_heads, n_q // n_sub),
        in_specs=[
            pl.BlockSpec((1, n_sub * blk, lane_w), lambda bi, h, i: (bi, i, h)),
            keys, keys, *cache,
            _const_spec(tri1.shape), _const_spec(tri2.shape),
        ],
        out_specs=pl.BlockSpec((1, n_sub * blk, lane_w), lambda bi, h, i: (bi, i, h)),
        out_shape=jax.ShapeDtypeStruct(q.shape, BF16),
        compiler_params=_params(("arbitrary", "arbitrary", "arbitrary")),
        name="sb_attn",
    )(q, k, v, *((past_k, past_v) if has_past else ()), tri1, tri2)


def _mix_kernel(x_ref, u_ref, ysb_ref, gp_ref, gs_ref, hist_ref, mk_ref, mv_ref,
                pool_w_ref, pool_scale_ref, wbp_ref, wbs_ref, wout_ref, gxa_ref, wq_ref, wo_ref,
                o_ref, ubuf_ref, *, tm, seg, chunk, tiles_per_seq, pos0):
    i = pl.program_id(0)
    group = u_ref.shape[1] // len(POOL_WINDOWS)
    piece = min(chunk, seg)
    u_row = lambda r: r + POOL_HIST * (r // seg + 1)

    if tiles_per_seq > 1:
        tile_in_seq = i % tiles_per_seq

        @pl.when(tile_in_seq == 0)
        def _():
            ubuf_ref[0:POOL_HIST, :] = hist_ref[0]

        seq_row0 = tile_in_seq * tm
    else:
        for s in range(tm // seg):
            top = u_row(s * seg)
            ubuf_ref[top - POOL_HIST:top, :] = hist_ref[s]
        seq_row0 = 0

    for r0 in range(0, tm, chunk):
        rows = slice(r0, r0 + chunk)
        pieces = range(r0, r0 + chunk, piece)
        for r in pieces:
            ubuf_ref[u_row(r):u_row(r) + piece, :] = u_ref[r:r + piece, :]
        pooled = []
        for g, w in enumerate(POOL_WINDOWS):
            cols = slice(g * group, (g + 1) * group)
            deltas = []
            for r in pieces:
                top = u_row(r)
                u = u_ref[r:r + piece, cols]
                window_sum = u
                for back in range(1, w):
                    window_sum = window_sum + ubuf_ref[top - back:top - back + piece, cols]
                pos = pos0 + seq_row0 + r % seg + lax.broadcasted_iota(jnp.int32, (piece, 1), 0)
                count = jnp.minimum(w, pos + 1).astype(F32)
                deltas.append((window_sum / count - u).astype(BF16))
            delta = deltas[0] if len(deltas) == 1 else jnp.concatenate(deltas, axis=0)
            pooled.append(jnp.dot(delta, pool_w_ref[g], preferred_element_type=F32))
        y_pool = (jnp.concatenate(pooled, axis=-1) * pool_scale_ref[...]).astype(BF16)

        branch_pool = jnp.dot(y_pool, wbp_ref[...], preferred_element_type=F32)
        branch_sb = jnp.dot(ysb_ref[rows, :], wbs_ref[...], preferred_element_type=F32)
        merged = gp_ref[rows, :].astype(F32) * branch_pool + gs_ref[rows, :].astype(F32) * branch_sb
        x1 = x_ref[rows, :] + jnp.dot(merged.astype(BF16), wout_ref[...], preferred_element_type=F32)

        xn = _rmsnorm(x1, gxa_ref[...]).astype(BF16)
        q = (jnp.dot(xn, wq_ref[...], preferred_element_type=F32) * (1.0 / np.sqrt(HEAD_DIM))).astype(BF16)
        attended = []
        for r in pieces:
            s = r // seg
            heads = []
            for h in range(XA_HEADS):
                cols = slice(h * HEAD_DIM, (h + 1) * HEAD_DIM)
                sc = lax.dot_general(q[r - r0:r - r0 + piece, cols], mk_ref[s, :, cols], (((1,), (1,)), ((), ())),
                                     preferred_element_type=F32)
                p = jnp.exp(sc - jnp.max(sc, axis=-1, keepdims=True))
                p = p / jnp.sum(p, axis=-1, keepdims=True)
                heads.append(jnp.dot(p.astype(BF16), mv_ref[s, :, cols], preferred_element_type=F32))
            attended.append(jnp.concatenate(heads, axis=-1).astype(BF16))
        attn = attended[0] if len(attended) == 1 else jnp.concatenate(attended, axis=0)
        o_ref[rows, :] = x1 + jnp.dot(attn, wo_ref[...], preferred_element_type=F32)

    if tiles_per_seq > 1:
        ubuf_ref[0:POOL_HIST, :] = ubuf_ref[tm:tm + POOL_HIST, :]


def _mix(x, seq_len, u, ysb, gates, hist, mk, mv, pool_w, pool_scale, wbp, wbs, wout, gxa, wq, wo, pos0):
    n, d = x.shape
    tm = min(256, n)
    seg = min(seq_len, tm)
    n_seg = tm // seg
    tiles_per_seq = seq_len // seg
    assert n % tm == 0 and tm % seg == 0 and seq_len % seg == 0
    tile = lambda i: (i, 0)
    per_seq = lambda i: (i // tiles_per_seq, 0, 0)
    return pl.pallas_call(
        functools.partial(_mix_kernel, tm=tm, seg=seg, chunk=tm, tiles_per_seq=tiles_per_seq,
                          pos0=pos0),
        grid=(n // tm,),
        in_specs=[
            pl.BlockSpec((tm, d), tile),
            pl.BlockSpec((tm, u.shape[1]), tile),
            pl.BlockSpec((tm, ysb.shape[1]), tile),
            pl.BlockSpec((tm, d), tile),
            pl.BlockSpec((tm, d), lambda i: (i, 1)),
            pl.BlockSpec((n_seg,) + hist.shape[1:], per_seq),
            pl.BlockSpec((n_seg,) + mk.shape[1:], per_seq),
            pl.BlockSpec((n_seg,) + mv.shape[1:], per_seq),
            _const_spec(pool_w.shape), _const_spec(pool_scale.shape), _const_spec(wbp.shape),
            _const_spec(wbs.shape), _const_spec(wout.shape), _const_spec(gxa.shape),
            _const_spec(wq.shape), _const_spec(wo.shape),
        ],
        out_specs=pl.BlockSpec((tm, d), tile),
        out_shape=jax.ShapeDtypeStruct(x.shape, F32),
        scratch_shapes=[pltpu.VMEM((tm + POOL_HIST * n_seg, u.shape[1]), F32)],
        compiler_params=_params(("arbitrary",)),
        name="mix",
    )(x, u, ysb, gates, gates, hist, mk, mv, pool_w, pool_scale, wbp, wbs, wout, gxa, wq, wo)


def _ffn_kernel(x_ref, g_ref, wg_ref, wv_ref, cwg_ref, cwv_ref, cbg_ref, cbv_ref, wd_ref, sg_ref, sv_ref,
                gf_ref, y_ref, ng_ref, nv_ref, xn_ref, hg_ref, hv_ref, carry_ref,
                *, tm, seg, chunk, tiles_per_seq):
    i = pl.program_id(0)
    j = pl.program_id(1)
    piece = min(chunk, seg)
    lags = CONV_W - 1
    halves = ((wg_ref, cwg_ref, cbg_ref, sg_ref, ng_ref, hg_ref),
              (wv_ref, cwv_ref, cbv_ref, sv_ref, nv_ref, hv_ref))
    h_row = lambda r: r + CONV_HIST * (r // seg + 1)

    @pl.when(j == 0)
    def _():
        xn_ref[...] = _rmsnorm(x_ref[...], g_ref[...]).astype(BF16)
        y_ref[...] = jnp.zeros_like(y_ref)

    for half, (_, _, _, state_ref, _, h_ref) in enumerate(halves):
        if tiles_per_seq > 1:
            first_of_seq = (i % tiles_per_seq) == 0

            @pl.when(first_of_seq)
            def _(state_ref=state_ref, h_ref=h_ref):
                h_ref[CONV_HIST - lags:CONV_HIST, :] = state_ref[0]

            @pl.when(jnp.logical_not(first_of_seq))
            def _(h_ref=h_ref, half=half):
                h_ref[0:CONV_HIST, :] = carry_ref[half, j]
        else:
            for s in range(tm // seg):
                top = h_row(s * seg)
                h_ref[top - lags:top, :] = state_ref[s]

    acts = []
    for r0 in range(0, tm, chunk):
        conv = []
        for w_ref, cw_ref, cb_ref, _, new_ref, h_ref in halves:
            h = jnp.dot(xn_ref[r0:r0 + chunk, :], w_ref[...], preferred_element_type=F32)
            outs = []
            for p0 in range(0, chunk, piece):
                top = h_row(r0 + p0)
                hp = h[p0:p0 + piece]
                h_ref[top:top + piece, :] = hp
                out = cw_ref[lags:lags + 1, :] * hp
                for tap in range(lags):
                    lag = lags - tap
                    out = out + cw_ref[tap:tap + 1, :] * h_ref[top - lag:top - lag + piece, :]
                outs.append(out + cb_ref[...])
                seg_end = r0 + p0 + piece
                if seg_end % seg == 0:
                    new_ref[seg_end // seg - 1] = hp[piece - lags:, :]
            conv.append(outs[0] if len(outs) == 1 else jnp.concatenate(outs, axis=0))
        gate, val = conv
        acts.append((gate * jax.nn.sigmoid(gate) * val).astype(BF16))
    for c, act in enumerate(acts):
        rows = slice(c * chunk, (c + 1) * chunk)
        y_ref[rows, :] += jnp.dot(act, wd_ref[...], preferred_element_type=F32)

    if tiles_per_seq > 1:
        for half, (_, _, _, _, _, h_ref) in enumerate(halves):
            carry_ref[half, j] = h_ref[tm:tm + CONV_HIST, :]

    @pl.when(j == pl.num_programs(1) - 1)
    def _():
        y_ref[...] = _rmsnorm(x_ref[...] + y_ref[...], gf_ref[...])


def _conv_ffn(x, seq_len, g, w_up, conv_w, conv_b, w_down, state, g_final):
    n, d = x.shape
    f = w_down.shape[0]
    tm = min(512, n)
    tf = FFN_TILE
    seg = min(seq_len, tm)
    n_seg = tm // seg
    tiles_per_seq = seq_len // seg
    assert f % tf == 0 and n % tm == 0 and tm % seg == 0 and seq_len % seg == 0
    n_f = f // tf
    h_rows = tm + CONV_HIST * n_seg
    lo = lambda i, j: (0, j)
    hi = lambda i, j: (0, n_f + j)
    st_lo = lambda i, j: (i // tiles_per_seq, 0, j)
    st_hi = lambda i, j: (i // tiles_per_seq, 0, n_f + j)
    y, tail_g, tail_v = pl.pallas_call(
        functools.partial(_ffn_kernel, tm=tm, seg=seg, chunk=min(FFN_ROW_CHUNK, tm), tiles_per_seq=tiles_per_seq),
        grid=(n // tm, n_f),
        in_specs=[
            pl.BlockSpec((tm, d), lambda i, j: (i, 0)),
            pl.BlockSpec((1, d), lambda i, j: (0, 0)),
            pl.BlockSpec((d, tf), lo), pl.BlockSpec((d, tf), hi),
            pl.BlockSpec((CONV_W, tf), lo), pl.BlockSpec((CONV_W, tf), hi),
            pl.BlockSpec((1, tf), lo), pl.BlockSpec((1, tf), hi),
            pl.BlockSpec((tf, d), lambda i, j: (j, 0)),
            pl.BlockSpec((n_seg, CONV_W - 1, tf), st_lo), pl.BlockSpec((n_seg, CONV_W - 1, tf), st_hi),
            pl.BlockSpec((1, d), lambda i, j: (0, 0)),
        ],
        out_specs=(
            pl.BlockSpec((tm, d), lambda i, j: (i, 0)),
            pl.BlockSpec((n_seg, CONV_W - 1, tf), lambda i, j: (i, 0, j)),
            pl.BlockSpec((n_seg, CONV_W - 1, tf), lambda i, j: (i, 0, j)),
        ),
        out_shape=(
            jax.ShapeDtypeStruct((n, d), F32),
            jax.ShapeDtypeStruct((n // seg, CONV_W - 1, f), F32),
            jax.ShapeDtypeStruct((n // seg, CONV_W - 1, f), F32),
        ),
        scratch_shapes=[
            pltpu.VMEM((tm, d), BF16),
            pltpu.VMEM((h_rows, tf), F32),
            pltpu.VMEM((h_rows, tf), F32),
            pltpu.VMEM((2, n_f, CONV_HIST, tf), F32),
        ],
        compiler_params=_params(("arbitrary", "arbitrary")),
        name="conv_ffn",
    )(x, g, w_up, w_up, conv_w, conv_w, conv_b, conv_b, w_down, state, state, g_final)
    last_tile = slice(tiles_per_seq - 1, None, tiles_per_seq)
    return y, tail_g[last_tile], tail_v[last_tile]


def _layer(x, pool_state, past_k, past_v, mk, mv, conv_state, p):
    b, l, d = x.shape
    n = b * l
    assert l >= POOL_HIST and l >= CONV_W - 1
    x = x.reshape(n, d)
    u, q, kf, kb, vf, vb, gates = _in_proj(x, p["norm_mix_g"], p["w_in"], p["b_gate"])
    width = u.shape[1]
    if past_k is None:
        past = 0
        hist = jnp.zeros((b, POOL_HIST, width), F32)
    else:
        past = past_k.shape[1]
        hist = jnp.pad(pool_state, ((0, 0), (POOL_HIST - pool_state.shape[1], 0), (0, 0)))
    by_seq = lambda a: a.reshape(b, l, width)
    y_sb = _sb_attention(by_seq(q), by_seq(kb), by_seq(vb), past_k, past_v)

    x2 = _mix(x, l, u, y_sb.reshape(n, width), gates, hist, mk, mv,
              p["pool_w"], p["pool_scale"], p["w_branch_pool"], p["w_branch_sb"], p["w_out"],
              p["norm_xa_g"], p["xa_wq"], p["xa_wo"], past)

    y, new_g, new_v = _conv_ffn(x2, l, p["norm_ffn_g"], p["ffn_w_up"], p["ffn_conv_w"],
                                p["ffn_conv_b"], p["ffn_w_down"], conv_state, p["norm_final_g"])
    new_pool = by_seq(u)[:, l - (POOL_HIST - 1):, :]
    new_conv = jnp.concatenate([new_g, new_v], axis=-1)
    return (y.reshape(b, l, d), new_pool, kf.reshape(b, l, SB_HEADS, HEAD_DIM),
            vf.reshape(b, l, SB_HEADS, HEAD_DIM), new_conv)


def kernel(x_prompt, x_sample, mem_prompt, state_pool, cache_sb_k, cache_sb_v, cache_mem_k, cache_mem_v, state_ffn_conv, norm_mix_g, w_in, b_gate, pool_w, pool_scale, w_branch_pool, w_branch_sb, w_out, norm_xa_g, norm_mem_g, xa_wq, xa_wkv, xa_wo, norm_ffn_g, ffn_w_up, ffn_conv_w, ffn_conv_b, ffn_w_down, norm_final_g):
    assert norm_mix_g.shape[0] == 1, "single-layer step"
    bp, lp, d = x_prompt.shape
    n_mem = mem_prompt.shape[1]
    xa_width = xa_wq.shape[2]
    row = lambda a: a.reshape(1, -1)
    p = {
        "norm_mix_g": row(norm_mix_g[0]), "w_in": w_in[0].astype(BF16), "b_gate": row(b_gate[0]),
        "pool_w": pool_w[0].astype(BF16), "pool_scale": row(pool_scale[0]),
        "w_branch_pool": w_branch_pool[0].astype(BF16), "w_branch_sb": w_branch_sb[0].astype(BF16),
        "w_out": w_out[0].astype(BF16), "norm_xa_g": row(norm_xa_g[0]),
        "xa_wq": xa_wq[0].astype(BF16), "xa_wo": xa_wo[0].astype(BF16),
        "norm_ffn_g": row(norm_ffn_g[0]), "ffn_w_up": ffn_w_up[0].astype(BF16),
        "ffn_conv_w": ffn_conv_w[0], "ffn_conv_b": row(ffn_conv_b[0]),
        "ffn_w_down": ffn_w_down[0].astype(BF16), "norm_final_g": row(norm_final_g),
    }

    mem_kv = _norm_matmul(mem_prompt.reshape(bp * n_mem, d), row(norm_mem_g[0]), xa_wkv[0].astype(BF16))
    mem_kv = mem_kv.reshape(bp, n_mem, 2 * xa_width)
    mk_p, mv_p = mem_kv[..., :xa_width], mem_kv[..., xa_width:]

    conv0 = jnp.zeros((bp, CONV_W - 1, ffn_w_up.shape[2]), F32)
    y_p, pool_p, k_p, v_p, conv_p = _layer(x_prompt, None, None, None, mk_p.astype(BF16), mv_p.astype(BF16),
                                           conv0, p)

    bs = x_sample.shape[0]
    mk_s = cache_mem_k[0].reshape(bs, n_mem, xa_width).astype(BF16)
    mv_s = cache_mem_v[0].reshape(bs, n_mem, xa_width).astype(BF16)
    y_s, pool_s, k_s, v_s, conv_s = _layer(x_sample, state_pool[0], cache_sb_k[0], cache_sb_v[0], mk_s, mv_s,
                                           state_ffn_conv[0], p)

    mem_shape = (1, bp, n_mem, XA_HEADS, HEAD_DIM)
    return (y_p, y_s,
            pool_p[None], k_p[None], v_p[None], mk_p.reshape(mem_shape), mv_p.reshape(mem_shape), conv_p[None],
            pool_s[None], k_s[None], v_s[None], conv_s[None])
```

```python
import functools

import jax
import jax.numpy as jnp
import numpy as np
from jax import lax
from jax.experimental import pallas as pl
from jax.experimental.pallas import tpu as pltpu

F32 = jnp.float32
BF16 = jnp.bfloat16

EPS = 1e-6
POOL_WINDOWS = (2, 4, 8, 16)
POOL_HIST = 16
SB_HEADS = 8
HEAD_DIM = 128
XA_HEADS = 4
CONV_W = 3
CONV_HIST = 8
LANES = 128
FFN_TILE = 512
ROW_CHUNK = 128
FFN_ROW_CHUNK = 256

SB_DEAD_LOG = -88.0

VMEM_LIMIT = 56 * 1024 * 1024


def _rmsnorm(xf, g):
    ms = jnp.mean(xf * xf, axis=-1, keepdims=True)
    return xf * lax.rsqrt(ms + EPS) * g


def _const_spec(shape):
    zeros = (0,) * len(shape)
    return pl.BlockSpec(shape, lambda *_: zeros, pipeline_mode=pl.Buffered(1))


def _params(semantics):
    return pltpu.CompilerParams(dimension_semantics=semantics, vmem_limit_bytes=VMEM_LIMIT)


def _in_proj_kernel(x_ref, g_ref, w_ref, b_ref, u_ref, q_ref, kf_ref, kb_ref, vf_ref, vb_ref, gate_ref,
                    *, q_scale, chunk):
    j = pl.program_id(0)
    tm = x_ref.shape[0]
    half = u_ref.shape[1]

    def project(epilogue):
        for r0 in range(0, tm, chunk):
            rows = slice(r0, r0 + chunk)
            xn = _rmsnorm(x_ref[rows, :], g_ref[...]).astype(BF16)
            epilogue(rows, jnp.dot(xn, w_ref[...], preferred_element_type=F32))

    def put_uq(rows, acc):
        u_ref[rows, :] = acc[:, :half]
        q_ref[rows, :] = (acc[:, half:] * q_scale).astype(BF16)

    def put_heads(f32_ref, bf16_ref, rows, acc):
        f32_ref[rows, :, :] = acc.reshape(acc.shape[0], f32_ref.shape[1], f32_ref.shape[2])
        bf16_ref[rows, :] = acc.astype(BF16)

    def put_kv(rows, acc):
        put_heads(kf_ref, kb_ref, rows, acc[:, :half])
        put_heads(vf_ref, vb_ref, rows, acc[:, half:])

    def put_gate(rows, acc):
        gate_ref[rows, :] = jax.nn.sigmoid(acc + b_ref[...]).astype(BF16)

    pl.when(j == 0)(functools.partial(project, put_uq))
    pl.when(j == 1)(functools.partial(project, put_kv))
    pl.when(j >= 2)(functools.partial(project, put_gate))


def _in_proj(x, g, w_in, b_gate):
    n, d = x.shape
    half = d // 2
    n_col = w_in.shape[1] // d
    heads = half // HEAD_DIM
    tm = min(512, n)
    n_tiles = n // tm

    def rows_while(active):
        return lambda j, i: jnp.where(j < active, 0, jnp.where(j == active, i, n_tiles - 1))

    def flat(active):
        tile = rows_while(active)
        return pl.BlockSpec((tm, half), lambda j, i: (tile(j, i), 0))

    def by_head(active):
        tile = rows_while(active)
        return pl.BlockSpec((tm, heads, HEAD_DIM), lambda j, i: (tile(j, i), 0, 0))

    gate_tile = lambda j, i: (jnp.where(j < 2, 0, i), jnp.maximum(j - 2, 0))
    out_shape = (
        jax.ShapeDtypeStruct((n, half), F32),
        jax.ShapeDtypeStruct((n, half), BF16),
        jax.ShapeDtypeStruct((n, heads, HEAD_DIM), F32),
        jax.ShapeDtypeStruct((n, half), BF16),
        jax.ShapeDtypeStruct((n, heads, HEAD_DIM), F32),
        jax.ShapeDtypeStruct((n, half), BF16),
        jax.ShapeDtypeStruct((n, 2 * d), BF16),
    )
    out_specs = (flat(0), flat(0), by_head(1), flat(1), by_head(1), flat(1), pl.BlockSpec((tm, d), gate_tile))
    return pl.pallas_call(
        functools.partial(_in_proj_kernel, q_scale=1.0 / np.sqrt(HEAD_DIM), chunk=min(ROW_CHUNK, tm)),
        grid=(n_col, n_tiles),
        in_specs=[
            pl.BlockSpec((tm, d), lambda j, i: (i, 0)),
            pl.BlockSpec((1, d), lambda j, i: (0, 0)),
            pl.BlockSpec((d, d), lambda j, i: (0, j)),
            pl.BlockSpec((1, d), lambda j, i: (0, jnp.maximum(j - 2, 0))),
        ],
        out_specs=out_specs,
        out_shape=out_shape,
        compiler_params=_params(("arbitrary", "arbitrary")),
        name="in_proj",
    )(x, g, w_in, b_gate)


def _norm_matmul_kernel(x_ref, g_ref, w_ref, o_ref):
    xn = _rmsnorm(x_ref[...], g_ref[...]).astype(BF16)
    o_ref[...] = jnp.dot(xn, w_ref[...], preferred_element_type=F32)


def _norm_matmul(x, g, w):
    n, d = x.shape
    tm = min(256, n)
    return pl.pallas_call(
        _norm_matmul_kernel,
        grid=(n // tm,),
        in_specs=[pl.BlockSpec((tm, d), lambda i: (i, 0)), _const_spec((1, d)), _const_spec(w.shape)],
        out_specs=pl.BlockSpec((tm, w.shape[1]), lambda i: (i, 0)),
        out_shape=jax.ShapeDtypeStruct((n, w.shape[1]), F32),
        compiler_params=_params(("arbitrary",)),
        name="mem_kv",
    )(x, g, w)


def _sb_kernel(*refs, blk, n_sub, n_heads, key_block_offset, has_past):
    if has_past:
        q_ref, k_ref, v_ref, past_k_ref, past_v_ref, tri1_ref, tri2_ref, o_ref = refs
    else:
        q_ref, k_ref, v_ref, tri1_ref, tri2_ref, o_ref = refs
    step = pl.program_id(2)
    first_span = 2
    wide = first_span * blk
    col_minus_row = (lax.broadcasted_iota(jnp.int32, (blk, wide), 1)
                     - lax.broadcasted_iota(jnp.int32, (blk, wide), 0))
    chains = [(sub, h) for sub in range(n_sub) for h in range(n_heads)]
    rows = lambda sub: slice(sub * blk, (sub + 1) * blk)
    lanes = lambda h: slice(h * HEAD_DIM, (h + 1) * HEAD_DIM)
    qs = [q_ref[0, rows(sub), lanes(h)] for sub, h in chains]
    first_diag = step * n_sub + key_block_offset

    def visit(newest, span, accs, runs, first):
        width = span * blk
        tri = (tri1_ref if span == 1 else tri2_ref)[...]
        log_nots, log_betas, vs, masks = [], [], [], []
        runs = list(runs)
        for c, (sub, h) in enumerate(chains):
            kb_new = first_diag + sub - newest
            kb_old = kb_new - (span - 1)
            start_blk = jnp.maximum(kb_old, 0)
            if first:
                masks.append(col_minus_row[:, :width] < (kb_new - start_blk) * blk)
            else:
                runs[c] = jnp.where(kb_old >= 0, runs[c], -1e30)
            start = pl.multiple_of(start_blk * blk, blk)
            if not has_past:
                k = k_ref[0, pl.ds(start, width), lanes(h)]
                v = v_ref[0, pl.ds(start, width), lanes(h)]
            elif first:
                newest_past = slice((key_block_offset - 1) * blk, key_block_offset * blk)
                k = jnp.concatenate([past_k_ref[0, newest_past, h, :].astype(BF16), k_ref[0, :, lanes(h)]], axis=0)
                v = jnp.concatenate([past_v_ref[0, newest_past, h, :].astype(BF16), v_ref[0, :, lanes(h)]], axis=0)
            else:
                k = past_k_ref[0, pl.ds(start, width), h, :].astype(BF16)
                v = past_v_ref[0, pl.ds(start, width), h, :].astype(BF16)
            vs.append(v)
            z = lax.dot_general(qs[c], k, (((1,), (1,)), ((), ())), preferred_element_type=F32)
            t = jnp.log(1.0 + jnp.exp(-jnp.abs(z)))
            log_not = -(jnp.maximum(z, 0.0) + t)
            if first:
                log_not = jnp.where(masks[c], log_not, 0.0)
            log_nots.append(log_not)
            log_betas.append(jnp.minimum(z, 0.0) - t)
        stacked = jnp.concatenate(log_nots, axis=0)
        hi = stacked.astype(BF16)
        lo = (stacked - hi.astype(F32)).astype(BF16)
        sums = (jnp.dot(hi, tri, preferred_element_type=F32)
                + jnp.dot(lo, tri, preferred_element_type=F32))
        new_accs, new_runs = [], []
        for c in range(len(chains)):
            part = sums[c * blk:(c + 1) * blk]
            between = part[:, :width]
            total = part[:, width:]
            if first:
                a = jnp.where(masks[c], jnp.exp(log_betas[c] + between), 0.0)
            else:
                a = jnp.exp(log_betas[c] + between + runs[c][:, :width])
            new_accs.append(accs[c] + jnp.dot(a.astype(BF16), vs[c], preferred_element_type=F32))
            new_runs.append(runs[c] + total)
        return tuple(new_accs), tuple(new_runs)

    zeros = tuple(jnp.zeros((blk, LANES), F32) for _ in chains)
    accs, runs = visit(0, first_span, zeros, zeros, True)

    def alive(carry):
        newest, _, runs = carry
        live = functools.reduce(jnp.maximum, runs)
        return jnp.logical_and(first_diag + n_sub - 1 - newest >= 0, jnp.max(live) > SB_DEAD_LOG)

    def older(carry):
        newest, accs, runs = carry
        accs, runs = visit(newest, 1, accs, runs, False)
        return newest + 1, accs, runs

    _, accs, _ = lax.while_loop(alive, older, (jnp.int32(first_span), accs, runs))
    for c, (sub, h) in enumerate(chains):
        o_ref[0, rows(sub), lanes(h)] = accs[c].astype(BF16)


def _suffix_sum_matrix(width):
    jj = np.arange(width)
    return jnp.asarray(np.concatenate([jj[:, None] > jj[None, :], np.ones((width, LANES), bool)], axis=1), BF16)


def _sb_attention(q, k, v, past_k=None, past_v=None):
    b, lq, width = q.shape
    has_past = past_k is not None
    past = past_k.shape[1] if has_past else 0
    lk = past + lq
    blk = min(128, lq)
    assert lq % blk == 0 and past % blk == 0 and lk >= 2 * blk
    assert not has_past or lq == blk, "with a cache the new keys must form one block"
    n_q = lq // blk
    n_chains = 8
    n_sub = min(n_chains, n_q)
    n_heads = min(SB_HEADS, n_chains // n_sub)
    assert n_q % n_sub == 0 and SB_HEADS % n_heads == 0
    tri1, tri2 = _suffix_sum_matrix(blk), _suffix_sum_matrix(2 * blk)
    lane_w = n_heads * HEAD_DIM
    keys = pl.BlockSpec((1, lq, lane_w), lambda bi, h, i: (bi, 0, h))
    cache = [pl.BlockSpec((1, past, n_heads, HEAD_DIM), lambda bi, h, i: (bi, 0, h, 0))] * 2 if has_past else []
    return pl.pallas_call(
        functools.partial(_sb_kernel, blk=blk, n_sub=n_sub, n_heads=n_heads,
                          key_block_offset=past // blk, has_past=has_past),
        grid=(b, SB_HEADS // n_heads, n_q // n_sub),
        in_specs=[
            pl.BlockSpec((1, n_sub * blk, lane_w), lambda bi, h, i: (bi, i, h)),
            keys, keys, *cache,
            _const_spec(tri1.shape), _const_spec(tri2.shape),
        ],
        out_specs=pl.BlockSpec((1, n_sub * blk, lane_w), lambda bi, h, i: (bi, i, h)),
        out_shape=jax.ShapeDtypeStruct(q.shape, BF16),
        compiler_params=_params(("arbitrary", "arbitrary", "arbitrary")),
        name="sb_attn",
    )(q, k, v, *((past_k, past_v) if has_past else ()), tri1, tri2)


def _mix_kernel(x_ref, u_ref, ysb_ref, gp_ref, gs_ref, hist_ref, mk_ref, mv_ref,
                pool_w_ref, pool_scale_ref, wbp_ref, wbs_ref, wout_ref, gxa_ref, wq_ref, wo_ref,
                o_ref, ubuf_ref, *, tm, seg, chunk, tiles_per_seq, pos0):
    i = pl.program_id(0)
    group = u_ref.shape[1] // len(POOL_WINDOWS)
    piece = min(chunk, seg)
    u_row = lambda r: r + POOL_HIST * (r // seg + 1)

    if tiles_per_seq > 1:
        tile_in_seq = i % tiles_per_seq

        @pl.when(tile_in_seq == 0)
        def _():
            ubuf_ref[0:POOL_HIST, :] = hist_ref[0]

        seq_row0 = tile_in_seq * tm
    else:
        for s in range(tm // seg):
            top = u_row(s * seg)
            ubuf_ref[top - POOL_HIST:top, :] = hist_ref[s]
        seq_row0 = 0

    for r0 in range(0, tm, chunk):
        rows = slice(r0, r0 + chunk)
        pieces = range(r0, r0 + chunk, piece)
        for r in pieces:
            ubuf_ref[u_row(r):u_row(r) + piece, :] = u_ref[r:r + piece, :]
        pooled = []
        for g, w in enumerate(POOL_WINDOWS):
            cols = slice(g * group, (g + 1) * group)
            deltas = []
            for r in pieces:
                top = u_row(r)
                u = u_ref[r:r + piece, cols]
                window_sum = u
                for back in range(1, w):
                    window_sum = window_sum + ubuf_ref[top - back:top - back + piece, cols]
                pos = pos0 + seq_row0 + r % seg + lax.broadcasted_iota(jnp.int32, (piece, 1), 0)
                count = jnp.minimum(w, pos + 1).astype(F32)
                deltas.append((window_sum / count - u).astype(BF16))
            delta = deltas[0] if len(deltas) == 1 else jnp.concatenate(deltas, axis=0)
            pooled.append(jnp.dot(delta, pool_w_ref[g], preferred_element_type=F32))
        y_pool = (jnp.concatenate(pooled, axis=-1) * pool_scale_ref[...]).astype(BF16)

        branch_pool = jnp.dot(y_pool, wbp_ref[...], preferred_element_type=F32)
        branch_sb = jnp.dot(ysb_ref[rows, :], wbs_ref[...], preferred_element_type=F32)
        merged = gp_ref[rows, :].astype(F32) * branch_pool + gs_ref[rows, :].astype(F32) * branch_sb
        x1 = x_ref[rows, :] + jnp.dot(merged.astype(BF16), wout_ref[...], preferred_element_type=F32)

        xn = _rmsnorm(x1, gxa_ref[...]).astype(BF16)
        q = (jnp.dot(xn, wq_ref[...], preferred_element_type=F32) * (1.0 / np.sqrt(HEAD_DIM))).astype(BF16)
        attended = []
        for r in pieces:
            s = r // seg
            heads = []
            for h in range(XA_HEADS):
                cols = slice(h * HEAD_DIM, (h + 1) * HEAD_DIM)
                sc = lax.dot_general(q[r - r0:r - r0 + piece, cols], mk_ref[s, :, cols], (((1,), (1,)), ((), ())),
                                     preferred_element_type=F32)
                p = jnp.exp(sc - jnp.max(sc, axis=-1, keepdims=True))
                p = p / jnp.sum(p, axis=-1, keepdims=True)
                heads.append(jnp.dot(p.astype(BF16), mv_ref[s, :, cols], preferred_element_type=F32))
            attended.append(jnp.concatenate(heads, axis=-1).astype(BF16))
        attn = attended[0] if len(attended) == 1 else jnp.concatenate(attended, axis=0)
        o_ref[rows, :] = x1 + jnp.dot(attn, wo_ref[...], preferred_element_type=F32)

    if tiles_per_seq > 1:
        ubuf_ref[0:POOL_HIST, :] = ubuf_ref[tm:tm + POOL_HIST, :]


def _mix(x, seq_len, u, ysb, gates, hist, mk, mv, pool_w, pool_scale, wbp, wbs, wout, gxa, wq, wo, pos0):
    n, d = x.shape
    tm = min(256, n)
    seg = min(seq_len, tm)
    n_seg = tm // seg
    tiles_per_seq = seq_len // seg
    assert n % tm == 0 and tm % seg == 0 and seq_len % seg == 0
    tile = lambda i: (i, 0)
    per_seq = lambda i: (i // tiles_per_seq, 0, 0)
    return pl.pallas_call(
        functools.partial(_mix_kernel, tm=tm, seg=seg, chunk=tm, tiles_per_seq=tiles_per_seq,
                          pos0=pos0),
        grid=(n // tm,),
        in_specs=[
            pl.BlockSpec((tm, d), tile),
            pl.BlockSpec((tm, u.shape[1]), tile),
            pl.BlockSpec((tm, ysb.shape[1]), tile),
            pl.BlockSpec((tm, d), tile),
            pl.BlockSpec((tm, d), lambda i: (i, 1)),
            pl.BlockSpec((n_seg,) + hist.shape[1:], per_seq),
            pl.BlockSpec((n_seg,) + mk.shape[1:], per_seq),
            pl.BlockSpec((n_seg,) + mv.shape[1:], per_seq),
            _const_spec(pool_w.shape), _const_spec(pool_scale.shape), _const_spec(wbp.shape),
            _const_spec(wbs.shape), _const_spec(wout.shape), _const_spec(gxa.shape),
            _const_spec(wq.shape), _const_spec(wo.shape),
        ],
        out_specs=pl.BlockSpec((tm, d), tile),
        out_shape=jax.ShapeDtypeStruct(x.shape, F32),
        scratch_shapes=[pltpu.VMEM((tm + POOL_HIST * n_seg, u.shape[1]), F32)],
        compiler_params=_params(("arbitrary",)),
        name="mix",
    )(x, u, ysb, gates, gates, hist, mk, mv, pool_w, pool_scale, wbp, wbs, wout, gxa, wq, wo)


def _ffn_kernel(x_ref, g_ref, wg_ref, wv_ref, cwg_ref, cwv_ref, cbg_ref, cbv_ref, wd_ref, sg_ref, sv_ref,
                gf_ref, y_ref, ng_ref, nv_ref, xn_ref, hg_ref, hv_ref, carry_ref,
                *, tm, seg, chunk, tiles_per_seq):
    i = pl.program_id(0)
    j = pl.program_id(1)
    piece = min(chunk, seg)
    lags = CONV_W - 1
    halves = ((wg_ref, cwg_ref, cbg_ref, sg_ref, ng_ref, hg_ref),
              (wv_ref, cwv_ref, cbv_ref, sv_ref, nv_ref, hv_ref))
    h_row = lambda r: r + CONV_HIST * (r // seg + 1)

    @pl.when(j == 0)
    def _():
        xn_ref[...] = _rmsnorm(x_ref[...], g_ref[...]).astype(BF16)
        y_ref[...] = jnp.zeros_like(y_ref)

    for half, (_, _, _, state_ref, _, h_ref) in enumerate(halves):
        if tiles_per_seq > 1:
            first_of_seq = (i % tiles_per_seq) == 0

            @pl.when(first_of_seq)
            def _(state_ref=state_ref, h_ref=h_ref):
                h_ref[CONV_HIST - lags:CONV_HIST, :] = state_ref[0]

            @pl.when(jnp.logical_not(first_of_seq))
            def _(h_ref=h_ref, half=half):
                h_ref[0:CONV_HIST, :] = carry_ref[half, j]
        else:
            for s in range(tm // seg):
                top = h_row(s * seg)
                h_ref[top - lags:top, :] = state_ref[s]

    def up_conv(r0, half):
        w_ref, cw_ref, cb_ref, _, new_ref, h_ref = halves[half]
        h = jnp.dot(xn_ref[r0:r0 + chunk, :], w_ref[...], preferred_element_type=F32)
        outs = []
        for p0 in range(0, chunk, piece):
            top = h_row(r0 + p0)
            hp = h[p0:p0 + piece]
            h_ref[top:top + piece, :] = hp
            out = cw_ref[lags:lags + 1, :] * hp
            for tap in range(lags):
                lag = lags - tap
                out = out + cw_ref[tap:tap + 1, :] * h_ref[top - lag:top - lag + piece, :]
            outs.append(out + cb_ref[...])
            seg_end = r0 + p0 + piece
            if seg_end % seg == 0:
                new_ref[seg_end // seg - 1] = hp[piece - lags:, :]
        return outs[0] if len(outs) == 1 else jnp.concatenate(outs, axis=0)

    def down(r0, act):
        y_ref[r0:r0 + chunk, :] += jnp.dot(act, wd_ref[...], preferred_element_type=F32)

    acts = []
    for r0 in range(0, tm, chunk):
        gate, val = up_conv(r0, 0), up_conv(r0, 1)
        acts.append((r0, (gate * jax.nn.sigmoid(gate) * val).astype(BF16)))
    for r0, act in acts:
        down(r0, act)

    if tiles_per_seq > 1:
        for half, (_, _, _, _, _, h_ref) in enumerate(halves):
            carry_ref[half, j] = h_ref[tm:tm + CONV_HIST, :]

    @pl.when(j == pl.num_programs(1) - 1)
    def _():
        y_ref[...] = _rmsnorm(x_ref[...] + y_ref[...], gf_ref[...])


def _conv_ffn(x, seq_len, g, w_up, conv_w, conv_b, w_down, state, g_final):
    n, d = x.shape
    f = w_down.shape[0]
    tm = min(512, n)
    tf = FFN_TILE
    seg = min(seq_len, tm)
    n_seg = tm // seg
    tiles_per_seq = seq_len // seg
    assert f % tf == 0 and n % tm == 0 and tm % seg == 0 and seq_len % seg == 0
    n_f = f // tf
    h_rows = tm + CONV_HIST * n_seg
    lo = lambda i, j: (0, j)
    hi = lambda i, j: (0, n_f + j)
    st_lo = lambda i, j: (i // tiles_per_seq, 0, j)
    st_hi = lambda i, j: (i // tiles_per_seq, 0, n_f + j)
    y, tail_g, tail_v = pl.pallas_call(
        functools.partial(_ffn_kernel, tm=tm, seg=seg, chunk=min(FFN_ROW_CHUNK, tm), tiles_per_seq=tiles_per_seq),
        grid=(n // tm, n_f),
        in_specs=[
            pl.BlockSpec((tm, d), lambda i, j: (i, 0)),
            pl.BlockSpec((1, d), lambda i, j: (0, 0)),
            pl.BlockSpec((d, tf), lo), pl.BlockSpec((d, tf), hi),
            pl.BlockSpec((CONV_W, tf), lo), pl.BlockSpec((CONV_W, tf), hi),
            pl.BlockSpec((1, tf), lo), pl.BlockSpec((1, tf), hi),
            pl.BlockSpec((tf, d), lambda i, j: (j, 0)),
            pl.BlockSpec((n_seg, CONV_W - 1, tf), st_lo), pl.BlockSpec((n_seg, CONV_W - 1, tf), st_hi),
            pl.BlockSpec((1, d), lambda i, j: (0, 0)),
        ],
        out_specs=(
            pl.BlockSpec((tm, d), lambda i, j: (i, 0)),
            pl.BlockSpec((n_seg, CONV_W - 1, tf), lambda i, j: (i, 0, j)),
            pl.BlockSpec((n_seg, CONV_W - 1, tf), lambda i, j: (i, 0, j)),
        ),
        out_shape=(
            jax.ShapeDtypeStruct((n, d), F32),
            jax.ShapeDtypeStruct((n // seg, CONV_W - 1, f), F32),
            jax.ShapeDtypeStruct((n // seg, CONV_W - 1, f), F32),
        ),
        scratch_shapes=[
            pltpu.VMEM((tm, d), BF16),
            pltpu.VMEM((h_rows, tf), F32),
            pltpu.VMEM((h_rows, tf), F32),
            pltpu.VMEM((2, n_f, CONV_HIST, tf), F32),
        ],
        compiler_params=_params(("arbitrary", "arbitrary")),
        name="conv_ffn",
    )(x, g, w_up, w_up, conv_w, conv_w, conv_b, conv_b, w_down, state, state, g_final)
    last_tile = slice(tiles_per_seq - 1, None, tiles_per_seq)
    return y, tail_g[last_tile], tail_v[last_tile]


def _layer(x, pool_state, past_k, past_v, mk, mv, conv_state, p):
    b, l, d = x.shape
    n = b * l
    assert l >= POOL_HIST and l >= CONV_W - 1
    x = x.reshape(n, d)
    u, q, kf, kb, vf, vb, gates = _in_proj(x, p["norm_mix_g"], p["w_in"], p["b_gate"])
    width = u.shape[1]
    if past_k is None:
        past = 0
        hist = jnp.zeros((b, POOL_HIST, width), F32)
    else:
        past = past_k.shape[1]
        hist = jnp.pad(pool_state, ((0, 0), (POOL_HIST - pool_state.shape[1], 0), (0, 0)))
    by_seq = lambda a: a.reshape(b, l, width)
    y_sb = _sb_attention(by_seq(q), by_seq(kb), by_seq(vb), past_k, past_v)

    x2 = _mix(x, l, u, y_sb.reshape(n, width), gates, hist, mk, mv,
              p["pool_w"], p["pool_scale"], p["w_branch_pool"], p["w_branch_sb"], p["w_out"],
              p["norm_xa_g"], p["xa_wq"], p["xa_wo"], past)

    y, new_g, new_v = _conv_ffn(x2, l, p["norm_ffn_g"], p["ffn_w_up"], p["ffn_conv_w"],
                                p["ffn_conv_b"], p["ffn_w_down"], conv_state, p["norm_final_g"])
    new_pool = by_seq(u)[:, l - (POOL_HIST - 1):, :]
    new_conv = jnp.concatenate([new_g, new_v], axis=-1)
    return (y.reshape(b, l, d), new_pool, kf.reshape(b, l, SB_HEADS, HEAD_DIM),
            vf.reshape(b, l, SB_HEADS, HEAD_DIM), new_conv)


def kernel(x_prompt, x_sample, mem_prompt, state_pool, cache_sb_k, cache_sb_v, cache_mem_k, cache_mem_v, state_ffn_conv, norm_mix_g, w_in, b_gate, pool_w, pool_scale, w_branch_pool, w_branch_sb, w_out, norm_xa_g, norm_mem_g, xa_wq, xa_wkv, xa_wo, norm_ffn_g, ffn_w_up, ffn_conv_w, ffn_conv_b, ffn_w_down, norm_final_g):
    assert norm_mix_g.shape[0] == 1, "single-layer step"
    bp, lp, d = x_prompt.shape
    n_mem = mem_prompt.shape[1]
    xa_width = xa_wq.shape[2]
    row = lambda a: a.reshape(1, -1)
    p = {
        "norm_mix_g": row(norm_mix_g[0]), "w_in": w_in[0].astype(BF16), "b_gate": row(b_gate[0]),
        "pool_w": pool_w[0].astype(BF16), "pool_scale": row(pool_scale[0]),
        "w_branch_pool": w_branch_pool[0].astype(BF16), "w_branch_sb": w_branch_sb[0].astype(BF16),
        "w_out": w_out[0].astype(BF16), "norm_xa_g": row(norm_xa_g[0]),
        "xa_wq": xa_wq[0].astype(BF16), "xa_wo": xa_wo[0].astype(BF16),
        "norm_ffn_g": row(norm_ffn_g[0]), "ffn_w_up": ffn_w_up[0].astype(BF16),
        "ffn_conv_w": ffn_conv_w[0], "ffn_conv_b": row(ffn_conv_b[0]),
        "ffn_w_down": ffn_w_down[0].astype(BF16), "norm_final_g": row(norm_final_g),
    }

    mem_kv = _norm_matmul(mem_prompt.reshape(bp * n_mem, d), row(norm_mem_g[0]), xa_wkv[0].astype(BF16))
    mem_kv = mem_kv.reshape(bp, n_mem, 2 * xa_width)
    mk_p, mv_p = mem_kv[..., :xa_width], mem_kv[..., xa_width:]

    conv0 = jnp.zeros((bp, CONV_W - 1, ffn_w_up.shape[2]), F32)
    y_p, pool_p, k_p, v_p, conv_p = _layer(x_prompt, None, None, None, mk_p.astype(BF16), mv_p.astype(BF16),
                                           conv0, p)

    bs = x_sample.shape[0]
    mk_s = cache_mem_k[0].reshape(bs, n_mem, xa_width).astype(BF16)
    mv_s = cache_mem_v[0].reshape(bs, n_mem, xa_width).astype(BF16)
    y_s, pool_s, k_s, v_s, conv_s = _layer(x_sample, state_pool[0], cache_sb_k[0], cache_sb_v[0], mk_s, mv_s,
                                           state_ffn_conv[0], p)

    mem_shape = (1, bp, n_mem, XA_HEADS, HEAD_DIM)
    return (y_p, y_s,
            pool_p[None], k_p[None], v_p[None], mk_p.reshape(mem_shape), mv_p.reshape(mem_shape), conv_p[None],
            pool_s[None], k_s[None], v_s[None], conv_s[None])
```

```python
import functools

import jax
import jax.numpy as jnp
import numpy as np
from jax import lax
from jax.experimental import pallas as pl
from jax.experimental.pallas import tpu as pltpu

F32 = jnp.float32
BF16 = jnp.bfloat16

EPS = 1e-6
POOL_WINDOWS = (2, 4, 8, 16)
POOL_HIST = 16
SB_HEADS = 8
HEAD_DIM = 128
XA_HEADS = 4
CONV_W = 3
CONV_HIST = 8
LANES = 128
FFN_TILE = 512
ROW_CHUNK = 128
FFN_ROW_CHUNK = 256

SB_DEAD_LOG = -88.0

VMEM_LIMIT = 56 * 1024 * 1024


def _rmsnorm(xf, g):
    ms = jnp.mean(xf * xf, axis=-1, keepdims=True)
    return xf * lax.rsqrt(ms + EPS) * g


def _const_spec(shape):
    zeros = (0,) * len(shape)
    return pl.BlockSpec(shape, lambda *_: zeros, pipeline_mode=pl.Buffered(1))


def _params(semantics):
    return pltpu.CompilerParams(dimension_semantics=semantics, vmem_limit_bytes=VMEM_LIMIT)


def _in_proj_kernel(x_ref, g_ref, w_ref, b_ref, u_ref, q_ref, kf_ref, kb_ref, vf_ref, vb_ref, gate_ref,
                    *, q_scale, chunk):
    j = pl.program_id(0)
    tm = x_ref.shape[0]
    half = u_ref.shape[1]

    def project(epilogue):
        for r0 in range(0, tm, chunk):
            rows = slice(r0, r0 + chunk)
            xn = _rmsnorm(x_ref[rows, :], g_ref[...]).astype(BF16)
            epilogue(rows, jnp.dot(xn, w_ref[...], preferred_element_type=F32))

    def put_uq(rows, acc):
        u_ref[rows, :] = acc[:, :half]
        q_ref[rows, :] = (acc[:, half:] * q_scale).astype(BF16)

    def put_heads(f32_ref, bf16_ref, rows, acc):
        f32_ref[rows, :, :] = acc.reshape(acc.shape[0], f32_ref.shape[1], f32_ref.shape[2])
        bf16_ref[rows, :] = acc.astype(BF16)

    def put_kv(rows, acc):
        put_heads(kf_ref, kb_ref, rows, acc[:, :half])
        put_heads(vf_ref, vb_ref, rows, acc[:, half:])

    def put_gate(rows, acc):
        gate_ref[rows, :] = jax.nn.sigmoid(acc + b_ref[...]).astype(BF16)

    pl.when(j == 0)(functools.partial(project, put_uq))
    pl.when(j == 1)(functools.partial(project, put_kv))
    pl.when(j >= 2)(functools.partial(project, put_gate))


def _in_proj(x, g, w_in, b_gate):
    n, d = x.shape
    half = d // 2
    n_col = w_in.shape[1] // d
    heads = half // HEAD_DIM
    tm = min(512, n)
    n_tiles = n // tm

    def rows_while(active):
        return lambda j, i: jnp.where(j < active, 0, jnp.where(j == active, i, n_tiles - 1))

    def flat(active):
        tile = rows_while(active)
        return pl.BlockSpec((tm, half), lambda j, i: (tile(j, i), 0))

    def by_head(active):
        tile = rows_while(active)
        return pl.BlockSpec((tm, heads, HEAD_DIM), lambda j, i: (tile(j, i), 0, 0))

    gate_tile = lambda j, i: (jnp.where(j < 2, 0, i), jnp.maximum(j - 2, 0))
    out_shape = (
        jax.ShapeDtypeStruct((n, half), F32),
        jax.ShapeDtypeStruct((n, half), BF16),
        jax.ShapeDtypeStruct((n, heads, HEAD_DIM), F32),
        jax.ShapeDtypeStruct((n, half), BF16),
        jax.ShapeDtypeStruct((n, heads, HEAD_DIM), F32),
        jax.ShapeDtypeStruct((n, half), BF16),
        jax.ShapeDtypeStruct((n, 2 * d), BF16),
    )
    out_specs = (flat(0), flat(0), by_head(1), flat(1), by_head(1), flat(1), pl.BlockSpec((tm, d), gate_tile))
    return pl.pallas_call(
        functools.partial(_in_proj_kernel, q_scale=1.0 / np.sqrt(HEAD_DIM), chunk=min(ROW_CHUNK, tm)),
        grid=(n_col, n_tiles),
        in_specs=[
            pl.BlockSpec((tm, d), lambda j, i: (i, 0)),
            pl.BlockSpec((1, d), lambda j, i: (0, 0)),
            pl.BlockSpec((d, d), lambda j, i: (0, j)),
            pl.BlockSpec((1, d), lambda j, i: (0, jnp.maximum(j - 2, 0))),
        ],
        out_specs=out_specs,
        out_shape=out_shape,
        compiler_params=_params(("arbitrary", "arbitrary")),
        name="in_proj",
    )(x, g, w_in, b_gate)


def _norm_matmul_kernel(x_ref, g_ref, w_ref, o_ref):
    xn = _rmsnorm(x_ref[...], g_ref[...]).astype(BF16)
    o_ref[...] = jnp.dot(xn, w_ref[...], preferred_element_type=F32)


def _norm_matmul(x, g, w):
    n, d = x.shape
    tm = min(256, n)
    return pl.pallas_call(
        _norm_matmul_kernel,
        grid=(n // tm,),
        in_specs=[pl.BlockSpec((tm, d), lambda i: (i, 0)), _const_spec((1, d)), _const_spec(w.shape)],
        out_specs=pl.BlockSpec((tm, w.shape[1]), lambda i: (i, 0)),
        out_shape=jax.ShapeDtypeStruct((n, w.shape[1]), F32),
        compiler_params=_params(("arbitrary",)),
        name="mem_kv",
    )(x, g, w)


def _sb_kernel(*refs, blk, n_sub, n_heads, key_block_offset, has_past):
    if has_past:
        q_ref, k_ref, v_ref, past_k_ref, past_v_ref, tri1_ref, tri2_ref, o_ref = refs
    else:
        q_ref, k_ref, v_ref, tri1_ref, tri2_ref, o_ref = refs
    step = pl.program_id(2)
    first_span = 2
    wide = first_span * blk
    col_minus_row = (lax.broadcasted_iota(jnp.int32, (blk, wide), 1)
                     - lax.broadcasted_iota(jnp.int32, (blk, wide), 0))
    chains = [(sub, h) for sub in range(n_sub) for h in range(n_heads)]
    rows = lambda sub: slice(sub * blk, (sub + 1) * blk)
    lanes = lambda h: slice(h * HEAD_DIM, (h + 1) * HEAD_DIM)
    qs = [q_ref[0, rows(sub), lanes(h)] for sub, h in chains]
    first_diag = step * n_sub + key_block_offset

    def visit(newest, span, accs, runs, first):
        width = span * blk
        tri = (tri1_ref if span == 1 else tri2_ref)[...]
        log_nots, log_betas, vs, masks = [], [], [], []
        runs = list(runs)
        for c, (sub, h) in enumerate(chains):
            kb_new = first_diag + sub - newest
            kb_old = kb_new - (span - 1)
            start_blk = jnp.maximum(kb_old, 0)
            if first:
                masks.append(col_minus_row[:, :width] < (kb_new - start_blk) * blk)
            else:
                runs[c] = jnp.where(kb_old >= 0, runs[c], -1e30)
            start = pl.multiple_of(start_blk * blk, blk)
            if not has_past:
                k = k_ref[0, pl.ds(start, width), lanes(h)]
                v = v_ref[0, pl.ds(start, width), lanes(h)]
            elif first:
                newest_past = slice((key_block_offset - 1) * blk, key_block_offset * blk)
                k = jnp.concatenate([past_k_ref[0, newest_past, h, :].astype(BF16), k_ref[0, :, lanes(h)]], axis=0)
                v = jnp.concatenate([past_v_ref[0, newest_past, h, :].astype(BF16), v_ref[0, :, lanes(h)]], axis=0)
            else:
                k = past_k_ref[0, pl.ds(start, width), h, :].astype(BF16)
                v = past_v_ref[0, pl.ds(start, width), h, :].astype(BF16)
            vs.append(v)
            z = lax.dot_general(qs[c], k, (((1,), (1,)), ((), ())), preferred_element_type=F32)
            t = jnp.log(1.0 + jnp.exp(-jnp.abs(z)))
            log_not = -(jnp.maximum(z, 0.0) + t)
            if first:
                log_not = jnp.where(masks[c], log_not, 0.0)
            log_nots.append(log_not)
            log_betas.append(jnp.minimum(z, 0.0) - t)
        stacked = jnp.concatenate(log_nots, axis=0)
        hi = stacked.astype(BF16)
        lo = (stacked - hi.astype(F32)).astype(BF16)
        sums = (jnp.dot(hi, tri, preferred_element_type=F32)
                + jnp.dot(lo, tri, preferred_element_type=F32))
        new_accs, new_runs = [], []
        for c in range(len(chains)):
            part = sums[c * blk:(c + 1) * blk]
            between = part[:, :width]
            total = part[:, width:]
            if first:
                a = jnp.where(masks[c], jnp.exp(log_betas[c] + between), 0.0)
            else:
                a = jnp.exp(log_betas[c] + between + runs[c][:, :width])
            new_accs.append(accs[c] + jnp.dot(a.astype(BF16), vs[c], preferred_element_type=F32))
            new_runs.append(runs[c] + total)
        return tuple(new_accs), tuple(new_runs)

    zeros = tuple(jnp.zeros((blk, LANES), F32) for _ in chains)
    accs, runs = visit(0, first_span, zeros, zeros, True)

    def alive(carry):
        newest, _, runs = carry
        live = functools.reduce(jnp.maximum, runs)
        return jnp.logical_and(first_diag + n_sub - 1 - newest >= 0, jnp.max(live) > SB_DEAD_LOG)

    def older(carry):
        newest, accs, runs = carry
        accs, runs = visit(newest, 1, accs, runs, False)
        return newest + 1, accs, runs

    _, accs, _ = lax.while_loop(alive, older, (jnp.int32(first_span), accs, runs))
    for c, (sub, h) in enumerate(chains):
        o_ref[0, rows(sub), lanes(h)] = accs[c].astype(BF16)


def _suffix_sum_matrix(width):
    jj = np.arange(width)
    return jnp.asarray(np.concatenate([jj[:, None] > jj[None, :], np.ones((width, LANES), bool)], axis=1), BF16)


def _sb_attention(q, k, v, past_k=None, past_v=None):
    b, lq, width = q.shape
    has_past = past_k is not None
    past = past_k.shape[1] if has_past else 0
    lk = past + lq
    blk = min(128, lq)
    assert lq % blk == 0 and past % blk == 0 and lk >= 2 * blk
    assert not has_past or lq == blk, "with a cache the new keys must form one block"
    n_q = lq // blk
    n_chains = 8
    n_sub = min(n_chains, n_q)
    n_heads = min(SB_HEADS, n_chains // n_sub)
    assert n_q % n_sub == 0 and SB_HEADS % n_heads == 0
    tri1, tri2 = _suffix_sum_matrix(blk), _suffix_sum_matrix(2 * blk)
    lane_w = n_heads * HEAD_DIM
    keys = pl.BlockSpec((1, lq, lane_w), lambda bi, h, i: (bi, 0, h))
    cache = [pl.BlockSpec((1, past, n_heads, HEAD_DIM), lambda bi, h, i: (bi, 0, h, 0))] * 2 if has_past else []
    return pl.pallas_call(
        functools.partial(_sb_kernel, blk=blk, n_sub=n_sub, n_heads=n_heads,
                          key_block_offset=past // blk, has_past=has_past),
        grid=(b, SB_HEADS // n_heads, n_q // n_sub),
        in_specs=[
            pl.BlockSpec((1, n_sub * blk, lane_w), lambda bi, h, i: (bi, i, h)),
            keys, keys, *cache,
            _const_spec(tri1.shape), _const_spec(tri2.shape),
        ],
        out_specs=pl.BlockSpec((1, n_sub * blk, lane_w), lambda bi, h, i: (bi, i, h)),
        out_shape=jax.ShapeDtypeStruct(q.shape, BF16),
        compiler_params=_params(("arbitrary", "arbitrary", "arbitrary")),
        name="sb_attn",
    )(q, k, v, *((past_k, past_v) if has_past else ()), tri1, tri2)


def _mix_kernel(x_ref, u_ref, ysb_ref, gp_ref, gs_ref, hist_ref, mk_ref, mv_ref,
                pool_w_ref, pool_scale_ref, wbp_ref, wbs_ref, wout_ref, gxa_ref, wq_ref, wo_ref,
                o_ref, ubuf_ref, *, tm, seg, chunk, tiles_per_seq, pos0):
    i = pl.program_id(0)
    group = u_ref.shape[1] // len(POOL_WINDOWS)
    piece = min(chunk, seg)
    u_row = lambda r: r + POOL_HIST * (r // seg + 1)

    if tiles_per_seq > 1:
        tile_in_seq = i % tiles_per_seq

        @pl.when(tile_in_seq == 0)
        def _():
            ubuf_ref[0:POOL_HIST, :] = hist_ref[0]

        seq_row0 = tile_in_seq * tm
    else:
        for s in range(tm // seg):
            top = u_row(s * seg)
            ubuf_ref[top - POOL_HIST:top, :] = hist_ref[s]
        seq_row0 = 0

    for r0 in range(0, tm, chunk):
        rows = slice(r0, r0 + chunk)
        pieces = range(r0, r0 + chunk, piece)
        for r in pieces:
            ubuf_ref[u_row(r):u_row(r) + piece, :] = u_ref[r:r + piece, :]
        pooled = []
        for g, w in enumerate(POOL_WINDOWS):
            cols = slice(g * group, (g + 1) * group)
            deltas = []
            for r in pieces:
                top = u_row(r)
                u = u_ref[r:r + piece, cols]
                window_sum = u
                for back in range(1, w):
                    window_sum = window_sum + ubuf_ref[top - back:top - back + piece, cols]
                pos = pos0 + seq_row0 + r % seg + lax.broadcasted_iota(jnp.int32, (piece, 1), 0)
                count = jnp.minimum(w, pos + 1).astype(F32)
                deltas.append((window_sum / count - u).astype(BF16))
            delta = deltas[0] if len(deltas) == 1 else jnp.concatenate(deltas, axis=0)
            pooled.append(jnp.dot(delta, pool_w_ref[g], preferred_element_type=F32))
        y_pool = (jnp.concatenate(pooled, axis=-1) * pool_scale_ref[...]).astype(BF16)

        branch_pool = jnp.dot(y_pool, wbp_ref[...], preferred_element_type=F32)
        branch_sb = jnp.dot(ysb_ref[rows, :], wbs_ref[...], preferred_element_type=F32)
        merged = gp_ref[rows, :].astype(F32) * branch_pool + gs_ref[rows, :].astype(F32) * branch_sb
        x1 = x_ref[rows, :] + jnp.dot(merged.astype(BF16), wout_ref[...], preferred_element_type=F32)

        xn = _rmsnorm(x1, gxa_ref[...]).astype(BF16)
        q = (jnp.dot(xn, wq_ref[...], preferred_element_type=F32) * (1.0 / np.sqrt(HEAD_DIM))).astype(BF16)
        attended = []
        for r in pieces:
            s = r // seg
            heads = []
            for h in range(XA_HEADS):
                cols = slice(h * HEAD_DIM, (h + 1) * HEAD_DIM)
                sc = lax.dot_general(q[r - r0:r - r0 + piece, cols], mk_ref[s, :, cols], (((1,), (1,)), ((), ())),
                                     preferred_element_type=F32)
                p = jnp.exp(sc - jnp.max(sc, axis=-1, keepdims=True))
                p = p / jnp.sum(p, axis=-1, keepdims=True)
                heads.append(jnp.dot(p.astype(BF16), mv_ref[s, :, cols], preferred_element_type=F32))
            attended.append(jnp.concatenate(heads, axis=-1).astype(BF16))
        attn = attended[0] if len(attended) == 1 else jnp.concatenate(attended, axis=0)
        o_ref[rows, :] = x1 + jnp.dot(attn, wo_ref[...], preferred_element_type=F32)

    if tiles_per_seq > 1:
        ubuf_ref[0:POOL_HIST, :] = ubuf_ref[tm:tm + POOL_HIST, :]


def _mix(x, seq_len, u, ysb, gates, hist, mk, mv, pool_w, pool_scale, wbp, wbs, wout, gxa, wq, wo, pos0):
    n, d = x.shape
    tm = min(256, n)
    seg = min(seq_len, tm)
    n_seg = tm // seg
    tiles_per_seq = seq_len // seg
    assert n % tm == 0 and tm % seg == 0 and seq_len % seg == 0
    tile = lambda i: (i, 0)
    per_seq = lambda i: (i // tiles_per_seq, 0, 0)
    return pl.pallas_call(
        functools.partial(_mix_kernel, tm=tm, seg=seg, chunk=tm, tiles_per_seq=tiles_per_seq,
                          pos0=pos0),
        grid=(n // tm,),
        in_specs=[
            pl.BlockSpec((tm, d), tile),
            pl.BlockSpec((tm, u.shape[1]), tile),
            pl.BlockSpec((tm, ysb.shape[1]), tile),
            pl.BlockSpec((tm, d), tile),
            pl.BlockSpec((tm, d), lambda i: (i, 1)),
            pl.BlockSpec((n_seg,) + hist.shape[1:], per_seq),
            pl.BlockSpec((n_seg,) + mk.shape[1:], per_seq),
            pl.BlockSpec((n_seg,) + mv.shape[1:], per_seq),
            _const_spec(pool_w.shape), _const_spec(pool_scale.shape), _const_spec(wbp.shape),
            _const_spec(wbs.shape), _const_spec(wout.shape), _const_spec(gxa.shape),
            _const_spec(wq.shape), _const_spec(wo.shape),
        ],
        out_specs=pl.BlockSpec((tm, d), tile),
        out_shape=jax.ShapeDtypeStruct(x.shape, F32),
        scratch_shapes=[pltpu.VMEM((tm + POOL_HIST * n_seg, u.shape[1]), F32)],
        compiler_params=_params(("arbitrary",)),
        name="mix",
    )(x, u, ysb, gates, gates, hist, mk, mv, pool_w, pool_scale, wbp, wbs, wout, gxa, wq, wo)


def _ffn_kernel(x_ref, g_ref, wg_ref, wv_ref, cwg_ref, cwv_ref, cbg_ref, cbv_ref, wd_ref, sg_ref, sv_ref,
                gf_ref, y_ref, ng_ref, nv_ref, xn_ref, hg_ref, hv_ref, carry_ref,
                *, tm, seg, chunk, tiles_per_seq):
    i = pl.program_id(0)
    j = pl.program_id(1)
    piece = min(chunk, seg)
    lags = CONV_W - 1
    halves = ((wg_ref, cwg_ref, cbg_ref, sg_ref, ng_ref, hg_ref),
              (wv_ref, cwv_ref, cbv_ref, sv_ref, nv_ref, hv_ref))
    h_row = lambda r: r + CONV_HIST * (r // seg + 1)

    for half, (_, _, _, state_ref, _, h_ref) in enumerate(halves):
        if tiles_per_seq > 1:
            first_of_seq = (i % tiles_per_seq) == 0

            @pl.when(first_of_seq)
            def _(state_ref=state_ref, h_ref=h_ref):
                h_ref[CONV_HIST - lags:CONV_HIST, :] = state_ref[0]

            @pl.when(jnp.logical_not(first_of_seq))
            def _(h_ref=h_ref, half=half):
                h_ref[0:CONV_HIST, :] = carry_ref[half, j]
        else:
            for s in range(tm // seg):
                top = h_row(s * seg)
                h_ref[top - lags:top, :] = state_ref[s]

    def up_conv(r0, half, xn):
        w_ref, cw_ref, cb_ref, _, new_ref, h_ref = halves[half]
        h = jnp.dot(xn, w_ref[...], preferred_element_type=F32)
        outs = []
        for p0 in range(0, chunk, piece):
            top = h_row(r0 + p0)
            hp = h[p0:p0 + piece]
            h_ref[top:top + piece, :] = hp
            out = cw_ref[lags:lags + 1, :] * hp
            for tap in range(lags):
                lag = lags - tap
                out = out + cw_ref[tap:tap + 1, :] * h_ref[top - lag:top - lag + piece, :]
            outs.append(out + cb_ref[...])
            seg_end = r0 + p0 + piece
            if seg_end % seg == 0:
                new_ref[seg_end // seg - 1] = hp[piece - lags:, :]
        return outs[0] if len(outs) == 1 else jnp.concatenate(outs, axis=0)

    def step(first, last):
        acts = []
        for r0 in range(0, tm, chunk):
            rows = slice(r0, r0 + chunk)
            if first:
                xn = _rmsnorm(x_ref[rows, :], g_ref[...]).astype(BF16)
                xn_ref[rows, :] = xn
            else:
                xn = xn_ref[rows, :]
            gate, val = up_conv(r0, 0, xn), up_conv(r0, 1, xn)
            acts.append((rows, (gate * jax.nn.sigmoid(gate) * val).astype(BF16)))
        for rows, act in acts:
            partial = jnp.dot(act, wd_ref[...], preferred_element_type=F32)
            if first:
                y_ref[rows, :] = partial
            elif last:
                y_ref[rows, :] = _rmsnorm(x_ref[rows, :] + (y_ref[rows, :] + partial), gf_ref[...])
            else:
                y_ref[rows, :] += partial
        if tiles_per_seq > 1:
            for half, (_, _, _, _, _, h_ref) in enumerate(halves):
                carry_ref[half, j] = h_ref[tm:tm + CONV_HIST, :]

    last_j = pl.num_programs(1) - 1
    pl.when(j == 0)(functools.partial(step, True, False))
    pl.when(jnp.logical_and(j > 0, j < last_j))(functools.partial(step, False, False))
    pl.when(j == last_j)(functools.partial(step, False, True))


def _conv_ffn(x, seq_len, g, w_up, conv_w, conv_b, w_down, state, g_final):
    n, d = x.shape
    f = w_down.shape[0]
    tm = min(512, n)
    tf = FFN_TILE
    seg = min(seq_len, tm)
    n_seg = tm // seg
    tiles_per_seq = seq_len // seg
    assert f % tf == 0 and n % tm == 0 and tm % seg == 0 and seq_len % seg == 0
    n_f = f // tf
    assert n_f >= 2, "first and last F tile must be different grid steps"
    h_rows = tm + CONV_HIST * n_seg
    lo = lambda i, j: (0, j)
    hi = lambda i, j: (0, n_f + j)
    st_lo = lambda i, j: (i // tiles_per_seq, 0, j)
    st_hi = lambda i, j: (i // tiles_per_seq, 0, n_f + j)
    y, tail_g, tail_v = pl.pallas_call(
        functools.partial(_ffn_kernel, tm=tm, seg=seg, chunk=min(FFN_ROW_CHUNK, tm), tiles_per_seq=tiles_per_seq),
        grid=(n // tm, n_f),
        in_specs=[
            pl.BlockSpec((tm, d), lambda i, j: (i, 0)),
            pl.BlockSpec((1, d), lambda i, j: (0, 0)),
            pl.BlockSpec((d, tf), lo), pl.BlockSpec((d, tf), hi),
            pl.BlockSpec((CONV_W, tf), lo), pl.BlockSpec((CONV_W, tf), hi),
            pl.BlockSpec((1, tf), lo), pl.BlockSpec((1, tf), hi),
            pl.BlockSpec((tf, d), lambda i, j: (j, 0)),
            pl.BlockSpec((n_seg, CONV_W - 1, tf), st_lo), pl.BlockSpec((n_seg, CONV_W - 1, tf), st_hi),
            pl.BlockSpec((1, d), lambda i, j: (0, 0)),
        ],
        out_specs=(
            pl.BlockSpec((tm, d), lambda i, j: (i, 0)),
            pl.BlockSpec((n_seg, CONV_W - 1, tf), lambda i, j: (i, 0, j)),
            pl.BlockSpec((n_seg, CONV_W - 1, tf), lambda i, j: (i, 0, j)),
        ),
        out_shape=(
            jax.ShapeDtypeStruct((n, d), F32),
            jax.ShapeDtypeStruct((n // seg, CONV_W - 1, f), F32),
            jax.ShapeDtypeStruct((n // seg, CONV_W - 1, f), F32),
        ),
        scratch_shapes=[
            pltpu.VMEM((tm, d), BF16),
            pltpu.VMEM((h_rows, tf), F32),
            pltpu.VMEM((h_rows, tf), F32),
            pltpu.VMEM((2, n_f, CONV_HIST, tf), F32),
        ],
        compiler_params=_params(("arbitrary", "arbitrary")),
        name="conv_ffn",
    )(x, g, w_up, w_up, conv_w, conv_w, conv_b, conv_b, w_down, state, state, g_final)
    last_tile = slice(tiles_per_seq - 1, None, tiles_per_seq)
    return y, tail_g[last_tile], tail_v[last_tile]


def _layer(x, pool_state, past_k, past_v, mk, mv, conv_state, p):
    b, l, d = x.shape
    n = b * l
    assert l >= POOL_HIST and l >= CONV_W - 1
    x = x.reshape(n, d)
    u, q, kf, kb, vf, vb, gates = _in_proj(x, p["norm_mix_g"], p["w_in"], p["b_gate"])
    width = u.shape[1]
    if past_k is None:
        past = 0
        hist = jnp.zeros((b, POOL_HIST, width), F32)
    else:
        past = past_k.shape[1]
        hist = jnp.pad(pool_state, ((0, 0), (POOL_HIST - pool_state.shape[1], 0), (0, 0)))
    by_seq = lambda a: a.reshape(b, l, width)
    y_sb = _sb_attention(by_seq(q), by_seq(kb), by_seq(vb), past_k, past_v)

    x2 = _mix(x, l, u, y_sb.reshape(n, width), gates, hist, mk, mv,
              p["pool_w"], p["pool_scale"], p["w_branch_pool"], p["w_branch_sb"], p["w_out"],
              p["norm_xa_g"], p["xa_wq"], p["xa_wo"], past)

    y, new_g, new_v = _conv_ffn(x2, l, p["norm_ffn_g"], p["ffn_w_up"], p["ffn_conv_w"],
                                p["ffn_conv_b"], p["ffn_w_down"], conv_state, p["norm_final_g"])
    new_pool = by_seq(u)[:, l - (POOL_HIST - 1):, :]
    new_conv = jnp.concatenate([new_g, new_v], axis=-1)
    return (y.reshape(b, l, d), new_pool, kf.reshape(b, l, SB_HEADS, HEAD_DIM),
            vf.reshape(b, l, SB_HEADS, HEAD_DIM), new_conv)


def kernel(x_prompt, x_sample, mem_prompt, state_pool, cache_sb_k, cache_sb_v, cache_mem_k, cache_mem_v, state_ffn_conv, norm_mix_g, w_in, b_gate, pool_w, pool_scale, w_branch_pool, w_branch_sb, w_out, norm_xa_g, norm_mem_g, xa_wq, xa_wkv, xa_wo, norm_ffn_g, ffn_w_up, ffn_conv_w, ffn_conv_b, ffn_w_down, norm_final_g):
    assert norm_mix_g.shape[0] == 1, "single-layer step"
    bp, lp, d = x_prompt.shape
    n_mem = mem_prompt.shape[1]
    xa_width = xa_wq.shape[2]
    row = lambda a: a.reshape(1, -1)
    p = {
        "norm_mix_g": row(norm_mix_g[0]), "w_in": w_in[0].astype(BF16), "b_gate": row(b_gate[0]),
        "pool_w": pool_w[0].astype(BF16), "pool_scale": row(pool_scale[0]),
        "w_branch_pool": w_branch_pool[0].astype(BF16), "w_branch_sb": w_branch_sb[0].astype(BF16),
        "w_out": w_out[0].astype(BF16), "norm_xa_g": row(norm_xa_g[0]),
        "xa_wq": xa_wq[0].astype(BF16), "xa_wo": xa_wo[0].astype(BF16),
        "norm_ffn_g": row(norm_ffn_g[0]), "ffn_w_up": ffn_w_up[0].astype(BF16),
        "ffn_conv_w": ffn_conv_w[0], "ffn_conv_b": row(ffn_conv_b[0]),
        "ffn_w_down": ffn_w_down[0].astype(BF16), "norm_final_g": row(norm_final_g),
    }

    mem_kv = _norm_matmul(mem_prompt.reshape(bp * n_mem, d), row(norm_mem_g[0]), xa_wkv[0].astype(BF16))
    mem_kv = mem_kv.reshape(bp, n_mem, 2 * xa_width)
    mk_p, mv_p = mem_kv[..., :xa_width], mem_kv[..., xa_width:]

    conv0 = jnp.zeros((bp, CONV_W - 1, ffn_w_up.shape[2]), F32)
    y_p, pool_p, k_p, v_p, conv_p = _layer(x_prompt, None, None, None, mk_p.astype(BF16), mv_p.astype(BF16),
                                           conv0, p)

    bs = x_sample.shape[0]
    mk_s = cache_mem_k[0].reshape(bs, n_mem, xa_width).astype(BF16)
    mv_s = cache_mem_v[0].reshape(bs, n_mem, xa_width).astype(BF16)
    y_s, pool_s, k_s, v_s, conv_s = _layer(x_sample, state_pool[0], cache_sb_k[0], cache_sb_v[0], mk_s, mv_s,
                                           state_ffn_conv[0], p)

    mem_shape = (1, bp, n_mem, XA_HEADS, HEAD_DIM)
    return (y_p, y_s,
            pool_p[None], k_p[None], v_p[None], mk_p.reshape(mem_shape), mv_p.reshape(mem_shape), conv_p[None],
            pool_s[None], k_s[None], v_s[None], conv_s[None])
```

```python
import functools

import jax
import jax.numpy as jnp
import numpy as np
from jax import lax
from jax.experimental import pallas as pl
from jax.experimental.pallas import tpu as pltpu

F32 = jnp.float32
BF16 = jnp.bfloat16

EPS = 1e-6
POOL_WINDOWS = (2, 4, 8, 16)
POOL_HIST = 16
SB_HEADS = 8
HEAD_DIM = 128
XA_HEADS = 4
CONV_W = 3
CONV_HIST = 8
LANES = 128
FFN_TILE = 512
ROW_CHUNK = 128
FFN_ROW_CHUNK = 256

SB_DEAD_LOG = -88.0

VMEM_LIMIT = 56 * 1024 * 1024


def _rmsnorm(xf, g):
    ms = jnp.mean(xf * xf, axis=-1, keepdims=True)
    return xf * lax.rsqrt(ms + EPS) * g


def _const_spec(shape):
    zeros = (0,) * len(shape)
    return pl.BlockSpec(shape, lambda *_: zeros, pipeline_mode=pl.Buffered(1))


def _params(semantics):
    return pltpu.CompilerParams(dimension_semantics=semantics, vmem_limit_bytes=VMEM_LIMIT)


def _in_proj_kernel(x_ref, g_ref, w_ref, b_ref, u_ref, q_ref, kf_ref, kb_ref, vf_ref, vb_ref, gate_ref,
                    *, q_scale, chunk):
    j = pl.program_id(0)
    tm = x_ref.shape[0]
    half = u_ref.shape[1]

    def project(epilogue):
        for r0 in range(0, tm, chunk):
            rows = slice(r0, r0 + chunk)
            xn = _rmsnorm(x_ref[rows, :], g_ref[...]).astype(BF16)
            epilogue(rows, jnp.dot(xn, w_ref[...], preferred_element_type=F32))

    def put_uq(rows, acc):
        u_ref[rows, :] = acc[:, :half]
        q_ref[rows, :] = (acc[:, half:] * q_scale).astype(BF16)

    def put_heads(f32_ref, bf16_ref, rows, acc):
        f32_ref[rows, :, :] = acc.reshape(acc.shape[0], f32_ref.shape[1], f32_ref.shape[2])
        bf16_ref[rows, :] = acc.astype(BF16)

    def put_kv(rows, acc):
        put_heads(kf_ref, kb_ref, rows, acc[:, :half])
        put_heads(vf_ref, vb_ref, rows, acc[:, half:])

    def put_gate(rows, acc):
        gate_ref[rows, :] = jax.nn.sigmoid(acc + b_ref[...]).astype(BF16)

    pl.when(j == 0)(functools.partial(project, put_uq))
    pl.when(j == 1)(functools.partial(project, put_kv))
    pl.when(j >= 2)(functools.partial(project, put_gate))


def _in_proj(x, g, w_in, b_gate):
    n, d = x.shape
    half = d // 2
    n_col = w_in.shape[1] // d
    heads = half // HEAD_DIM
    tm = min(512, n)
    n_tiles = n // tm

    def rows_while(active):
        return lambda j, i: jnp.where(j < active, 0, jnp.where(j == active, i, n_tiles - 1))

    def flat(active):
        tile = rows_while(active)
        return pl.BlockSpec((tm, half), lambda j, i: (tile(j, i), 0))

    def by_head(active):
        tile = rows_while(active)
        return pl.BlockSpec((tm, heads, HEAD_DIM), lambda j, i: (tile(j, i), 0, 0))

    gate_tile = lambda j, i: (jnp.where(j < 2, 0, i), jnp.maximum(j - 2, 0))
    out_shape = (
        jax.ShapeDtypeStruct((n, half), F32),
        jax.ShapeDtypeStruct((n, half), BF16),
        jax.ShapeDtypeStruct((n, heads, HEAD_DIM), F32),
        jax.ShapeDtypeStruct((n, half), BF16),
        jax.ShapeDtypeStruct((n, heads, HEAD_DIM), F32),
        jax.ShapeDtypeStruct((n, half), BF16),
        jax.ShapeDtypeStruct((n, 2 * d), BF16),
    )
    out_specs = (flat(0), flat(0), by_head(1), flat(1), by_head(1), flat(1), pl.BlockSpec((tm, d), gate_tile))
    return pl.pallas_call(
        functools.partial(_in_proj_kernel, q_scale=1.0 / np.sqrt(HEAD_DIM), chunk=min(ROW_CHUNK, tm)),
        grid=(n_col, n_tiles),
        in_specs=[
            pl.BlockSpec((tm, d), lambda j, i: (i, 0)),
            pl.BlockSpec((1, d), lambda j, i: (0, 0)),
            pl.BlockSpec((d, d), lambda j, i: (0, j)),
            pl.BlockSpec((1, d), lambda j, i: (0, jnp.maximum(j - 2, 0))),
        ],
        out_specs=out_specs,
        out_shape=out_shape,
        compiler_params=_params(("arbitrary", "arbitrary")),
        name="in_proj",
    )(x, g, w_in, b_gate)


def _norm_matmul_kernel(x_ref, g_ref, w_ref, o_ref):
    xn = _rmsnorm(x_ref[...], g_ref[...]).astype(BF16)
    o_ref[...] = jnp.dot(xn, w_ref[...], preferred_element_type=F32)


def _norm_matmul(x, g, w):
    n, d = x.shape
    tm = min(256, n)
    return pl.pallas_call(
        _norm_matmul_kernel,
        grid=(n // tm,),
        in_specs=[pl.BlockSpec((tm, d), lambda i: (i, 0)), _const_spec((1, d)), _const_spec(w.shape)],
        out_specs=pl.BlockSpec((tm, w.shape[1]), lambda i: (i, 0)),
        out_shape=jax.ShapeDtypeStruct((n, w.shape[1]), F32),
        compiler_params=_params(("arbitrary",)),
        name="mem_kv",
    )(x, g, w)


def _sb_kernel(*refs, blk, n_sub, n_heads, key_block_offset, has_past):
    if has_past:
        q_ref, k_ref, v_ref, past_k_ref, past_v_ref, tri1_ref, tri2_ref, o_ref = refs
    else:
        q_ref, k_ref, v_ref, tri1_ref, tri2_ref, o_ref = refs
    step = pl.program_id(2)
    first_span = 2
    wide = first_span * blk
    col_minus_row = (lax.broadcasted_iota(jnp.int32, (blk, wide), 1)
                     - lax.broadcasted_iota(jnp.int32, (blk, wide), 0))
    chains = [(sub, h) for sub in range(n_sub) for h in range(n_heads)]
    rows = lambda sub: slice(sub * blk, (sub + 1) * blk)
    lanes = lambda h: slice(h * HEAD_DIM, (h + 1) * HEAD_DIM)
    qs = [q_ref[0, rows(sub), lanes(h)] for sub, h in chains]
    first_diag = step * n_sub + key_block_offset

    def visit(newest, span, accs, runs, first):
        width = span * blk
        tri = (tri1_ref if span == 1 else tri2_ref)[...]
        log_nots, log_betas, vs, masks = [], [], [], []
        runs = list(runs)
        for c, (sub, h) in enumerate(chains):
            kb_new = first_diag + sub - newest
            kb_old = kb_new - (span - 1)
            start_blk = jnp.maximum(kb_old, 0)
            if first:
                masks.append(col_minus_row[:, :width] < (kb_new - start_blk) * blk)
            else:
                runs[c] = jnp.where(kb_old >= 0, runs[c], -1e30)
            start = pl.multiple_of(start_blk * blk, blk)
            if not has_past:
                k = k_ref[0, pl.ds(start, width), lanes(h)]
                v = v_ref[0, pl.ds(start, width), lanes(h)]
            elif first:
                newest_past = slice((key_block_offset - 1) * blk, key_block_offset * blk)
                k = jnp.concatenate([past_k_ref[0, newest_past, h, :].astype(BF16), k_ref[0, :, lanes(h)]], axis=0)
                v = jnp.concatenate([past_v_ref[0, newest_past, h, :].astype(BF16), v_ref[0, :, lanes(h)]], axis=0)
            else:
                k = past_k_ref[0, pl.ds(start, width), h, :].astype(BF16)
                v = past_v_ref[0, pl.ds(start, width), h, :].astype(BF16)
            vs.append(v)
            z = lax.dot_general(qs[c], k, (((1,), (1,)), ((), ())), preferred_element_type=F32)
            t = jnp.log(1.0 + jnp.exp(-jnp.abs(z)))
            log_not = -(jnp.maximum(z, 0.0) + t)
            if first:
                log_not = jnp.where(masks[c], log_not, 0.0)
            log_nots.append(log_not)
            log_betas.append(jnp.minimum(z, 0.0) - t)
        stacked = jnp.concatenate(log_nots, axis=0)
        hi = stacked.astype(BF16)
        lo = (stacked - hi.astype(F32)).astype(BF16)
        sums = (jnp.dot(hi, tri, preferred_element_type=F32)
                + jnp.dot(lo, tri, preferred_element_type=F32))
        new_accs, new_runs = [], []
        for c in range(len(chains)):
            part = sums[c * blk:(c + 1) * blk]
            between = part[:, :width]
            total = part[:, width:]
            if first:
                a = jnp.where(masks[c], jnp.exp(log_betas[c] + between), 0.0)
            else:
                a = jnp.exp(log_betas[c] + between + runs[c][:, :width])
            new_accs.append(accs[c] + jnp.dot(a.astype(BF16), vs[c], preferred_element_type=F32))
            new_runs.append(runs[c] + total)
        return tuple(new_accs), tuple(new_runs)

    zeros = tuple(jnp.zeros((blk, LANES), F32) for _ in chains)
    accs, runs = visit(0, first_span, zeros, zeros, True)

    def alive(carry):
        newest, _, runs = carry
        live = functools.reduce(jnp.maximum, runs)
        return jnp.logical_and(first_diag + n_sub - 1 - newest >= 0, jnp.max(live) > SB_DEAD_LOG)

    def older(carry):
        newest, accs, runs = carry
        accs, runs = visit(newest, 1, accs, runs, False)
        return newest + 1, accs, runs

    _, accs, _ = lax.while_loop(alive, older, (jnp.int32(first_span), accs, runs))
    for c, (sub, h) in enumerate(chains):
        o_ref[0, rows(sub), lanes(h)] = accs[c].astype(BF16)


def _suffix_sum_matrix(width):
    jj = np.arange(width)
    return jnp.asarray(np.concatenate([jj[:, None] > jj[None, :], np.ones((width, LANES), bool)], axis=1), BF16)


def _sb_attention(q, k, v, past_k=None, past_v=None):
    b, lq, width = q.shape
    has_past = past_k is not None
    past = past_k.shape[1] if has_past else 0
    lk = past + lq
    blk = min(128, lq)
    assert lq % blk == 0 and past % blk == 0 and lk >= 2 * blk
    assert not has_past or lq == blk, "with a cache the new keys must form one block"
    n_q = lq // blk
    n_chains = 8
    n_sub = min(n_chains, n_q)
    n_heads = min(SB_HEADS, n_chains // n_sub)
    assert n_q % n_sub == 0 and SB_HEADS % n_heads == 0
    tri1, tri2 = _suffix_sum_matrix(blk), _suffix_sum_matrix(2 * blk)
    lane_w = n_heads * HEAD_DIM
    keys = pl.BlockSpec((1, lq, lane_w), lambda bi, h, i: (bi, 0, h))
    cache = [pl.BlockSpec((1, past, n_heads, HEAD_DIM), lambda bi, h, i: (bi, 0, h, 0))] * 2 if has_past else []
    return pl.pallas_call(
        functools.partial(_sb_kernel, blk=blk, n_sub=n_sub, n_heads=n_heads,
                          key_block_offset=past // blk, has_past=has_past),
        grid=(b, SB_HEADS // n_heads, n_q // n_sub),
        in_specs=[
            pl.BlockSpec((1, n_sub * blk, lane_w), lambda bi, h, i: (bi, i, h)),
            keys, keys, *cache,
            _const_spec(tri1.shape), _const_spec(tri2.shape),
        ],
        out_specs=pl.BlockSpec((1, n_sub * blk, lane_w), lambda bi, h, i: (bi, i, h)),
        out_shape=jax.ShapeDtypeStruct(q.shape, BF16),
        compiler_params=_params(("arbitrary", "arbitrary", "arbitrary")),
        name="sb_attn",
    )(q, k, v, *((past_k, past_v) if has_past else ()), tri1, tri2)


def _mix_kernel(x_ref, u_ref, ysb_ref, gp_ref, gs_ref, hist_ref, mk_ref, mv_ref,
                pool_w_ref, pool_scale_ref, wbp_ref, wbs_ref, wout_ref, gxa_ref, wq_ref, wo_ref,
                o_ref, ubuf_ref, *, tm, seg, chunk, tiles_per_seq, pos0):
    i = pl.program_id(0)
    group = u_ref.shape[1] // len(POOL_WINDOWS)
    piece = min(chunk, seg)
    u_row = lambda r: r + POOL_HIST * (r // seg + 1)

    if tiles_per_seq > 1:
        tile_in_seq = i % tiles_per_seq

        @pl.when(tile_in_seq == 0)
        def _():
            ubuf_ref[0:POOL_HIST, :] = hist_ref[0]

        seq_row0 = tile_in_seq * tm
    else:
        for s in range(tm // seg):
            top = u_row(s * seg)
            ubuf_ref[top - POOL_HIST:top, :] = hist_ref[s]
        seq_row0 = 0

    for r0 in range(0, tm, chunk):
        rows = slice(r0, r0 + chunk)
        pieces = range(r0, r0 + chunk, piece)
        for r in pieces:
            ubuf_ref[u_row(r):u_row(r) + piece, :] = u_ref[r:r + piece, :]
        pooled = []
        for g, w in enumerate(POOL_WINDOWS):
            cols = slice(g * group, (g + 1) * group)
            deltas = []
            for r in pieces:
                top = u_row(r)
                u = u_ref[r:r + piece, cols]
                window_sum = u
                for back in range(1, w):
                    window_sum = window_sum + ubuf_ref[top - back:top - back + piece, cols]
                pos = pos0 + seq_row0 + r % seg + lax.broadcasted_iota(jnp.int32, (piece, 1), 0)
                count = jnp.minimum(w, pos + 1).astype(F32)
                deltas.append((window_sum / count - u).astype(BF16))
            delta = deltas[0] if len(deltas) == 1 else jnp.concatenate(deltas, axis=0)
            pooled.append(jnp.dot(delta, pool_w_ref[g], preferred_element_type=F32))
        y_pool = (jnp.concatenate(pooled, axis=-1) * pool_scale_ref[...]).astype(BF16)

        branch_pool = jnp.dot(y_pool, wbp_ref[...], preferred_element_type=F32)
        branch_sb = jnp.dot(ysb_ref[rows, :], wbs_ref[...], preferred_element_type=F32)
        merged = gp_ref[rows, :].astype(F32) * branch_pool + gs_ref[rows, :].astype(F32) * branch_sb
        x1 = x_ref[rows, :] + jnp.dot(merged.astype(BF16), wout_ref[...], preferred_element_type=F32)

        xn = _rmsnorm(x1, gxa_ref[...]).astype(BF16)
        q = (jnp.dot(xn, wq_ref[...], preferred_element_type=F32) * (1.0 / np.sqrt(HEAD_DIM))).astype(BF16)
        attended = []
        for r in pieces:
            s = r // seg
            heads = []
            for h in range(XA_HEADS):
                cols = slice(h * HEAD_DIM, (h + 1) * HEAD_DIM)
                sc = lax.dot_general(q[r - r0:r - r0 + piece, cols], mk_ref[s, :, cols], (((1,), (1,)), ((), ())),
                                     preferred_element_type=F32)
                p = jnp.exp(sc - jnp.max(sc, axis=-1, keepdims=True))
                p = p / jnp.sum(p, axis=-1, keepdims=True)
                heads.append(jnp.dot(p.astype(BF16), mv_ref[s, :, cols], preferred_element_type=F32))
            attended.append(jnp.concatenate(heads, axis=-1).astype(BF16))
        attn = attended[0] if len(attended) == 1 else jnp.concatenate(attended, axis=0)
        o_ref[rows, :] = x1 + jnp.dot(attn, wo_ref[...], preferred_element_type=F32)

    if tiles_per_seq > 1:
        ubuf_ref[0:POOL_HIST, :] = ubuf_ref[tm:tm + POOL_HIST, :]


def _mix(x, seq_len, u, ysb, gates, hist, mk, mv, pool_w, pool_scale, wbp, wbs, wout, gxa, wq, wo, pos0):
    n, d = x.shape
    tm = min(256, n)
    seg = min(seq_len, tm)
    n_seg = tm // seg
    tiles_per_seq = seq_len // seg
    assert n % tm == 0 and tm % seg == 0 and seq_len % seg == 0
    tile = lambda i: (i, 0)
    per_seq = lambda i: (i // tiles_per_seq, 0, 0)
    return pl.pallas_call(
        functools.partial(_mix_kernel, tm=tm, seg=seg, chunk=tm, tiles_per_seq=tiles_per_seq,
                          pos0=pos0),
        grid=(n // tm,),
        in_specs=[
            pl.BlockSpec((tm, d), tile),
            pl.BlockSpec((tm, u.shape[1]), tile),
            pl.BlockSpec((tm, ysb.shape[1]), tile),
            pl.BlockSpec((tm, d), tile),
            pl.BlockSpec((tm, d), lambda i: (i, 1)),
            pl.BlockSpec((n_seg,) + hist.shape[1:], per_seq),
            pl.BlockSpec((n_seg,) + mk.shape[1:], per_seq),
            pl.BlockSpec((n_seg,) + mv.shape[1:], per_seq),
            _const_spec(pool_w.shape), _const_spec(pool_scale.shape), _const_spec(wbp.shape),
            _const_spec(wbs.shape), _const_spec(wout.shape), _const_spec(gxa.shape),
            _const_spec(wq.shape), _const_spec(wo.shape),
        ],
        out_specs=pl.BlockSpec((tm, d), tile),
        out_shape=jax.ShapeDtypeStruct(x.shape, F32),
        scratch_shapes=[pltpu.VMEM((tm + POOL_HIST * n_seg, u.shape[1]), F32)],
        compiler_params=_params(("arbitrary",)),
        name="mix",
    )(x, u, ysb, gates, gates, hist, mk, mv, pool_w, pool_scale, wbp, wbs, wout, gxa, wq, wo)


def _ffn_kernel(x_ref, g_ref, wg_ref, wv_ref, conv_ref, wd_ref, state_ref, gf_ref, y_ref, tail_ref,
                xn_ref, hg_ref, hv_ref, carry_ref, *, tm, seg, chunk, tiles_per_seq):
    i = pl.program_id(0)
    j = pl.program_id(1)
    piece = min(chunk, seg)
    lags = CONV_W - 1
    tf = wg_ref.shape[1]
    n_f = pl.num_programs(1)
    cols = [pl.ds(pl.multiple_of((half * n_f + j) * tf, tf), tf) for half in range(2)]
    halves = ((wg_ref, hg_ref), (wv_ref, hv_ref))
    h_row = lambda r: r + CONV_HIST * (r // seg + 1)

    for half, (_, h_ref) in enumerate(halves):
        if tiles_per_seq > 1:
            first_of_seq = (i % tiles_per_seq) == 0

            @pl.when(first_of_seq)
            def _(h_ref=h_ref, half=half):
                h_ref[CONV_HIST - lags:CONV_HIST, :] = state_ref[0, :, cols[half]]

            @pl.when(jnp.logical_not(first_of_seq))
            def _(h_ref=h_ref, half=half):
                h_ref[0:CONV_HIST, :] = carry_ref[half, j]
        else:
            for s in range(tm // seg):
                top = h_row(s * seg)
                h_ref[top - lags:top, :] = state_ref[s, :, cols[half]]

    def up_conv(r0, half, xn):
        w_ref, h_ref = halves[half]
        taps = conv_ref[0:CONV_W, cols[half]]
        bias = conv_ref[CONV_W:CONV_W + 1, cols[half]]
        h = jnp.dot(xn, w_ref[...], preferred_element_type=F32)
        outs = []
        for p0 in range(0, chunk, piece):
            top = h_row(r0 + p0)
            hp = h[p0:p0 + piece]
            h_ref[top:top + piece, :] = hp
            out = taps[lags:lags + 1, :] * hp
            for tap in range(lags):
                lag = lags - tap
                out = out + taps[tap:tap + 1, :] * h_ref[top - lag:top - lag + piece, :]
            outs.append(out + bias)
            seg_end = r0 + p0 + piece
            if seg_end % seg == 0:
                tail_ref[seg_end // seg - 1, :, cols[half]] = hp[piece - lags:, :]
        return outs[0] if len(outs) == 1 else jnp.concatenate(outs, axis=0)

    def step(first, last):
        acts = []
        for r0 in range(0, tm, chunk):
            rows = slice(r0, r0 + chunk)
            if first:
                xn = _rmsnorm(x_ref[rows, :], g_ref[...]).astype(BF16)
                xn_ref[rows, :] = xn
            else:
                xn = xn_ref[rows, :]
            gate, val = up_conv(r0, 0, xn), up_conv(r0, 1, xn)
            acts.append((rows, (gate * jax.nn.sigmoid(gate) * val).astype(BF16)))
        for rows, act in acts:
            partial = jnp.dot(act, wd_ref[...], preferred_element_type=F32)
            if first:
                y_ref[rows, :] = partial
            elif last:
                y_ref[rows, :] = _rmsnorm(x_ref[rows, :] + (y_ref[rows, :] + partial), gf_ref[...])
            else:
                y_ref[rows, :] += partial
        if tiles_per_seq > 1:
            for half, (_, h_ref) in enumerate(halves):
                carry_ref[half, j] = h_ref[tm:tm + CONV_HIST, :]

    last_j = pl.num_programs(1) - 1
    pl.when(j == 0)(functools.partial(step, True, False))
    pl.when(jnp.logical_and(j > 0, j < last_j))(functools.partial(step, False, False))
    pl.when(j == last_j)(functools.partial(step, False, True))


def _conv_ffn(x, seq_len, g, w_up, conv_w, conv_b, w_down, state, g_final):
    n, d = x.shape
    f = w_down.shape[0]
    tm = min(512, n)
    tf = FFN_TILE
    seg = min(seq_len, tm)
    n_seg = tm // seg
    tiles_per_seq = seq_len // seg
    assert f % tf == 0 and n % tm == 0 and tm % seg == 0 and seq_len % seg == 0
    n_f = f // tf
    assert n_f >= 2, "first and last F tile must be different grid steps"
    h_rows = tm + CONV_HIST * n_seg
    conv = jnp.concatenate([conv_w, conv_b], axis=0)
    y, tails = pl.pallas_call(
        functools.partial(_ffn_kernel, tm=tm, seg=seg, chunk=min(FFN_ROW_CHUNK, tm), tiles_per_seq=tiles_per_seq),
        grid=(n // tm, n_f),
        in_specs=[
            pl.BlockSpec((tm, d), lambda i, j: (i, 0)),
            _const_spec((1, d)),
            pl.BlockSpec((d, tf), lambda i, j: (0, j)), pl.BlockSpec((d, tf), lambda i, j: (0, n_f + j)),
            _const_spec(conv.shape),
            pl.BlockSpec((tf, d), lambda i, j: (j, 0)),
            pl.BlockSpec((n_seg, CONV_W - 1, 2 * f), lambda i, j: (i // tiles_per_seq, 0, 0)),
            _const_spec((1, d)),
        ],
        out_specs=(
            pl.BlockSpec((tm, d), lambda i, j: (i, 0)),
            pl.BlockSpec((n_seg, CONV_W - 1, 2 * f), lambda i, j: (i, 0, 0)),
        ),
        out_shape=(
            jax.ShapeDtypeStruct((n, d), F32),
            jax.ShapeDtypeStruct((n // seg, CONV_W - 1, 2 * f), F32),
        ),
        scratch_shapes=[
            pltpu.VMEM((tm, d), BF16),
            pltpu.VMEM((h_rows, tf), F32),
            pltpu.VMEM((h_rows, tf), F32),
            pltpu.VMEM((2, n_f, CONV_HIST, tf), F32),
        ],
        compiler_params=_params(("arbitrary", "arbitrary")),
        name="conv_ffn",
    )(x, g, w_up, w_up, conv, w_down, state, g_final)
    return y, tails[tiles_per_seq - 1::tiles_per_seq]


def _layer(x, pool_state, past_k, past_v, mk, mv, conv_state, p):
    b, l, d = x.shape
    n = b * l
    assert l >= POOL_HIST and l >= CONV_W - 1
    x = x.reshape(n, d)
    u, q, kf, kb, vf, vb, gates = _in_proj(x, p["norm_mix_g"], p["w_in"], p["b_gate"])
    width = u.shape[1]
    if past_k is None:
        past = 0
        hist = jnp.zeros((b, POOL_HIST, width), F32)
    else:
        past = past_k.shape[1]
        hist = jnp.pad(pool_state, ((0, 0), (POOL_HIST - pool_state.shape[1], 0), (0, 0)))
    by_seq = lambda a: a.reshape(b, l, width)
    y_sb = _sb_attention(by_seq(q), by_seq(kb), by_seq(vb), past_k, past_v)

    x2 = _mix(x, l, u, y_sb.reshape(n, width), gates, hist, mk, mv,
              p["pool_w"], p["pool_scale"], p["w_branch_pool"], p["w_branch_sb"], p["w_out"],
              p["norm_xa_g"], p["xa_wq"], p["xa_wo"], past)

    y, new_conv = _conv_ffn(x2, l, p["norm_ffn_g"], p["ffn_w_up"], p["ffn_conv_w"],
                            p["ffn_conv_b"], p["ffn_w_down"], conv_state, p["norm_final_g"])
    new_pool = by_seq(u)[:, l - (POOL_HIST - 1):, :]
    return (y.reshape(b, l, d), new_pool, kf.reshape(b, l, SB_HEADS, HEAD_DIM),
            vf.reshape(b, l, SB_HEADS, HEAD_DIM), new_conv)


def kernel(x_prompt, x_sample, mem_prompt, state_pool, cache_sb_k, cache_sb_v, cache_mem_k, cache_mem_v, state_ffn_conv, norm_mix_g, w_in, b_gate, pool_w, pool_scale, w_branch_pool, w_branch_sb, w_out, norm_xa_g, norm_mem_g, xa_wq, xa_wkv, xa_wo, norm_ffn_g, ffn_w_up, ffn_conv_w, ffn_conv_b, ffn_w_down, norm_final_g):
    assert norm_mix_g.shape[0] == 1, "single-layer step"
    bp, lp, d = x_prompt.shape
    n_mem = mem_prompt.shape[1]
    xa_width = xa_wq.shape[2]
    row = lambda a: a.reshape(1, -1)
    p = {
        "norm_mix_g": row(norm_mix_g[0]), "w_in": w_in[0].astype(BF16), "b_gate": row(b_gate[0]),
        "pool_w": pool_w[0].astype(BF16), "pool_scale": row(pool_scale[0]),
        "w_branch_pool": w_branch_pool[0].astype(BF16), "w_branch_sb": w_branch_sb[0].astype(BF16),
        "w_out": w_out[0].astype(BF16), "norm_xa_g": row(norm_xa_g[0]),
        "xa_wq": xa_wq[0].astype(BF16), "xa_wo": xa_wo[0].astype(BF16),
        "norm_ffn_g": row(norm_ffn_g[0]), "ffn_w_up": ffn_w_up[0].astype(BF16),
        "ffn_conv_w": ffn_conv_w[0], "ffn_conv_b": row(ffn_conv_b[0]),
        "ffn_w_down": ffn_w_down[0].astype(BF16), "norm_final_g": row(norm_final_g),
    }

    mem_kv = _norm_matmul(mem_prompt.reshape(bp * n_mem, d), row(norm_mem_g[0]), xa_wkv[0].astype(BF16))
    mem_kv = mem_kv.reshape(bp, n_mem, 2 * xa_width)
    mk_p, mv_p = mem_kv[..., :xa_width], mem_kv[..., xa_width:]

    conv0 = jnp.zeros((bp, CONV_W - 1, ffn_w_up.shape[2]), F32)
    y_p, pool_p, k_p, v_p, conv_p = _layer(x_prompt, None, None, None, mk_p.astype(BF16), mv_p.astype(BF16),
                                           conv0, p)

    bs = x_sample.shape[0]
    mk_s = cache_mem_k[0].reshape(bs, n_mem, xa_width).astype(BF16)
    mv_s = cache_mem_v[0].reshape(bs, n_mem, xa_width).astype(BF16)
    y_s, pool_s, k_s, v_s, conv_s = _layer(x_sample, state_pool[0], cache_sb_k[0], cache_sb_v[0], mk_s, mv_s,
                                           state_ffn_conv[0], p)

    mem_shape = (1, bp, n_mem, XA_HEADS, HEAD_DIM)
    return (y_p, y_s,
            pool_p[None], k_p[None], v_p[None], mk_p.reshape(mem_shape), mv_p.reshape(mem_shape), conv_p[None],
            pool_s[None], k_s[None], v_s[None], conv_s[None])
```

```python
import functools

import jax
import jax.numpy as jnp
import numpy as np
from jax import lax
from jax.experimental import pallas as pl
from jax.experimental.pallas import tpu as pltpu

F32 = jnp.float32
BF16 = jnp.bfloat16

EPS = 1e-6
POOL_WINDOWS = (2, 4, 8, 16)
POOL_HIST = 16
SB_HEADS = 8
HEAD_DIM = 128
XA_HEADS = 4
CONV_W = 3
CONV_HIST = 8
LANES = 128
FFN_TILE = 512
ROW_CHUNK = 128
FFN_ROW_CHUNK = 256

SB_DEAD_LOG2 = -127.0
LOG2_E = 1.4426950408889634

VMEM_LIMIT = 56 * 1024 * 1024


def _rmsnorm(xf, g):
    ms = jnp.mean(xf * xf, axis=-1, keepdims=True)
    return xf * lax.rsqrt(ms + EPS) * g


def _const_spec(shape):
    zeros = (0,) * len(shape)
    return pl.BlockSpec(shape, lambda *_: zeros, pipeline_mode=pl.Buffered(1))


def _params(semantics):
    return pltpu.CompilerParams(dimension_semantics=semantics, vmem_limit_bytes=VMEM_LIMIT)


def _in_proj_kernel(x_ref, g_ref, w_ref, b_ref, u_ref, q_ref, kf_ref, kb_ref, vf_ref, vb_ref, gate_ref,
                    *, q_scale, chunk):
    j = pl.program_id(0)
    tm = x_ref.shape[0]
    half = u_ref.shape[1]

    def project(epilogue):
        for r0 in range(0, tm, chunk):
            rows = slice(r0, r0 + chunk)
            xn = _rmsnorm(x_ref[rows, :], g_ref[...]).astype(BF16)
            epilogue(rows, jnp.dot(xn, w_ref[...], preferred_element_type=F32))

    def put_uq(rows, acc):
        u_ref[rows, :] = acc[:, :half]
        q_ref[rows, :] = (acc[:, half:] * q_scale).astype(BF16)

    def put_heads(f32_ref, bf16_ref, rows, acc):
        f32_ref[rows, :, :] = acc.reshape(acc.shape[0], f32_ref.shape[1], f32_ref.shape[2])
        bf16_ref[rows, :] = acc.astype(BF16)

    def put_kv(rows, acc):
        put_heads(kf_ref, kb_ref, rows, acc[:, :half])
        put_heads(vf_ref, vb_ref, rows, acc[:, half:])

    def put_gate(rows, acc):
        gate_ref[rows, :] = jax.nn.sigmoid(acc + b_ref[...]).astype(BF16)

    pl.when(j == 0)(functools.partial(project, put_uq))
    pl.when(j == 1)(functools.partial(project, put_kv))
    pl.when(j >= 2)(functools.partial(project, put_gate))


def _in_proj(x, g, w_in, b_gate):
    n, d = x.shape
    half = d // 2
    n_col = w_in.shape[1] // d
    heads = half // HEAD_DIM
    tm = min(512, n)
    n_tiles = n // tm

    def rows_while(active):
        return lambda j, i: jnp.where(j < active, 0, jnp.where(j == active, i, n_tiles - 1))

    def flat(active):
        tile = rows_while(active)
        return pl.BlockSpec((tm, half), lambda j, i: (tile(j, i), 0))

    def by_head(active):
        tile = rows_while(active)
        return pl.BlockSpec((tm, heads, HEAD_DIM), lambda j, i: (tile(j, i), 0, 0))

    gate_tile = lambda j, i: (jnp.where(j < 2, 0, i), jnp.maximum(j - 2, 0))
    out_shape = (
        jax.ShapeDtypeStruct((n, half), F32),
        jax.ShapeDtypeStruct((n, half), BF16),
        jax.ShapeDtypeStruct((n, heads, HEAD_DIM), F32),
        jax.ShapeDtypeStruct((n, half), BF16),
        jax.ShapeDtypeStruct((n, heads, HEAD_DIM), F32),
        jax.ShapeDtypeStruct((n, half), BF16),
        jax.ShapeDtypeStruct((n, 2 * d), BF16),
    )
    out_specs = (flat(0), flat(0), by_head(1), flat(1), by_head(1), flat(1), pl.BlockSpec((tm, d), gate_tile))
    return pl.pallas_call(
        functools.partial(_in_proj_kernel, q_scale=LOG2_E / np.sqrt(HEAD_DIM), chunk=min(ROW_CHUNK, tm)),
        grid=(n_col, n_tiles),
        in_specs=[
            pl.BlockSpec((tm, d), lambda j, i: (i, 0)),
            pl.BlockSpec((1, d), lambda j, i: (0, 0)),
            pl.BlockSpec((d, d), lambda j, i: (0, j)),
            pl.BlockSpec((1, d), lambda j, i: (0, jnp.maximum(j - 2, 0))),
        ],
        out_specs=out_specs,
        out_shape=out_shape,
        compiler_params=_params(("arbitrary", "arbitrary")),
        name="in_proj",
    )(x, g, w_in, b_gate)


def _norm_matmul_kernel(x_ref, g_ref, w_ref, o_ref):
    xn = _rmsnorm(x_ref[...], g_ref[...]).astype(BF16)
    o_ref[...] = jnp.dot(xn, w_ref[...], preferred_element_type=F32)


def _norm_matmul(x, g, w):
    n, d = x.shape
    tm = min(256, n)
    return pl.pallas_call(
        _norm_matmul_kernel,
        grid=(n // tm,),
        in_specs=[pl.BlockSpec((tm, d), lambda i: (i, 0)), _const_spec((1, d)), _const_spec(w.shape)],
        out_specs=pl.BlockSpec((tm, w.shape[1]), lambda i: (i, 0)),
        out_shape=jax.ShapeDtypeStruct((n, w.shape[1]), F32),
        compiler_params=_params(("arbitrary",)),
        name="mem_kv",
    )(x, g, w)


def _sb_kernel(*refs, blk, n_sub, n_heads, key_block_offset, has_past, side_blocks):
    n_side = len(side_blocks)
    n_in = len(refs) - 1 - n_side
    side_in, side_out = refs[n_in - n_side:n_in], refs[n_in + 1:]
    o_ref = refs[n_in]
    if has_past:
        q_ref, k_ref, v_ref, past_k_ref, past_v_ref, tri1_ref, tri2_ref = refs[:n_in - n_side]
    else:
        q_ref, k_ref, v_ref, tri1_ref, tri2_ref = refs[:n_in - n_side]
    step = pl.program_id(2)

    linear_step = (pl.program_id(0) * pl.num_programs(1) + pl.program_id(1)) * pl.num_programs(2) + step
    for src_ref, dst_ref, n_blocks in zip(side_in, side_out, side_blocks):
        @pl.when(linear_step < n_blocks)
        def _(src_ref=src_ref, dst_ref=dst_ref):
            dst_ref[...] = src_ref[...].astype(BF16)

    first_span = 2
    wide = first_span * blk
    col_minus_row = (lax.broadcasted_iota(jnp.int32, (blk, wide), 1)
                     - lax.broadcasted_iota(jnp.int32, (blk, wide), 0))
    chains = [(sub, h) for sub in range(n_sub) for h in range(n_heads)]
    rows = lambda sub: slice(sub * blk, (sub + 1) * blk)
    lanes = lambda h: slice(h * HEAD_DIM, (h + 1) * HEAD_DIM)
    qs = [q_ref[0, rows(sub), lanes(h)] for sub, h in chains]
    first_diag = step * n_sub + key_block_offset

    def visit(newest, span, accs, runs, first):
        width = span * blk
        tri = (tri1_ref if span == 1 else tri2_ref)[...]
        log_nots, log_betas, vs, masks = [], [], [], []
        runs = list(runs)
        for c, (sub, h) in enumerate(chains):
            kb_new = first_diag + sub - newest
            kb_old = kb_new - (span - 1)
            start_blk = jnp.maximum(kb_old, 0)
            if first:
                masks.append(col_minus_row[:, :width] < (kb_new - start_blk) * blk)
            else:
                runs[c] = jnp.where(kb_old >= 0, runs[c], -1e30)
            start = pl.multiple_of(start_blk * blk, blk)
            if not has_past:
                k = k_ref[0, pl.ds(start, width), lanes(h)]
                v = v_ref[0, pl.ds(start, width), lanes(h)]
            elif first:
                newest_past = slice((key_block_offset - 1) * blk, key_block_offset * blk)
                k = jnp.concatenate([past_k_ref[0, newest_past, h, :].astype(BF16), k_ref[0, :, lanes(h)]], axis=0)
                v = jnp.concatenate([past_v_ref[0, newest_past, h, :].astype(BF16), v_ref[0, :, lanes(h)]], axis=0)
            else:
                k = past_k_ref[0, pl.ds(start, width), h, :].astype(BF16)
                v = past_v_ref[0, pl.ds(start, width), h, :].astype(BF16)
            vs.append(v)
            z = lax.dot_general(qs[c], k, (((1,), (1,)), ((), ())), preferred_element_type=F32)
            t = jnp.log2(1.0 + jnp.exp2(-jnp.abs(z)))
            log_not = -(jnp.maximum(z, 0.0) + t)
            if first:
                log_not = jnp.where(masks[c], log_not, 0.0)
            log_nots.append(log_not)
            log_betas.append(jnp.minimum(z, 0.0) - t)
        stacked = jnp.concatenate(log_nots, axis=0)
        hi = stacked.astype(BF16)
        lo = (stacked - hi.astype(F32)).astype(BF16)
        sums = (jnp.dot(hi, tri, preferred_element_type=F32)
                + jnp.dot(lo, tri, preferred_element_type=F32))
        new_accs, new_runs = [], []
        for c in range(len(chains)):
            part = sums[c * blk:(c + 1) * blk]
            between = part[:, :width]
            total = part[:, width:]
            if first:
                a = jnp.where(masks[c], jnp.exp2(log_betas[c] + between), 0.0)
            else:
                a = jnp.exp2(log_betas[c] + between + runs[c][:, :width])
            new_accs.append(accs[c] + jnp.dot(a.astype(BF16), vs[c], preferred_element_type=F32))
            new_runs.append(runs[c] + total)
        return tuple(new_accs), tuple(new_runs)

    zeros = tuple(jnp.zeros((blk, LANES), F32) for _ in chains)
    accs, runs = visit(0, first_span, zeros, zeros, True)

    def alive(carry):
        newest, _, runs = carry
        live = functools.reduce(jnp.maximum, runs)
        return jnp.logical_and(first_diag + n_sub - 1 - newest >= 0, jnp.max(live) > SB_DEAD_LOG2)

    def older(carry):
        newest, accs, runs = carry
        accs, runs = visit(newest, 1, accs, runs, False)
        return newest + 1, accs, runs

    _, accs, _ = lax.while_loop(alive, older, (jnp.int32(first_span), accs, runs))
    for c, (sub, h) in enumerate(chains):
        o_ref[0, rows(sub), lanes(h)] = accs[c].astype(BF16)


def _suffix_sum_matrix(width):
    jj = np.arange(width)
    return jnp.asarray(np.concatenate([jj[:, None] > jj[None, :], np.ones((width, LANES), bool)], axis=1), BF16)


def _row_blocks(rows, max_blocks):
    units = rows // 16
    assert units * 16 == rows
    return max(nb for nb in range(1, min(units, max_blocks) + 1) if units % nb == 0)


def _sb_attention(q, k, v, past_k=None, past_v=None, side=()):
    b, lq, width = q.shape
    has_past = past_k is not None
    past = past_k.shape[1] if has_past else 0
    lk = past + lq
    blk = min(128, lq)
    assert lq % blk == 0 and past % blk == 0 and lk >= 2 * blk
    assert not has_past or lq == blk, "with a cache the new keys must form one block"
    n_q = lq // blk
    n_chains = 8
    n_sub = min(n_chains, n_q)
    n_heads = min(SB_HEADS, n_chains // n_sub)
    assert n_q % n_sub == 0 and SB_HEADS % n_heads == 0
    tri1, tri2 = _suffix_sum_matrix(blk), _suffix_sum_matrix(2 * blk)
    lane_w = n_heads * HEAD_DIM
    keys = pl.BlockSpec((1, lq, lane_w), lambda bi, h, i: (bi, 0, h))
    cache = [pl.BlockSpec((1, past, n_heads, HEAD_DIM), lambda bi, h, i: (bi, 0, h, 0))] * 2 if has_past else []
    grid = (b, SB_HEADS // n_heads, n_q // n_sub)
    n_steps = grid[0] * grid[1] * grid[2]
    side_blocks = tuple(_row_blocks(a.shape[0], n_steps) for a in side)

    def side_spec(a, n_blocks):
        return pl.BlockSpec((a.shape[0] // n_blocks, a.shape[1]),
                            lambda bi, h, i: (jnp.minimum((bi * grid[1] + h) * grid[2] + i, n_blocks - 1), 0))

    side_specs = [side_spec(a, nb) for a, nb in zip(side, side_blocks)]
    out = pl.pallas_call(
        functools.partial(_sb_kernel, blk=blk, n_sub=n_sub, n_heads=n_heads,
                          key_block_offset=past // blk, has_past=has_past, side_blocks=side_blocks),
        grid=grid,
        in_specs=[
            pl.BlockSpec((1, n_sub * blk, lane_w), lambda bi, h, i: (bi, i, h)),
            keys, keys, *cache,
            _const_spec(tri1.shape), _const_spec(tri2.shape),
            *side_specs,
        ],
        out_specs=(pl.BlockSpec((1, n_sub * blk, lane_w), lambda bi, h, i: (bi, i, h)), *side_specs),
        out_shape=(jax.ShapeDtypeStruct(q.shape, BF16), *(jax.ShapeDtypeStruct(a.shape, BF16) for a in side)),
        compiler_params=_params(("arbitrary", "arbitrary", "arbitrary")),
        name="sb_attn",
    )(q, k, v, *((past_k, past_v) if has_past else ()), tri1, tri2, *side)
    return out[0], out[1:]


def _mix_kernel(x_ref, u_ref, ysb_ref, gp_ref, gs_ref, hist_ref, mk_ref, mv_ref,
                pool_w_ref, pool_scale_ref, wbp_ref, wbs_ref, wout_ref, gxa_ref, wq_ref, wo_ref,
                o_ref, ubuf_ref, *, tm, seg, chunk, tiles_per_seq, pos0):
    i = pl.program_id(0)
    group = u_ref.shape[1] // len(POOL_WINDOWS)
    piece = min(chunk, seg)
    u_row = lambda r: r + POOL_HIST * (r // seg + 1)

    if tiles_per_seq > 1:
        tile_in_seq = i % tiles_per_seq

        @pl.when(tile_in_seq == 0)
        def _():
            ubuf_ref[0:POOL_HIST, :] = hist_ref[0]

        seq_row0 = tile_in_seq * tm
    else:
        for s in range(tm // seg):
            top = u_row(s * seg)
            ubuf_ref[top - POOL_HIST:top, :] = hist_ref[s]
        seq_row0 = 0

    for r0 in range(0, tm, chunk):
        rows = slice(r0, r0 + chunk)
        pieces = range(r0, r0 + chunk, piece)
        for r in pieces:
            ubuf_ref[u_row(r):u_row(r) + piece, :] = u_ref[r:r + piece, :]
        pooled = []
        for g, w in enumerate(POOL_WINDOWS):
            cols = slice(g * group, (g + 1) * group)
            deltas = []
            for r in pieces:
                top = u_row(r)
                u = u_ref[r:r + piece, cols]
                partial = ubuf_ref[top - POOL_HIST:top + piece, cols]
                span = 1
                while span < w:
                    partial = partial + pltpu.roll(partial, span, axis=0)
                    span *= 2
                window_sum = partial[POOL_HIST:]
                pos = pos0 + seq_row0 + r % seg + lax.broadcasted_iota(jnp.int32, (piece, 1), 0)
                count = jnp.minimum(w, pos + 1).astype(F32)
                deltas.append((window_sum / count - u).astype(BF16))
            delta = deltas[0] if len(deltas) == 1 else jnp.concatenate(deltas, axis=0)
            pooled.append(jnp.dot(delta, pool_w_ref[g], preferred_element_type=F32))
        y_pool = (jnp.concatenate(pooled, axis=-1) * pool_scale_ref[...]).astype(BF16)

        branch_pool = jnp.dot(y_pool, wbp_ref[...], preferred_element_type=F32)
        branch_sb = jnp.dot(ysb_ref[rows, :], wbs_ref[...], preferred_element_type=F32)
        merged = gp_ref[rows, :].astype(F32) * branch_pool + gs_ref[rows, :].astype(F32) * branch_sb
        x1 = x_ref[rows, :] + jnp.dot(merged.astype(BF16), wout_ref[...], preferred_element_type=F32)

        xn = _rmsnorm(x1, gxa_ref[...]).astype(BF16)
        q = (jnp.dot(xn, wq_ref[...], preferred_element_type=F32) * (1.0 / np.sqrt(HEAD_DIM))).astype(BF16)
        attended = []
        for r in pieces:
            s = r // seg
            heads = []
            for h in range(XA_HEADS):
                cols = slice(h * HEAD_DIM, (h + 1) * HEAD_DIM)
                sc = lax.dot_general(q[r - r0:r - r0 + piece, cols], mk_ref[s, :, cols], (((1,), (1,)), ((), ())),
                                     preferred_element_type=F32)
                p = jnp.exp(sc - jnp.max(sc, axis=-1, keepdims=True))
                p = p / jnp.sum(p, axis=-1, keepdims=True)
                heads.append(jnp.dot(p.astype(BF16), mv_ref[s, :, cols], preferred_element_type=F32))
            attended.append(jnp.concatenate(heads, axis=-1).astype(BF16))
        attn = attended[0] if len(attended) == 1 else jnp.concatenate(attended, axis=0)
        o_ref[rows, :] = x1 + jnp.dot(attn, wo_ref[...], preferred_element_type=F32)

    if tiles_per_seq > 1:
        ubuf_ref[0:POOL_HIST, :] = ubuf_ref[tm:tm + POOL_HIST, :]


def _mix(x, seq_len, u, ysb, gates, hist, mk, mv, pool_w, pool_scale, wbp, wbs, wout, gxa, wq, wo, pos0):
    n, d = x.shape
    tm = min(256, n)
    seg = min(seq_len, tm)
    n_seg = tm // seg
    tiles_per_seq = seq_len // seg
    assert n % tm == 0 and tm % seg == 0 and seq_len % seg == 0
    tile = lambda i: (i, 0)
    per_seq = lambda i: (i // tiles_per_seq, 0, 0)
    return pl.pallas_call(
        functools.partial(_mix_kernel, tm=tm, seg=seg, chunk=tm, tiles_per_seq=tiles_per_seq,
                          pos0=pos0),
        grid=(n // tm,),
        in_specs=[
            pl.BlockSpec((tm, d), tile),
            pl.BlockSpec((tm, u.shape[1]), tile),
            pl.BlockSpec((tm, ysb.shape[1]), tile),
            pl.BlockSpec((tm, d), tile),
            pl.BlockSpec((tm, d), lambda i: (i, 1)),
            pl.BlockSpec((n_seg,) + hist.shape[1:], per_seq),
            pl.BlockSpec((n_seg,) + mk.shape[1:], per_seq),
            pl.BlockSpec((n_seg,) + mv.shape[1:], per_seq),
            _const_spec(pool_w.shape), _const_spec(pool_scale.shape), _const_spec(wbp.shape),
            _const_spec(wbs.shape), _const_spec(wout.shape), _const_spec(gxa.shape),
            _const_spec(wq.shape), _const_spec(wo.shape),
        ],
        out_specs=pl.BlockSpec((tm, d), tile),
        out_shape=jax.ShapeDtypeStruct(x.shape, F32),
        scratch_shapes=[pltpu.VMEM((tm + POOL_HIST * n_seg, u.shape[1]), F32)],
        compiler_params=_params(("arbitrary",)),
        name="mix",
    )(x, u, ysb, gates, gates, hist, mk, mv, pool_w, pool_scale, wbp, wbs, wout, gxa, wq, wo)


def _ffn_kernel(x_ref, g_ref, wg_ref, wv_ref, conv_ref, wd_ref, state_ref, gf_ref, y_ref, tail_ref,
                xn_ref, hg_ref, hv_ref, carry_ref, *, tm, seg, chunk, tiles_per_seq):
    i = pl.program_id(0)
    j = pl.program_id(1)
    piece = min(chunk, seg)
    lags = CONV_W - 1
    tf = wg_ref.shape[1]
    n_f = pl.num_programs(1)
    cols = [pl.ds(pl.multiple_of((half * n_f + j) * tf, tf), tf) for half in range(2)]
    halves = ((wg_ref, hg_ref), (wv_ref, hv_ref))
    h_row = lambda r: r + CONV_HIST * (r // seg + 1)

    for half, (_, h_ref) in enumerate(halves):
        if tiles_per_seq > 1:
            first_of_seq = (i % tiles_per_seq) == 0

            @pl.when(first_of_seq)
            def _(h_ref=h_ref, half=half):
                h_ref[CONV_HIST - lags:CONV_HIST, :] = state_ref[0, :, cols[half]]

            @pl.when(jnp.logical_not(first_of_seq))
            def _(h_ref=h_ref, half=half):
                h_ref[0:CONV_HIST, :] = carry_ref[half, j]
        else:
            for s in range(tm // seg):
                top = h_row(s * seg)
                h_ref[top - lags:top, :] = state_ref[s, :, cols[half]]

    def up_conv(r0, half, xn):
        w_ref, h_ref = halves[half]
        taps = conv_ref[0:CONV_W, cols[half]]
        bias = conv_ref[CONV_W:CONV_W + 1, cols[half]]
        h = jnp.dot(xn, w_ref[...], preferred_element_type=F32)
        outs = []
        for p0 in range(0, chunk, piece):
            top = h_row(r0 + p0)
            hp = h[p0:p0 + piece]
            h_ref[top:top + piece, :] = hp
            out = taps[lags:lags + 1, :] * hp
            for tap in range(lags):
                lag = lags - tap
                out = out + taps[tap:tap + 1, :] * h_ref[top - lag:top - lag + piece, :]
            outs.append(out + bias)
            seg_end = r0 + p0 + piece
            if seg_end % seg == 0:
                tail_ref[seg_end // seg - 1, :, cols[half]] = hp[piece - lags:, :]
        return outs[0] if len(outs) == 1 else jnp.concatenate(outs, axis=0)

    def step(first, last):
        acts = []
        for r0 in range(0, tm, chunk):
            rows = slice(r0, r0 + chunk)
            if first:
                xn = _rmsnorm(x_ref[rows, :], g_ref[...]).astype(BF16)
                xn_ref[rows, :] = xn
            else:
                xn = xn_ref[rows, :]
            gate, val = up_conv(r0, 0, xn), up_conv(r0, 1, xn)
            acts.append((rows, (gate * jax.nn.sigmoid(gate) * val).astype(BF16)))
        for rows, act in acts:
            partial = jnp.dot(act, wd_ref[...], preferred_element_type=F32)
            if first:
                y_ref[rows, :] = partial
            elif last:
                y_ref[rows, :] = _rmsnorm(x_ref[rows, :] + (y_ref[rows, :] + partial), gf_ref[...])
            else:
                y_ref[rows, :] += partial
        if tiles_per_seq > 1:
            for half, (_, h_ref) in enumerate(halves):
                carry_ref[half, j] = h_ref[tm:tm + CONV_HIST, :]

    last_j = pl.num_programs(1) - 1
    pl.when(j == 0)(functools.partial(step, True, False))
    pl.when(jnp.logical_and(j > 0, j < last_j))(functools.partial(step, False, False))
    pl.when(j == last_j)(functools.partial(step, False, True))


def _conv_ffn(x, seq_len, g, w_up, conv_w, conv_b, w_down, state, g_final):
    n, d = x.shape
    f = w_down.shape[0]
    tm = min(512, n)
    tf = FFN_TILE
    seg = min(seq_len, tm)
    n_seg = tm // seg
    tiles_per_seq = seq_len // seg
    assert f % tf == 0 and n % tm == 0 and tm % seg == 0 and seq_len % seg == 0
    n_f = f // tf
    assert n_f >= 2, "first and last F tile must be different grid steps"
    h_rows = tm + CONV_HIST * n_seg
    conv = jnp.concatenate([conv_w, conv_b], axis=0)
    y, tails = pl.pallas_call(
        functools.partial(_ffn_kernel, tm=tm, seg=seg, chunk=min(FFN_ROW_CHUNK, tm), tiles_per_seq=tiles_per_seq),
        grid=(n // tm, n_f),
        in_specs=[
            pl.BlockSpec((tm, d), lambda i, j: (i, 0)),
            _const_spec((1, d)),
            pl.BlockSpec((d, tf), lambda i, j: (0, j)), pl.BlockSpec((d, tf), lambda i, j: (0, n_f + j)),
            _const_spec(conv.shape),
            pl.BlockSpec((tf, d), lambda i, j: (j, 0)),
            pl.BlockSpec((n_seg, CONV_W - 1, 2 * f), lambda i, j: (i // tiles_per_seq, 0, 0)),
            _const_spec((1, d)),
        ],
        out_specs=(
            pl.BlockSpec((tm, d), lambda i, j: (i, 0)),
            pl.BlockSpec((n_seg, CONV_W - 1, 2 * f), lambda i, j: (i, 0, 0)),
        ),
        out_shape=(
            jax.ShapeDtypeStruct((n, d), F32),
            jax.ShapeDtypeStruct((n // seg, CONV_W - 1, 2 * f), F32),
        ),
        scratch_shapes=[
            pltpu.VMEM((tm, d), BF16),
            pltpu.VMEM((h_rows, tf), F32),
            pltpu.VMEM((h_rows, tf), F32),
            pltpu.VMEM((2, n_f, CONV_HIST, tf), F32),
        ],
        compiler_params=_params(("arbitrary", "arbitrary")),
        name="conv_ffn",
    )(x, g, w_up, w_up, conv, w_down, state, g_final)
    return y, tails[tiles_per_seq - 1::tiles_per_seq]


LATE_WEIGHTS = ("pool_w", "w_branch_pool", "w_branch_sb", "w_out", "xa_wq", "xa_wo", "ffn_w_up", "ffn_w_down")


def _layer(x, pool_state, past_k, past_v, mk, mv, conv_state, p, late=None):
    b, l, d = x.shape
    n = b * l
    assert l >= POOL_HIST and l >= CONV_W - 1
    x = x.reshape(n, d)
    u, q, kf, kb, vf, vb, gates = _in_proj(x, p["norm_mix_g"], p["w_in"], p["b_gate"])
    width = u.shape[1]
    if past_k is None:
        past = 0
        hist = jnp.zeros((b, POOL_HIST, width), F32)
    else:
        past = past_k.shape[1]
        hist = jnp.pad(pool_state, ((0, 0), (POOL_HIST - pool_state.shape[1], 0), (0, 0)))
    by_seq = lambda a: a.reshape(b, l, width)
    side = () if late is not None else tuple(p[name].reshape(-1, p[name].shape[-1]) for name in LATE_WEIGHTS)
    y_sb, casted = _sb_attention(by_seq(q), by_seq(kb), by_seq(vb), past_k, past_v, side)
    if late is None:
        late = {name: c.reshape(p[name].shape) for name, c in zip(LATE_WEIGHTS, casted)}

    x2 = _mix(x, l, u, y_sb.reshape(n, width), gates, hist, mk, mv,
              late["pool_w"], p["pool_scale"], late["w_branch_pool"], late["w_branch_sb"], late["w_out"],
              p["norm_xa_g"], late["xa_wq"], late["xa_wo"], past)

    y, new_conv = _conv_ffn(x2, l, p["norm_ffn_g"], late["ffn_w_up"], p["ffn_conv_w"],
                            p["ffn_conv_b"], late["ffn_w_down"], conv_state, p["norm_final_g"])
    new_pool = by_seq(u)[:, l - (POOL_HIST - 1):, :]
    return (y.reshape(b, l, d), new_pool, kf.reshape(b, l, SB_HEADS, HEAD_DIM),
            vf.reshape(b, l, SB_HEADS, HEAD_DIM), new_conv), late


def kernel(x_prompt, x_sample, mem_prompt, state_pool, cache_sb_k, cache_sb_v, cache_mem_k, cache_mem_v, state_ffn_conv, norm_mix_g, w_in, b_gate, pool_w, pool_scale, w_branch_pool, w_branch_sb, w_out, norm_xa_g, norm_mem_g, xa_wq, xa_wkv, xa_wo, norm_ffn_g, ffn_w_up, ffn_conv_w, ffn_conv_b, ffn_w_down, norm_final_g):
    assert norm_mix_g.shape[0] == 1, "single-layer step"
    bp, lp, d = x_prompt.shape
    n_mem = mem_prompt.shape[1]
    xa_width = xa_wq.shape[2]
    row = lambda a: a.reshape(1, -1)
    p = {
        "norm_mix_g": row(norm_mix_g[0]), "w_in": w_in[0].astype(BF16), "b_gate": row(b_gate[0]),
        "pool_w": pool_w[0], "pool_scale": row(pool_scale[0]),
        "w_branch_pool": w_branch_pool[0], "w_branch_sb": w_branch_sb[0],
        "w_out": w_out[0], "norm_xa_g": row(norm_xa_g[0]),
        "xa_wq": xa_wq[0], "xa_wo": xa_wo[0],
        "norm_ffn_g": row(norm_ffn_g[0]), "ffn_w_up": ffn_w_up[0],
        "ffn_conv_w": ffn_conv_w[0], "ffn_conv_b": row(ffn_conv_b[0]),
        "ffn_w_down": ffn_w_down[0], "norm_final_g": row(norm_final_g),
    }

    mem_kv = _norm_matmul(mem_prompt.reshape(bp * n_mem, d), row(norm_mem_g[0]), xa_wkv[0].astype(BF16))
    mem_kv = mem_kv.reshape(bp, n_mem, 2 * xa_width)
    mk_p, mv_p = mem_kv[..., :xa_width], mem_kv[..., xa_width:]

    conv0 = jnp.zeros((bp, CONV_W - 1, ffn_w_up.shape[2]), F32)
    (y_p, pool_p, k_p, v_p, conv_p), late = _layer(x_prompt, None, None, None, mk_p.astype(BF16),
                                                   mv_p.astype(BF16), conv0, p)

    bs = x_sample.shape[0]
    mk_s = cache_mem_k[0].reshape(bs, n_mem, xa_width).astype(BF16)
    mv_s = cache_mem_v[0].reshape(bs, n_mem, xa_width).astype(BF16)
    (y_s, pool_s, k_s, v_s, conv_s), _ = _layer(x_sample, state_pool[0], cache_sb_k[0], cache_sb_v[0], mk_s, mv_s,
                                                state_ffn_conv[0], p, late)

    mem_shape = (1, bp, n_mem, XA_HEADS, HEAD_DIM)
    return (y_p, y_s,
            pool_p[None], k_p[None], v_p[None], mk_p.reshape(mem_shape), mv_p.reshape(mem_shape), conv_p[None],
            pool_s[None], k_s[None], v_s[None], conv_s[None])
```

```python
import functools

import jax
import jax.numpy as jnp
import numpy as np
from jax import lax
from jax.experimental import pallas as pl
from jax.experimental.pallas import tpu as pltpu

F32 = jnp.float32
BF16 = jnp.bfloat16

EPS = 1e-6
POOL_WINDOWS = (2, 4, 8, 16)
POOL_HIST = 16
SB_HEADS = 8
HEAD_DIM = 128
XA_HEADS = 4
CONV_W = 3
CONV_HIST = 8
LANES = 128
FFN_TILE = 512
ROW_CHUNK = 128
FFN_ROW_CHUNK = 256

SB_DEAD_LOG2 = -127.0
LOG2_E = 1.4426950408889634

VMEM_LIMIT = 56 * 1024 * 1024


def _rmsnorm(xf, g):
    ms = jnp.mean(xf * xf, axis=-1, keepdims=True)
    return xf * lax.rsqrt(ms + EPS) * g


def _const_spec(shape):
    zeros = (0,) * len(shape)
    return pl.BlockSpec(shape, lambda *_: zeros, pipeline_mode=pl.Buffered(1))


def _params(semantics):
    return pltpu.CompilerParams(dimension_semantics=semantics, vmem_limit_bytes=VMEM_LIMIT)


def _in_proj_kernel(x_ref, g_ref, w_ref, b_ref, u_ref, q_ref, kf_ref, kb_ref, vf_ref, vb_ref, gate_ref,
                    *, q_scale, chunk):
    j = pl.program_id(0)
    tm = x_ref.shape[0]
    half = u_ref.shape[1]

    def project(epilogue):
        for r0 in range(0, tm, chunk):
            rows = slice(r0, r0 + chunk)
            xn = _rmsnorm(x_ref[rows, :], g_ref[...]).astype(BF16)
            epilogue(rows, jnp.dot(xn, w_ref[...], preferred_element_type=F32))

    def put_uq(rows, acc):
        u_ref[rows, :] = acc[:, :half]
        q_ref[rows, :] = (acc[:, half:] * q_scale).astype(BF16)

    def put_heads(f32_ref, bf16_ref, rows, acc):
        f32_ref[rows, :, :] = acc.reshape(acc.shape[0], f32_ref.shape[1], f32_ref.shape[2])
        bf16_ref[rows, :] = acc.astype(BF16)

    def put_kv(rows, acc):
        put_heads(kf_ref, kb_ref, rows, acc[:, :half])
        put_heads(vf_ref, vb_ref, rows, acc[:, half:])

    def put_gate(rows, acc):
        gate_ref[rows, :] = jax.nn.sigmoid(acc + b_ref[...]).astype(BF16)

    pl.when(j == 0)(functools.partial(project, put_uq))
    pl.when(j == 1)(functools.partial(project, put_kv))
    pl.when(j >= 2)(functools.partial(project, put_gate))


def _in_proj(x, g, w_in, b_gate):
    n, d = x.shape
    half = d // 2
    n_col = w_in.shape[1] // d
    heads = half // HEAD_DIM
    tm = min(512, n)
    n_tiles = n // tm

    def rows_while(active):
        return lambda j, i: jnp.where(j < active, 0, jnp.where(j == active, i, n_tiles - 1))

    def flat(active):
        tile = rows_while(active)
        return pl.BlockSpec((tm, half), lambda j, i: (tile(j, i), 0))

    def by_head(active):
        tile = rows_while(active)
        return pl.BlockSpec((tm, heads, HEAD_DIM), lambda j, i: (tile(j, i), 0, 0))

    gate_tile = lambda j, i: (jnp.where(j < 2, 0, i), jnp.maximum(j - 2, 0))
    out_shape = (
        jax.ShapeDtypeStruct((n, half), F32),
        jax.ShapeDtypeStruct((n, half), BF16),
        jax.ShapeDtypeStruct((n, heads, HEAD_DIM), F32),
        jax.ShapeDtypeStruct((n, half), BF16),
        jax.ShapeDtypeStruct((n, heads, HEAD_DIM), F32),
        jax.ShapeDtypeStruct((n, half), BF16),
        jax.ShapeDtypeStruct((n, 2 * d), BF16),
    )
    out_specs = (flat(0), flat(0), by_head(1), flat(1), by_head(1), flat(1), pl.BlockSpec((tm, d), gate_tile))
    return pl.pallas_call(
        functools.partial(_in_proj_kernel, q_scale=LOG2_E / np.sqrt(HEAD_DIM), chunk=min(ROW_CHUNK, tm)),
        grid=(n_col, n_tiles),
        in_specs=[
            pl.BlockSpec((tm, d), lambda j, i: (i, 0)),
            pl.BlockSpec((1, d), lambda j, i: (0, 0)),
            pl.BlockSpec((d, d), lambda j, i: (0, j)),
            pl.BlockSpec((1, d), lambda j, i: (0, jnp.maximum(j - 2, 0))),
        ],
        out_specs=out_specs,
        out_shape=out_shape,
        compiler_params=_params(("arbitrary", "arbitrary")),
        name="in_proj",
    )(x, g, w_in, b_gate)


def _norm_matmul_kernel(x_ref, g_ref, w_ref, o_ref):
    xn = _rmsnorm(x_ref[...], g_ref[...]).astype(BF16)
    o_ref[...] = jnp.dot(xn, w_ref[...], preferred_element_type=F32)


def _norm_matmul(x, g, w):
    n, d = x.shape
    tm = min(256, n)
    return pl.pallas_call(
        _norm_matmul_kernel,
        grid=(n // tm,),
        in_specs=[pl.BlockSpec((tm, d), lambda i: (i, 0)), _const_spec((1, d)), _const_spec(w.shape)],
        out_specs=pl.BlockSpec((tm, w.shape[1]), lambda i: (i, 0)),
        out_shape=jax.ShapeDtypeStruct((n, w.shape[1]), F32),
        compiler_params=_params(("arbitrary",)),
        name="mem_kv",
    )(x, g, w)


def _sb_kernel(*refs, blk, n_sub, n_heads, key_block_offset, has_past, side_blocks):
    n_side = len(side_blocks)
    n_in = len(refs) - 1 - n_side
    side_in, side_out = refs[n_in - n_side:n_in], refs[n_in + 1:]
    o_ref = refs[n_in]
    if has_past:
        q_ref, k_ref, v_ref, past_k_ref, past_v_ref, tri1_ref, tri2_ref = refs[:n_in - n_side]
    else:
        q_ref, k_ref, v_ref, tri1_ref, tri2_ref = refs[:n_in - n_side]
    step = pl.program_id(2)

    linear_step = (pl.program_id(0) * pl.num_programs(1) + pl.program_id(1)) * pl.num_programs(2) + step
    for src_ref, dst_ref, n_blocks in zip(side_in, side_out, side_blocks):
        @pl.when(linear_step < n_blocks)
        def _(src_ref=src_ref, dst_ref=dst_ref):
            dst_ref[...] = src_ref[...].astype(BF16)

    first_span = 2
    wide = first_span * blk
    col_minus_row = (lax.broadcasted_iota(jnp.int32, (blk, wide), 1)
                     - lax.broadcasted_iota(jnp.int32, (blk, wide), 0))
    chains = [(sub, h) for sub in range(n_sub) for h in range(n_heads)]
    rows = lambda sub: slice(sub * blk, (sub + 1) * blk)
    lanes = lambda h: slice(h * HEAD_DIM, (h + 1) * HEAD_DIM)
    qs = [q_ref[0, rows(sub), lanes(h)] for sub, h in chains]
    first_diag = step * n_sub + key_block_offset

    def visit(newest, span, accs, runs, first):
        width = span * blk
        tri = (tri1_ref if span == 1 else tri2_ref)[...]
        log_nots, log_betas, vs, masks = [], [], [], []
        runs = list(runs)
        for c, (sub, h) in enumerate(chains):
            kb_new = first_diag + sub - newest
            kb_old = kb_new - (span - 1)
            start_blk = jnp.maximum(kb_old, 0)
            if first:
                masks.append(col_minus_row[:, :width] < (kb_new - start_blk) * blk)
            else:
                runs[c] = jnp.where(kb_old >= 0, runs[c], -1e30)
            start = pl.multiple_of(start_blk * blk, blk)
            if not has_past:
                k = k_ref[0, pl.ds(start, width), lanes(h)]
                v = v_ref[0, pl.ds(start, width), lanes(h)]
            elif first:
                newest_past = slice((key_block_offset - 1) * blk, key_block_offset * blk)
                k = jnp.concatenate([past_k_ref[0, newest_past, h, :].astype(BF16), k_ref[0, :, lanes(h)]], axis=0)
                v = jnp.concatenate([past_v_ref[0, newest_past, h, :].astype(BF16), v_ref[0, :, lanes(h)]], axis=0)
            else:
                k = past_k_ref[0, pl.ds(start, width), h, :].astype(BF16)
                v = past_v_ref[0, pl.ds(start, width), h, :].astype(BF16)
            vs.append(v)
            z = lax.dot_general(qs[c], k, (((1,), (1,)), ((), ())), preferred_element_type=F32)
            t = jnp.log2(1.0 + jnp.exp2(-jnp.abs(z)))
            log_not = -(jnp.maximum(z, 0.0) + t)
            if first:
                log_not = jnp.where(masks[c], log_not, 0.0)
            log_nots.append(log_not)
            log_betas.append(jnp.minimum(z, 0.0) - t)
        stacked = jnp.concatenate(log_nots, axis=0)
        hi = stacked.astype(BF16)
        lo = (stacked - hi.astype(F32)).astype(BF16)
        sums = (jnp.dot(hi, tri, preferred_element_type=F32)
                + jnp.dot(lo, tri, preferred_element_type=F32))
        new_accs, new_runs = [], []
        for c in range(len(chains)):
            part = sums[c * blk:(c + 1) * blk]
            between = part[:, :width]
            total = part[:, width:]
            if first:
                a = jnp.where(masks[c], jnp.exp2(log_betas[c] + between), 0.0)
            else:
                a = jnp.exp2(log_betas[c] + between + runs[c][:, :width])
            new_accs.append(accs[c] + jnp.dot(a.astype(BF16), vs[c], preferred_element_type=F32))
            new_runs.append(runs[c] + total)
        return tuple(new_accs), tuple(new_runs)

    zeros = tuple(jnp.zeros((blk, LANES), F32) for _ in chains)
    accs, runs = visit(0, first_span, zeros, zeros, True)

    def alive(carry):
        newest, _, runs = carry
        live = functools.reduce(jnp.maximum, runs)
        return jnp.logical_and(first_diag + n_sub - 1 - newest >= 0, jnp.max(live) > SB_DEAD_LOG2)

    def older(carry):
        newest, accs, runs = carry
        accs, runs = visit(newest, 1, accs, runs, False)
        return newest + 1, accs, runs

    _, accs, _ = lax.while_loop(alive, older, (jnp.int32(first_span), accs, runs))
    for c, (sub, h) in enumerate(chains):
        o_ref[0, rows(sub), lanes(h)] = accs[c].astype(BF16)


def _suffix_sum_matrix(width):
    jj = np.arange(width)
    return jnp.asarray(np.concatenate([jj[:, None] > jj[None, :], np.ones((width, LANES), bool)], axis=1), BF16)


def _row_blocks(rows, max_blocks):
    units = rows // 16
    assert units * 16 == rows
    return max(nb for nb in range(1, min(units, max_blocks) + 1) if units % nb == 0)


def _sb_attention(q, k, v, past_k=None, past_v=None, side=()):
    b, lq, width = q.shape
    has_past = past_k is not None
    past = past_k.shape[1] if has_past else 0
    lk = past + lq
    blk = min(128, lq)
    assert lq % blk == 0 and past % blk == 0 and lk >= 2 * blk
    assert not has_past or lq == blk, "with a cache the new keys must form one block"
    n_q = lq // blk
    n_chains = 8
    n_sub = min(n_chains, n_q)
    n_heads = min(SB_HEADS, n_chains // n_sub)
    assert n_q % n_sub == 0 and SB_HEADS % n_heads == 0
    tri1, tri2 = _suffix_sum_matrix(blk), _suffix_sum_matrix(2 * blk)
    lane_w = n_heads * HEAD_DIM
    keys = pl.BlockSpec((1, lq, lane_w), lambda bi, h, i: (bi, 0, h))
    cache = [pl.BlockSpec((1, past, n_heads, HEAD_DIM), lambda bi, h, i: (bi, 0, h, 0))] * 2 if has_past else []
    grid = (b, SB_HEADS // n_heads, n_q // n_sub)
    n_steps = grid[0] * grid[1] * grid[2]
    side_blocks = tuple(_row_blocks(a.shape[0], n_steps) for a in side)

    def side_spec(a, n_blocks):
        return pl.BlockSpec((a.shape[0] // n_blocks, a.shape[1]),
                            lambda bi, h, i: (jnp.minimum((bi * grid[1] + h) * grid[2] + i, n_blocks - 1), 0))

    side_specs = [side_spec(a, nb) for a, nb in zip(side, side_blocks)]
    out = pl.pallas_call(
        functools.partial(_sb_kernel, blk=blk, n_sub=n_sub, n_heads=n_heads,
                          key_block_offset=past // blk, has_past=has_past, side_blocks=side_blocks),
        grid=grid,
        in_specs=[
            pl.BlockSpec((1, n_sub * blk, lane_w), lambda bi, h, i: (bi, i, h)),
            keys, keys, *cache,
            _const_spec(tri1.shape), _const_spec(tri2.shape),
            *side_specs,
        ],
        out_specs=(pl.BlockSpec((1, n_sub * blk, lane_w), lambda bi, h, i: (bi, i, h)), *side_specs),
        out_shape=(jax.ShapeDtypeStruct(q.shape, BF16), *(jax.ShapeDtypeStruct(a.shape, BF16) for a in side)),
        compiler_params=_params(("arbitrary", "arbitrary", "arbitrary")),
        name="sb_attn",
    )(q, k, v, *((past_k, past_v) if has_past else ()), tri1, tri2, *side)
    return out[0], out[1:]


def _mix_kernel(x_ref, u_ref, ysb_ref, gp_ref, gs_ref, hist_ref, mk_ref, mv_ref,
                pool_w_ref, pool_scale_ref, wbp_ref, wbs_ref, wout_ref, gxa_ref, wq_ref, wo_ref,
                o_ref, ubuf_ref, *, tm, seg, chunk, tiles_per_seq, pos0):
    i = pl.program_id(0)
    group = u_ref.shape[1] // len(POOL_WINDOWS)
    piece = min(chunk, seg)
    u_row = lambda r: r + POOL_HIST * (r // seg + 1)

    if tiles_per_seq > 1:
        tile_in_seq = i % tiles_per_seq

        @pl.when(tile_in_seq == 0)
        def _():
            ubuf_ref[0:POOL_HIST, :] = hist_ref[0]

        seq_row0 = tile_in_seq * tm
    else:
        for s in range(tm // seg):
            top = u_row(s * seg)
            ubuf_ref[top - POOL_HIST:top, :] = hist_ref[s]
        seq_row0 = 0

    for r0 in range(0, tm, chunk):
        rows = slice(r0, r0 + chunk)
        pieces = range(r0, r0 + chunk, piece)
        for r in pieces:
            ubuf_ref[u_row(r):u_row(r) + piece, :] = u_ref[r:r + piece, :]
        pooled = []
        for g, w in enumerate(POOL_WINDOWS):
            cols = slice(g * group, (g + 1) * group)
            deltas = []
            for r in pieces:
                top = u_row(r)
                u = u_ref[r:r + piece, cols]
                partial = ubuf_ref[top - POOL_HIST:top + piece, cols]
                span = 1
                while span < w:
                    partial = partial + pltpu.roll(partial, span, axis=0)
                    span *= 2
                window_sum = partial[POOL_HIST:]
                pos = pos0 + seq_row0 + r % seg + lax.broadcasted_iota(jnp.int32, (piece, 1), 0)
                count = jnp.minimum(w, pos + 1).astype(F32)
                deltas.append((window_sum / count - u).astype(BF16))
            delta = deltas[0] if len(deltas) == 1 else jnp.concatenate(deltas, axis=0)
            pooled.append(jnp.dot(delta, pool_w_ref[g], preferred_element_type=F32))
        y_pool = (jnp.concatenate(pooled, axis=-1) * pool_scale_ref[...]).astype(BF16)

        branch_pool = jnp.dot(y_pool, wbp_ref[...], preferred_element_type=F32)
        branch_sb = jnp.dot(ysb_ref[rows, :], wbs_ref[...], preferred_element_type=F32)
        merged = gp_ref[rows, :].astype(F32) * branch_pool + gs_ref[rows, :].astype(F32) * branch_sb
        x1 = x_ref[rows, :] + jnp.dot(merged.astype(BF16), wout_ref[...], preferred_element_type=F32)

        xn = _rmsnorm(x1, gxa_ref[...]).astype(BF16)
        q = (jnp.dot(xn, wq_ref[...], preferred_element_type=F32) * (1.0 / np.sqrt(HEAD_DIM))).astype(BF16)
        attended = []
        for r in pieces:
            s = r // seg
            heads = []
            for h in range(XA_HEADS):
                cols = slice(h * HEAD_DIM, (h + 1) * HEAD_DIM)
                sc = lax.dot_general(q[r - r0:r - r0 + piece, cols], mk_ref[s, :, cols], (((1,), (1,)), ((), ())),
                                     preferred_element_type=F32)
                p = jnp.exp(sc - jnp.max(sc, axis=-1, keepdims=True))
                p = p / jnp.sum(p, axis=-1, keepdims=True)
                heads.append(jnp.dot(p.astype(BF16), mv_ref[s, :, cols], preferred_element_type=F32))
            attended.append(jnp.concatenate(heads, axis=-1).astype(BF16))
        attn = attended[0] if len(attended) == 1 else jnp.concatenate(attended, axis=0)
        o_ref[rows, :] = x1 + jnp.dot(attn, wo_ref[...], preferred_element_type=F32)

    if tiles_per_seq > 1:
        ubuf_ref[0:POOL_HIST, :] = ubuf_ref[tm:tm + POOL_HIST, :]


def _mix(x, seq_len, u, ysb, gates, hist, mk, mv, pool_w, pool_scale, wbp, wbs, wout, gxa, wq, wo, pos0):
    n, d = x.shape
    tm = min(256, n)
    seg = min(seq_len, tm)
    n_seg = tm // seg
    tiles_per_seq = seq_len // seg
    assert n % tm == 0 and tm % seg == 0 and seq_len % seg == 0
    tile = lambda i: (i, 0)
    per_seq = lambda i: (i // tiles_per_seq, 0, 0)
    return pl.pallas_call(
        functools.partial(_mix_kernel, tm=tm, seg=seg, chunk=tm, tiles_per_seq=tiles_per_seq,
                          pos0=pos0),
        grid=(n // tm,),
        in_specs=[
            pl.BlockSpec((tm, d), tile),
            pl.BlockSpec((tm, u.shape[1]), tile),
            pl.BlockSpec((tm, ysb.shape[1]), tile),
            pl.BlockSpec((tm, d), tile),
            pl.BlockSpec((tm, d), lambda i: (i, 1)),
            pl.BlockSpec((n_seg,) + hist.shape[1:], per_seq),
            pl.BlockSpec((n_seg,) + mk.shape[1:], per_seq),
            pl.BlockSpec((n_seg,) + mv.shape[1:], per_seq),
            _const_spec(pool_w.shape), _const_spec(pool_scale.shape), _const_spec(wbp.shape),
            _const_spec(wbs.shape), _const_spec(wout.shape), _const_spec(gxa.shape),
            _const_spec(wq.shape), _const_spec(wo.shape),
        ],
        out_specs=pl.BlockSpec((tm, d), tile),
        out_shape=jax.ShapeDtypeStruct(x.shape, F32),
        scratch_shapes=[pltpu.VMEM((tm + POOL_HIST * n_seg, u.shape[1]), F32)],
        compiler_params=_params(("arbitrary",)),
        name="mix",
    )(x, u, ysb, gates, gates, hist, mk, mv, pool_w, pool_scale, wbp, wbs, wout, gxa, wq, wo)


def _ffn_kernel(x_ref, g_ref, wg_ref, wv_ref, conv_ref, wd_ref, state_ref, gf_ref, y_ref, tail_ref,
                xn_ref, acc_ref, hg_ref, hv_ref, carry_ref, *, tm, seg, chunk, tiles_per_seq):
    j = pl.program_id(1)
    t = pl.program_id(2)
    i = pl.program_id(0) * pl.num_programs(2) + t
    n_seg = tm // seg
    piece = min(chunk, seg)
    lags = CONV_W - 1
    tf = wg_ref.shape[1]
    n_f = pl.num_programs(1)
    cols = [pl.ds(pl.multiple_of((half * n_f + j) * tf, tf), tf) for half in range(2)]
    halves = ((wg_ref, hg_ref), (wv_ref, hv_ref))
    h_row = lambda r: r + CONV_HIST * (r // seg + 1)

    for half, (_, h_ref) in enumerate(halves):
        if tiles_per_seq > 1:
            first_of_seq = (i % tiles_per_seq) == 0

            @pl.when(first_of_seq)
            def _(h_ref=h_ref, half=half):
                h_ref[CONV_HIST - lags:CONV_HIST, :] = state_ref[0, :, cols[half]]

            @pl.when(jnp.logical_not(first_of_seq))
            def _(h_ref=h_ref, half=half):
                h_ref[0:CONV_HIST, :] = carry_ref[half, j]
        else:
            for s in range(n_seg):
                top = h_row(s * seg)
                h_ref[top - lags:top, :] = state_ref[s, :, cols[half]]

    def up_conv(r0, half, xn):
        w_ref, h_ref = halves[half]
        taps = conv_ref[0:CONV_W, cols[half]]
        bias = conv_ref[CONV_W:CONV_W + 1, cols[half]]
        h = jnp.dot(xn, w_ref[...], preferred_element_type=F32)
        outs = []
        for p0 in range(0, chunk, piece):
            top = h_row(r0 + p0)
            hp = h[p0:p0 + piece]
            h_ref[top:top + piece, :] = hp
            out = taps[lags:lags + 1, :] * hp
            for tap in range(lags):
                lag = lags - tap
                out = out + taps[tap:tap + 1, :] * h_ref[top - lag:top - lag + piece, :]
            outs.append(out + bias)
            seg_end = r0 + p0 + piece
            if seg_end % seg == 0:
                tail_ref[t * n_seg + seg_end // seg - 1, :, cols[half]] = hp[piece - lags:, :]
        return outs[0] if len(outs) == 1 else jnp.concatenate(outs, axis=0)

    def step(first, last):
        acts = []
        for r0 in range(0, tm, chunk):
            rows = slice(r0, r0 + chunk)
            if first:
                xn = _rmsnorm(x_ref[rows, :], g_ref[...]).astype(BF16)
                xn_ref[t, rows, :] = xn
            else:
                xn = xn_ref[t, rows, :]
            gate, val = up_conv(r0, 0, xn), up_conv(r0, 1, xn)
            acts.append((rows, (gate * jax.nn.sigmoid(gate) * val).astype(BF16)))
        for rows, act in acts:
            partial = jnp.dot(act, wd_ref[...], preferred_element_type=F32)
            if first:
                acc_ref[t, rows, :] = partial
            elif last:
                y_ref[rows, :] = _rmsnorm(x_ref[rows, :] + (acc_ref[t, rows, :] + partial), gf_ref[...])
            else:
                acc_ref[t, rows, :] += partial
        if tiles_per_seq > 1:
            for half, (_, h_ref) in enumerate(halves):
                carry_ref[half, j] = h_ref[tm:tm + CONV_HIST, :]

    last_j = pl.num_programs(1) - 1
    pl.when(j == 0)(functools.partial(step, True, False))
    pl.when(jnp.logical_and(j > 0, j < last_j))(functools.partial(step, False, False))
    pl.when(j == last_j)(functools.partial(step, False, True))


def _conv_ffn(x, seq_len, g, w_up, conv_w, conv_b, w_down, state, g_final):
    n, d = x.shape
    f = w_down.shape[0]
    tm = min(512, n)
    tf = FFN_TILE
    seg = min(seq_len, tm)
    n_seg = tm // seg
    tiles_per_seq = seq_len // seg
    assert f % tf == 0 and n % tm == 0 and tm % seg == 0 and seq_len % seg == 0
    n_f = f // tf
    assert n_f >= 2, "first and last F tile must be different grid steps"
    n_tiles = n // tm
    group = 2 if n_tiles % 2 == 0 else 1
    h_rows = tm + CONV_HIST * n_seg
    conv = jnp.concatenate([conv_w, conv_b], axis=0)
    tile = lambda gi, j, t: gi * group + t

    def x_tile(gi, j, t):
        edge = jnp.logical_or(j == 0, j == n_f - 1)
        return jnp.where(edge, tile(gi, j, t), gi * group + group - 1), 0

    def y_tile(gi, j, t):
        return jnp.where(j == n_f - 1, tile(gi, j, t), jnp.maximum(gi * group - 1, 0)), 0

    y, tails = pl.pallas_call(
        functools.partial(_ffn_kernel, tm=tm, seg=seg, chunk=min(FFN_ROW_CHUNK, tm), tiles_per_seq=tiles_per_seq),
        grid=(n_tiles // group, n_f, group),
        in_specs=[
            pl.BlockSpec((tm, d), x_tile),
            _const_spec((1, d)),
            pl.BlockSpec((d, tf), lambda gi, j, t: (0, j)), pl.BlockSpec((d, tf), lambda gi, j, t: (0, n_f + j)),
            _const_spec(conv.shape),
            pl.BlockSpec((tf, d), lambda gi, j, t: (j, 0)),
            pl.BlockSpec((n_seg, CONV_W - 1, 2 * f), lambda gi, j, t: (tile(gi, j, t) // tiles_per_seq, 0, 0)),
            _const_spec((1, d)),
        ],
        out_specs=(
            pl.BlockSpec((tm, d), y_tile),
            pl.BlockSpec((group * n_seg, CONV_W - 1, 2 * f), lambda gi, j, t: (gi, 0, 0)),
        ),
        out_shape=(
            jax.ShapeDtypeStruct((n, d), F32),
            jax.ShapeDtypeStruct((n // seg, CONV_W - 1, 2 * f), F32),
        ),
        scratch_shapes=[
            pltpu.VMEM((group, tm, d), BF16),
            pltpu.VMEM((group, tm, d), F32),
            pltpu.VMEM((h_rows, tf), F32),
            pltpu.VMEM((h_rows, tf), F32),
            pltpu.VMEM((2, n_f, CONV_HIST, tf), F32),
        ],
        compiler_params=_params(("arbitrary", "arbitrary", "arbitrary")),
        name="conv_ffn",
    )(x, g, w_up, w_up, conv, w_down, state, g_final)
    return y, tails[tiles_per_seq - 1::tiles_per_seq]


LATE_WEIGHTS = ("pool_w", "w_branch_pool", "w_branch_sb", "w_out", "xa_wq", "xa_wo", "ffn_w_up", "ffn_w_down")


def _layer(x, pool_state, past_k, past_v, mk, mv, conv_state, p, late=None):
    b, l, d = x.shape
    n = b * l
    assert l >= POOL_HIST and l >= CONV_W - 1
    x = x.reshape(n, d)
    u, q, kf, kb, vf, vb, gates = _in_proj(x, p["norm_mix_g"], p["w_in"], p["b_gate"])
    width = u.shape[1]
    if past_k is None:
        past = 0
        hist = jnp.zeros((b, POOL_HIST, width), F32)
    else:
        past = past_k.shape[1]
        hist = jnp.pad(pool_state, ((0, 0), (POOL_HIST - pool_state.shape[1], 0), (0, 0)))
    by_seq = lambda a: a.reshape(b, l, width)
    side = () if late is not None else tuple(p[name].reshape(-1, p[name].shape[-1]) for name in LATE_WEIGHTS)
    y_sb, casted = _sb_attention(by_seq(q), by_seq(kb), by_seq(vb), past_k, past_v, side)
    if late is None:
        late = {name: c.reshape(p[name].shape) for name, c in zip(LATE_WEIGHTS, casted)}

    x2 = _mix(x, l, u, y_sb.reshape(n, width), gates, hist, mk, mv,
              late["pool_w"], p["pool_scale"], late["w_branch_pool"], late["w_branch_sb"], late["w_out"],
              p["norm_xa_g"], late["xa_wq"], late["xa_wo"], past)

    y, new_conv = _conv_ffn(x2, l, p["norm_ffn_g"], late["ffn_w_up"], p["ffn_conv_w"],
                            p["ffn_conv_b"], late["ffn_w_down"], conv_state, p["norm_final_g"])
    new_pool = by_seq(u)[:, l - (POOL_HIST - 1):, :]
    return (y.reshape(b, l, d), new_pool, kf.reshape(b, l, SB_HEADS, HEAD_DIM),
            vf.reshape(b, l, SB_HEADS, HEAD_DIM), new_conv), late


def kernel(x_prompt, x_sample, mem_prompt, state_pool, cache_sb_k, cache_sb_v, cache_mem_k, cache_mem_v, state_ffn_conv, norm_mix_g, w_in, b_gate, pool_w, pool_scale, w_branch_pool, w_branch_sb, w_out, norm_xa_g, norm_mem_g, xa_wq, xa_wkv, xa_wo, norm_ffn_g, ffn_w_up, ffn_conv_w, ffn_conv_b, ffn_w_down, norm_final_g):
    assert norm_mix_g.shape[0] == 1, "single-layer step"
    bp, lp, d = x_prompt.shape
    n_mem = mem_prompt.shape[1]
    xa_width = xa_wq.shape[2]
    row = lambda a: a.reshape(1, -1)
    p = {
        "norm_mix_g": row(norm_mix_g[0]), "w_in": w_in[0].astype(BF16), "b_gate": row(b_gate[0]),
        "pool_w": pool_w[0], "pool_scale": row(pool_scale[0]),
        "w_branch_pool": w_branch_pool[0], "w_branch_sb": w_branch_sb[0],
        "w_out": w_out[0], "norm_xa_g": row(norm_xa_g[0]),
        "xa_wq": xa_wq[0], "xa_wo": xa_wo[0],
        "norm_ffn_g": row(norm_ffn_g[0]), "ffn_w_up": ffn_w_up[0],
        "ffn_conv_w": ffn_conv_w[0], "ffn_conv_b": row(ffn_conv_b[0]),
        "ffn_w_down": ffn_w_down[0], "norm_final_g": row(norm_final_g),
    }

    mem_kv = _norm_matmul(mem_prompt.reshape(bp * n_mem, d), row(norm_mem_g[0]), xa_wkv[0].astype(BF16))
    mem_kv = mem_kv.reshape(bp, n_mem, 2 * xa_width)
    mk_p, mv_p = mem_kv[..., :xa_width], mem_kv[..., xa_width:]

    conv0 = jnp.zeros((bp, CONV_W - 1, ffn_w_up.shape[2]), F32)
    (y_p, pool_p, k_p, v_p, conv_p), late = _layer(x_prompt, None, None, None, mk_p.astype(BF16),
                                                   mv_p.astype(BF16), conv0, p)

    bs = x_sample.shape[0]
    mk_s = cache_mem_k[0].reshape(bs, n_mem, xa_width).astype(BF16)
    mv_s = cache_mem_v[0].reshape(bs, n_mem, xa_width).astype(BF16)
    (y_s, pool_s, k_s, v_s, conv_s), _ = _layer(x_sample, state_pool[0], cache_sb_k[0], cache_sb_v[0], mk_s, mv_s,
                                                state_ffn_conv[0], p, late)

    mem_shape = (1, bp, n_mem, XA_HEADS, HEAD_DIM)
    return (y_p, y_s,
            pool_p[None], k_p[None], v_p[None], mk_p.reshape(mem_shape), mv_p.reshape(mem_shape), conv_p[None],
            pool_s[None], k_s[None], v_s[None], conv_s[None])
```

```python
import functools

import jax
import jax.numpy as jnp
import numpy as np
from jax import lax
from jax.experimental import pallas as pl
from jax.experimental.pallas import tpu as pltpu

F32 = jnp.float32
BF16 = jnp.bfloat16

EPS = 1e-6
POOL_WINDOWS = (2, 4, 8, 16)
POOL_HIST = 16
SB_HEADS = 8
HEAD_DIM = 128
XA_HEADS = 4
CONV_W = 3
CONV_HIST = 8
LANES = 128
FFN_TILE = 512
FFN_TILES_PER_STEP = 2
ROW_CHUNK = 128
FFN_ROW_CHUNK = 256

SB_DEAD_LOG2 = -127.0
LOG2_E = 1.4426950408889634

VMEM_LIMIT = 56 * 1024 * 1024


def _rmsnorm(xf, g):
    ms = jnp.mean(xf * xf, axis=-1, keepdims=True)
    return xf * lax.rsqrt(ms + EPS) * g


def _const_spec(shape):
    zeros = (0,) * len(shape)
    return pl.BlockSpec(shape, lambda *_: zeros, pipeline_mode=pl.Buffered(1))


def _params(semantics):
    return pltpu.CompilerParams(dimension_semantics=semantics, vmem_limit_bytes=VMEM_LIMIT)


def _in_proj_kernel(x_ref, g_ref, w_ref, b_ref, u_ref, q_ref, kf_ref, kb_ref, vf_ref, vb_ref, gate_ref,
                    *, q_scale, chunk):
    j = pl.program_id(0)
    tm = x_ref.shape[0]
    half = u_ref.shape[1]

    def project(epilogue):
        for r0 in range(0, tm, chunk):
            rows = slice(r0, r0 + chunk)
            xn = _rmsnorm(x_ref[rows, :], g_ref[...]).astype(BF16)
            epilogue(rows, jnp.dot(xn, w_ref[...], preferred_element_type=F32))

    def put_uq(rows, acc):
        u_ref[rows, :] = acc[:, :half]
        q_ref[rows, :] = (acc[:, half:] * q_scale).astype(BF16)

    def put_heads(f32_ref, bf16_ref, rows, acc):
        f32_ref[rows, :, :] = acc.reshape(acc.shape[0], f32_ref.shape[1], f32_ref.shape[2])
        bf16_ref[rows, :] = acc.astype(BF16)

    def put_kv(rows, acc):
        put_heads(kf_ref, kb_ref, rows, acc[:, :half])
        put_heads(vf_ref, vb_ref, rows, acc[:, half:])

    def put_gate(rows, acc):
        gate_ref[rows, :] = jax.nn.sigmoid(acc + b_ref[...]).astype(BF16)

    pl.when(j == 0)(functools.partial(project, put_uq))
    pl.when(j == 1)(functools.partial(project, put_kv))
    pl.when(j >= 2)(functools.partial(project, put_gate))


def _in_proj(x, g, w_in, b_gate):
    n, d = x.shape
    half = d // 2
    n_col = w_in.shape[1] // d
    heads = half // HEAD_DIM
    tm = min(512, n)
    n_tiles = n // tm

    def rows_while(active):
        return lambda j, i: jnp.where(j < active, 0, jnp.where(j == active, i, n_tiles - 1))

    def flat(active):
        tile = rows_while(active)
        return pl.BlockSpec((tm, half), lambda j, i: (tile(j, i), 0))

    def by_head(active):
        tile = rows_while(active)
        return pl.BlockSpec((tm, heads, HEAD_DIM), lambda j, i: (tile(j, i), 0, 0))

    gate_tile = lambda j, i: (jnp.where(j < 2, 0, i), jnp.maximum(j - 2, 0))
    out_shape = (
        jax.ShapeDtypeStruct((n, half), F32),
        jax.ShapeDtypeStruct((n, half), BF16),
        jax.ShapeDtypeStruct((n, heads, HEAD_DIM), F32),
        jax.ShapeDtypeStruct((n, half), BF16),
        jax.ShapeDtypeStruct((n, heads, HEAD_DIM), F32),
        jax.ShapeDtypeStruct((n, half), BF16),
        jax.ShapeDtypeStruct((n, 2 * d), BF16),
    )
    out_specs = (flat(0), flat(0), by_head(1), flat(1), by_head(1), flat(1), pl.BlockSpec((tm, d), gate_tile))
    return pl.pallas_call(
        functools.partial(_in_proj_kernel, q_scale=LOG2_E / np.sqrt(HEAD_DIM), chunk=min(ROW_CHUNK, tm)),
        grid=(n_col, n_tiles),
        in_specs=[
            pl.BlockSpec((tm, d), lambda j, i: (i, 0)),
            pl.BlockSpec((1, d), lambda j, i: (0, 0)),
            pl.BlockSpec((d, d), lambda j, i: (0, j)),
            pl.BlockSpec((1, d), lambda j, i: (0, jnp.maximum(j - 2, 0))),
        ],
        out_specs=out_specs,
        out_shape=out_shape,
        compiler_params=_params(("arbitrary", "arbitrary")),
        name="in_proj",
    )(x, g, w_in, b_gate)


def _norm_matmul_kernel(x_ref, g_ref, w_ref, o_ref):
    xn = _rmsnorm(x_ref[...], g_ref[...]).astype(BF16)
    o_ref[...] = jnp.dot(xn, w_ref[...], preferred_element_type=F32)


def _norm_matmul(x, g, w):
    n, d = x.shape
    tm = min(256, n)
    return pl.pallas_call(
        _norm_matmul_kernel,
        grid=(n // tm,),
        in_specs=[pl.BlockSpec((tm, d), lambda i: (i, 0)), _const_spec((1, d)), _const_spec(w.shape)],
        out_specs=pl.BlockSpec((tm, w.shape[1]), lambda i: (i, 0)),
        out_shape=jax.ShapeDtypeStruct((n, w.shape[1]), F32),
        compiler_params=_params(("arbitrary",)),
        name="mem_kv",
    )(x, g, w)


def _sb_kernel(*refs, blk, n_sub, n_heads, key_block_offset, has_past, side_blocks):
    n_side = len(side_blocks)
    n_in = len(refs) - 1 - n_side
    side_in, side_out = refs[n_in - n_side:n_in], refs[n_in + 1:]
    o_ref = refs[n_in]
    if has_past:
        q_ref, k_ref, v_ref, past_k_ref, past_v_ref, tri1_ref, tri2_ref = refs[:n_in - n_side]
    else:
        q_ref, k_ref, v_ref, tri1_ref, tri2_ref = refs[:n_in - n_side]
    step = pl.program_id(2)

    linear_step = (pl.program_id(0) * pl.num_programs(1) + pl.program_id(1)) * pl.num_programs(2) + step
    for src_ref, dst_ref, n_blocks in zip(side_in, side_out, side_blocks):
        @pl.when(linear_step < n_blocks)
        def _(src_ref=src_ref, dst_ref=dst_ref):
            dst_ref[...] = src_ref[...].astype(BF16)

    first_span = 2
    wide = first_span * blk
    col_minus_row = (lax.broadcasted_iota(jnp.int32, (blk, wide), 1)
                     - lax.broadcasted_iota(jnp.int32, (blk, wide), 0))
    chains = [(sub, h) for sub in range(n_sub) for h in range(n_heads)]
    rows = lambda sub: slice(sub * blk, (sub + 1) * blk)
    lanes = lambda h: slice(h * HEAD_DIM, (h + 1) * HEAD_DIM)
    qs = [q_ref[0, rows(sub), lanes(h)] for sub, h in chains]
    first_diag = step * n_sub + key_block_offset

    def visit(newest, span, accs, runs, first):
        width = span * blk
        tri = (tri1_ref if span == 1 else tri2_ref)[...]
        log_nots, log_betas, vs, masks = [], [], [], []
        runs = list(runs)
        for c, (sub, h) in enumerate(chains):
            kb_new = first_diag + sub - newest
            kb_old = kb_new - (span - 1)
            start_blk = jnp.maximum(kb_old, 0)
            if first:
                masks.append(col_minus_row[:, :width] < (kb_new - start_blk) * blk)
            else:
                runs[c] = jnp.where(kb_old >= 0, runs[c], -1e30)
            start = pl.multiple_of(start_blk * blk, blk)
            if not has_past:
                k = k_ref[0, pl.ds(start, width), lanes(h)]
                v = v_ref[0, pl.ds(start, width), lanes(h)]
            elif first:
                newest_past = slice((key_block_offset - 1) * blk, key_block_offset * blk)
                k = jnp.concatenate([past_k_ref[0, newest_past, h, :].astype(BF16), k_ref[0, :, lanes(h)]], axis=0)
                v = jnp.concatenate([past_v_ref[0, newest_past, h, :].astype(BF16), v_ref[0, :, lanes(h)]], axis=0)
            else:
                k = past_k_ref[0, pl.ds(start, width), h, :].astype(BF16)
                v = past_v_ref[0, pl.ds(start, width), h, :].astype(BF16)
            vs.append(v)
            z = lax.dot_general(qs[c], k, (((1,), (1,)), ((), ())), preferred_element_type=F32)
            t = jnp.log2(1.0 + jnp.exp2(-jnp.abs(z)))
            log_not = -(jnp.maximum(z, 0.0) + t)
            if first:
                log_not = jnp.where(masks[c], log_not, 0.0)
            log_nots.append(log_not)
            log_betas.append(jnp.minimum(z, 0.0) - t)
        stacked = jnp.concatenate(log_nots, axis=0)
        hi = stacked.astype(BF16)
        lo = (stacked - hi.astype(F32)).astype(BF16)
        sums = (jnp.dot(hi, tri, preferred_element_type=F32)
                + jnp.dot(lo, tri, preferred_element_type=F32))
        new_accs, new_runs = [], []
        for c in range(len(chains)):
            part = sums[c * blk:(c + 1) * blk]
            between = part[:, :width]
            total = part[:, width:]
            if first:
                a = jnp.where(masks[c], jnp.exp2(log_betas[c] + between), 0.0)
            else:
                a = jnp.exp2(log_betas[c] + between + runs[c][:, :width])
            new_accs.append(accs[c] + jnp.dot(a.astype(BF16), vs[c], preferred_element_type=F32))
            new_runs.append(runs[c] + total)
        return tuple(new_accs), tuple(new_runs)

    zeros = tuple(jnp.zeros((blk, LANES), F32) for _ in chains)
    accs, runs = visit(0, first_span, zeros, zeros, True)

    def alive(carry):
        newest, _, runs = carry
        live = functools.reduce(jnp.maximum, runs)
        return jnp.logical_and(first_diag + n_sub - 1 - newest >= 0, jnp.max(live) > SB_DEAD_LOG2)

    def older(carry):
        newest, accs, runs = carry
        accs, runs = visit(newest, 1, accs, runs, False)
        return newest + 1, accs, runs

    _, accs, _ = lax.while_loop(alive, older, (jnp.int32(first_span), accs, runs))
    for c, (sub, h) in enumerate(chains):
        o_ref[0, rows(sub), lanes(h)] = accs[c].astype(BF16)


def _suffix_sum_matrix(width):
    jj = np.arange(width)
    return jnp.asarray(np.concatenate([jj[:, None] > jj[None, :], np.ones((width, LANES), bool)], axis=1), BF16)


def _row_blocks(rows, max_blocks):
    units = rows // 16
    assert units * 16 == rows
    return max(nb for nb in range(1, min(units, max_blocks) + 1) if units % nb == 0)


def _sb_attention(q, k, v, past_k=None, past_v=None, side=()):
    b, lq, width = q.shape
    has_past = past_k is not None
    past = past_k.shape[1] if has_past else 0
    lk = past + lq
    blk = min(128, lq)
    assert lq % blk == 0 and past % blk == 0 and lk >= 2 * blk
    assert not has_past or lq == blk, "with a cache the new keys must form one block"
    n_q = lq // blk
    n_chains = 16
    n_sub = min(n_chains, n_q)
    n_heads = min(SB_HEADS, n_chains // n_sub)
    assert n_q % n_sub == 0 and SB_HEADS % n_heads == 0
    tri1, tri2 = _suffix_sum_matrix(blk), _suffix_sum_matrix(2 * blk)
    lane_w = n_heads * HEAD_DIM
    keys = pl.BlockSpec((1, lq, lane_w), lambda bi, h, i: (bi, 0, h))
    cache = [pl.BlockSpec((1, past, n_heads, HEAD_DIM), lambda bi, h, i: (bi, 0, h, 0))] * 2 if has_past else []
    grid = (b, SB_HEADS // n_heads, n_q // n_sub)
    n_steps = grid[0] * grid[1] * grid[2]
    side_blocks = tuple(_row_blocks(a.shape[0], n_steps) for a in side)

    def side_spec(a, n_blocks):
        return pl.BlockSpec((a.shape[0] // n_blocks, a.shape[1]),
                            lambda bi, h, i: (jnp.minimum((bi * grid[1] + h) * grid[2] + i, n_blocks - 1), 0))

    side_specs = [side_spec(a, nb) for a, nb in zip(side, side_blocks)]
    out = pl.pallas_call(
        functools.partial(_sb_kernel, blk=blk, n_sub=n_sub, n_heads=n_heads,
                          key_block_offset=past // blk, has_past=has_past, side_blocks=side_blocks),
        grid=grid,
        in_specs=[
            pl.BlockSpec((1, n_sub * blk, lane_w), lambda bi, h, i: (bi, i, h)),
            keys, keys, *cache,
            _const_spec(tri1.shape), _const_spec(tri2.shape),
            *side_specs,
        ],
        out_specs=(pl.BlockSpec((1, n_sub * blk, lane_w), lambda bi, h, i: (bi, i, h)), *side_specs),
        out_shape=(jax.ShapeDtypeStruct(q.shape, BF16), *(jax.ShapeDtypeStruct(a.shape, BF16) for a in side)),
        compiler_params=_params(("arbitrary", "arbitrary", "arbitrary")),
        name="sb_attn",
    )(q, k, v, *((past_k, past_v) if has_past else ()), tri1, tri2, *side)
    return out[0], out[1:]


def _mix_kernel(x_ref, u_ref, ysb_ref, gp_ref, gs_ref, hist_ref, mk_ref, mv_ref,
                pool_w_ref, pool_scale_ref, wbp_ref, wbs_ref, wout_ref, gxa_ref, wq_ref, wo_ref,
                o_ref, ubuf_ref, *, tm, seg, chunk, tiles_per_seq, pos0):
    i = pl.program_id(0)
    group = u_ref.shape[1] // len(POOL_WINDOWS)
    piece = min(chunk, seg)
    u_row = lambda r: r + POOL_HIST * (r // seg + 1)

    if tiles_per_seq > 1:
        tile_in_seq = i % tiles_per_seq

        @pl.when(tile_in_seq == 0)
        def _():
            ubuf_ref[0:POOL_HIST, :] = hist_ref[0]

        seq_row0 = tile_in_seq * tm
    else:
        for s in range(tm // seg):
            top = u_row(s * seg)
            ubuf_ref[top - POOL_HIST:top, :] = hist_ref[s]
        seq_row0 = 0

    for r0 in range(0, tm, chunk):
        rows = slice(r0, r0 + chunk)
        pieces = range(r0, r0 + chunk, piece)
        for r in pieces:
            ubuf_ref[u_row(r):u_row(r) + piece, :] = u_ref[r:r + piece, :]
        pooled = []
        for g, w in enumerate(POOL_WINDOWS):
            cols = slice(g * group, (g + 1) * group)
            deltas = []
            for r in pieces:
                top = u_row(r)
                u = u_ref[r:r + piece, cols]
                partial = ubuf_ref[top - POOL_HIST:top + piece, cols]
                span = 1
                while span < w:
                    partial = partial + pltpu.roll(partial, span, axis=0)
                    span *= 2
                window_sum = partial[POOL_HIST:]
                pos = pos0 + seq_row0 + r % seg + lax.broadcasted_iota(jnp.int32, (piece, 1), 0)
                count = jnp.minimum(w, pos + 1).astype(F32)
                deltas.append((window_sum / count - u).astype(BF16))
            delta = deltas[0] if len(deltas) == 1 else jnp.concatenate(deltas, axis=0)
            pooled.append(jnp.dot(delta, pool_w_ref[g], preferred_element_type=F32))
        y_pool = (jnp.concatenate(pooled, axis=-1) * pool_scale_ref[...]).astype(BF16)

        branch_pool = jnp.dot(y_pool, wbp_ref[...], preferred_element_type=F32)
        branch_sb = jnp.dot(ysb_ref[rows, :], wbs_ref[...], preferred_element_type=F32)
        merged = gp_ref[rows, :].astype(F32) * branch_pool + gs_ref[rows, :].astype(F32) * branch_sb
        x1 = x_ref[rows, :] + jnp.dot(merged.astype(BF16), wout_ref[...], preferred_element_type=F32)

        xn = _rmsnorm(x1, gxa_ref[...]).astype(BF16)
        q = (jnp.dot(xn, wq_ref[...], preferred_element_type=F32) * (1.0 / np.sqrt(HEAD_DIM))).astype(BF16)
        attended = []
        for r in pieces:
            s = r // seg
            heads = []
            for h in range(XA_HEADS):
                cols = slice(h * HEAD_DIM, (h + 1) * HEAD_DIM)
                sc = lax.dot_general(q[r - r0:r - r0 + piece, cols], mk_ref[s, :, cols], (((1,), (1,)), ((), ())),
                                     preferred_element_type=F32)
                p = jnp.exp(sc - jnp.max(sc, axis=-1, keepdims=True))
                p = p / jnp.sum(p, axis=-1, keepdims=True)
                heads.append(jnp.dot(p.astype(BF16), mv_ref[s, :, cols], preferred_element_type=F32))
            attended.append(jnp.concatenate(heads, axis=-1).astype(BF16))
        attn = attended[0] if len(attended) == 1 else jnp.concatenate(attended, axis=0)
        o_ref[rows, :] = x1 + jnp.dot(attn, wo_ref[...], preferred_element_type=F32)

    if tiles_per_seq > 1:
        ubuf_ref[0:POOL_HIST, :] = ubuf_ref[tm:tm + POOL_HIST, :]


def _mix(x, seq_len, u, ysb, gates, hist, mk, mv, pool_w, pool_scale, wbp, wbs, wout, gxa, wq, wo, pos0):
    n, d = x.shape
    tm = min(256, n)
    seg = min(seq_len, tm)
    n_seg = tm // seg
    tiles_per_seq = seq_len // seg
    assert n % tm == 0 and tm % seg == 0 and seq_len % seg == 0
    tile = lambda i: (i, 0)
    per_seq = lambda i: (i // tiles_per_seq, 0, 0)
    return pl.pallas_call(
        functools.partial(_mix_kernel, tm=tm, seg=seg, chunk=tm, tiles_per_seq=tiles_per_seq,
                          pos0=pos0),
        grid=(n // tm,),
        in_specs=[
            pl.BlockSpec((tm, d), tile),
            pl.BlockSpec((tm, u.shape[1]), tile),
            pl.BlockSpec((tm, ysb.shape[1]), tile),
            pl.BlockSpec((tm, d), tile),
            pl.BlockSpec((tm, d), lambda i: (i, 1)),
            pl.BlockSpec((n_seg,) + hist.shape[1:], per_seq),
            pl.BlockSpec((n_seg,) + mk.shape[1:], per_seq),
            pl.BlockSpec((n_seg,) + mv.shape[1:], per_seq),
            _const_spec(pool_w.shape), _const_spec(pool_scale.shape), _const_spec(wbp.shape),
            _const_spec(wbs.shape), _const_spec(wout.shape), _const_spec(gxa.shape),
            _const_spec(wq.shape), _const_spec(wo.shape),
        ],
        out_specs=pl.BlockSpec((tm, d), tile),
        out_shape=jax.ShapeDtypeStruct(x.shape, F32),
        scratch_shapes=[pltpu.VMEM((tm + POOL_HIST * n_seg, u.shape[1]), F32)],
        compiler_params=_params(("arbitrary",)),
        name="mix",
    )(x, u, ysb, gates, gates, hist, mk, mv, pool_w, pool_scale, wbp, wbs, wout, gxa, wq, wo)


def _ffn_kernel(*refs, tm, seg, chunk, tiles_per_seq, n_f, per_step):
    x_ref, g_ref = refs[:2]
    wg_refs = refs[2:2 + per_step]
    wv_refs = refs[2 + per_step:2 + 2 * per_step]
    conv_ref = refs[2 + 2 * per_step]
    wd_refs = refs[3 + 2 * per_step:3 + 3 * per_step]
    state_ref, gf_ref, y_ref, tail_ref, xn_ref = refs[3 + 3 * per_step:8 + 3 * per_step]
    h_refs = refs[8 + 3 * per_step:8 + 5 * per_step]
    carry_ref = refs[8 + 5 * per_step]
    i = pl.program_id(0)
    j = pl.program_id(1)
    piece = min(chunk, seg)
    lags = CONV_W - 1
    tf = wg_refs[0].shape[1]
    h_row = lambda r: r + CONV_HIST * (r // seg + 1)

    def lanes_of(half, k):
        return pl.ds(pl.multiple_of((half * n_f + j * per_step + k) * tf, tf), tf)

    def slots(n_tiles):
        out = []
        for k in range(n_tiles):
            for half, w_ref in enumerate((wg_refs[k], wv_refs[k])):
                out.append((w_ref, h_refs[2 * k + half], lanes_of(half, k), (half, j * per_step + k)))
        return out

    def load_history(n_tiles):
        for _, h_ref, lanes, carry_at in slots(n_tiles):
            if tiles_per_seq > 1:
                first_of_seq = (i % tiles_per_seq) == 0

                @pl.when(first_of_seq)
                def _(h_ref=h_ref, lanes=lanes):
                    h_ref[CONV_HIST - lags:CONV_HIST, :] = state_ref[0, :, lanes]

                @pl.when(jnp.logical_not(first_of_seq))
                def _(h_ref=h_ref, carry_at=carry_at):
                    h_ref[0:CONV_HIST, :] = carry_ref[carry_at[0], carry_at[1]]
            else:
                for s in range(tm // seg):
                    top = h_row(s * seg)
                    h_ref[top - lags:top, :] = state_ref[s, :, lanes]

    def up_conv(r0, slot, xn):
        w_ref, h_ref, lanes, _ = slot
        taps = conv_ref[0:CONV_W, lanes]
        bias = conv_ref[CONV_W:CONV_W + 1, lanes]
        h = jnp.dot(xn, w_ref[...], preferred_element_type=F32)
        outs = []
        for p0 in range(0, chunk, piece):
            top = h_row(r0 + p0)
            hp = h[p0:p0 + piece]
            h_ref[top:top + piece, :] = hp
            out = taps[lags:lags + 1, :] * hp
            for tap in range(lags):
                lag = lags - tap
                out = out + taps[tap:tap + 1, :] * h_ref[top - lag:top - lag + piece, :]
            outs.append(out + bias)
            seg_end = r0 + p0 + piece
            if seg_end % seg == 0:
                tail_ref[seg_end // seg - 1, :, lanes] = hp[piece - lags:, :]
        return outs[0] if len(outs) == 1 else jnp.concatenate(outs, axis=0)

    def step(first, last, n_tiles):
        load_history(n_tiles)
        todo = slots(n_tiles)
        acts = []
        for r0 in range(0, tm, chunk):
            rows = slice(r0, r0 + chunk)
            if first:
                xn = _rmsnorm(x_ref[rows, :], g_ref[...]).astype(BF16)
                xn_ref[rows, :] = xn
            else:
                xn = xn_ref[rows, :]
            ups = [up_conv(r0, slot, xn) for slot in todo]
            acts.append((rows, [(ups[2 * k] * jax.nn.sigmoid(ups[2 * k]) * ups[2 * k + 1]).astype(BF16)
                                for k in range(n_tiles)]))
        for rows, chunk_acts in acts:
            partial = sum(jnp.dot(act, wd_refs[k][...], preferred_element_type=F32)
                          for k, act in enumerate(chunk_acts))
            if first:
                y_ref[rows, :] = partial
            elif last:
                y_ref[rows, :] = _rmsnorm(x_ref[rows, :] + (y_ref[rows, :] + partial), gf_ref[...])
            else:
                y_ref[rows, :] += partial
        if tiles_per_seq > 1:
            for _, h_ref, _, carry_at in todo:
                carry_ref[carry_at[0], carry_at[1]] = h_ref[tm:tm + CONV_HIST, :]

    last_j = pl.num_programs(1) - 1
    in_last = n_f - (n_f - 1) // per_step * per_step
    pl.when(j == 0)(functools.partial(step, True, False, per_step))
    pl.when(jnp.logical_and(j > 0, j < last_j))(functools.partial(step, False, False, per_step))
    pl.when(j == last_j)(functools.partial(step, False, True, in_last))


def _conv_ffn(x, seq_len, g, w_up, conv_w, conv_b, w_down, state, g_final):
    n, d = x.shape
    f = w_down.shape[0]
    tm = min(512, n)
    tf = FFN_TILE
    per_step = FFN_TILES_PER_STEP
    seg = min(seq_len, tm)
    n_seg = tm // seg
    tiles_per_seq = seq_len // seg
    assert f % tf == 0 and n % tm == 0 and tm % seg == 0 and seq_len % seg == 0
    n_f = f // tf
    n_steps = pl.cdiv(n_f, per_step)
    assert n_steps >= 2, "first and last step must be different grid steps"
    h_rows = tm + CONV_HIST * n_seg
    conv = jnp.concatenate([conv_w, conv_b], axis=0)
    f_tile = lambda j, k: jnp.minimum(j * per_step + k, n_f - 1)
    up_spec = lambda half, k: pl.BlockSpec((d, tf), lambda i, j: (0, half * n_f + f_tile(j, k)))
    down_spec = lambda k: pl.BlockSpec((tf, d), lambda i, j: (f_tile(j, k), 0))
    y, tails = pl.pallas_call(
        functools.partial(_ffn_kernel, tm=tm, seg=seg, chunk=min(FFN_ROW_CHUNK, tm), tiles_per_seq=tiles_per_seq,
                          n_f=n_f, per_step=per_step),
        grid=(n // tm, n_steps),
        in_specs=[
            pl.BlockSpec((tm, d), lambda i, j: (i, 0)),
            _const_spec((1, d)),
            *[up_spec(0, k) for k in range(per_step)], *[up_spec(1, k) for k in range(per_step)],
            _const_spec(conv.shape),
            *[down_spec(k) for k in range(per_step)],
            pl.BlockSpec((n_seg, CONV_W - 1, 2 * f), lambda i, j: (i // tiles_per_seq, 0, 0)),
            _const_spec((1, d)),
        ],
        out_specs=(
            pl.BlockSpec((tm, d), lambda i, j: (i, 0)),
            pl.BlockSpec((n_seg, CONV_W - 1, 2 * f), lambda i, j: (i, 0, 0)),
        ),
        out_shape=(
            jax.ShapeDtypeStruct((n, d), F32),
            jax.ShapeDtypeStruct((n // seg, CONV_W - 1, 2 * f), F32),
        ),
        scratch_shapes=[
            pltpu.VMEM((tm, d), BF16),
            *[pltpu.VMEM((h_rows, tf), F32) for _ in range(2 * per_step)],
            pltpu.VMEM((2, n_f, CONV_HIST, tf), F32),
        ],
        compiler_params=_params(("arbitrary", "arbitrary")),
        name="conv_ffn",
    )(x, g, *[w_up] * (2 * per_step), conv, *[w_down] * per_step, state, g_final)
    return y, tails[tiles_per_seq - 1::tiles_per_seq]


LATE_WEIGHTS = ("pool_w", "w_branch_pool", "w_branch_sb", "w_out", "xa_wq", "xa_wo", "ffn_w_up", "ffn_w_down")


def _layer(x, pool_state, past_k, past_v, mk, mv, conv_state, p, late=None):
    b, l, d = x.shape
    n = b * l
    assert l >= POOL_HIST and l >= CONV_W - 1
    x = x.reshape(n, d)
    u, q, kf, kb, vf, vb, gates = _in_proj(x, p["norm_mix_g"], p["w_in"], p["b_gate"])
    width = u.shape[1]
    if past_k is None:
        past = 0
        hist = jnp.zeros((b, POOL_HIST, width), F32)
    else:
        past = past_k.shape[1]
        hist = jnp.pad(pool_state, ((0, 0), (POOL_HIST - pool_state.shape[1], 0), (0, 0)))
    by_seq = lambda a: a.reshape(b, l, width)
    side = () if late is not None else tuple(p[name].reshape(-1, p[name].shape[-1]) for name in LATE_WEIGHTS)
    y_sb, casted = _sb_attention(by_seq(q), by_seq(kb), by_seq(vb), past_k, past_v, side)
    if late is None:
        late = {name: c.reshape(p[name].shape) for name, c in zip(LATE_WEIGHTS, casted)}

    x2 = _mix(x, l, u, y_sb.reshape(n, width), gates, hist, mk, mv,
              late["pool_w"], p["pool_scale"], late["w_branch_pool"], late["w_branch_sb"], late["w_out"],
              p["norm_xa_g"], late["xa_wq"], late["xa_wo"], past)

    y, new_conv = _conv_ffn(x2, l, p["norm_ffn_g"], late["ffn_w_up"], p["ffn_conv_w"],
                            p["ffn_conv_b"], late["ffn_w_down"], conv_state, p["norm_final_g"])
    new_pool = by_seq(u)[:, l - (POOL_HIST - 1):, :]
    return (y.reshape(b, l, d), new_pool, kf.reshape(b, l, SB_HEADS, HEAD_DIM),
            vf.reshape(b, l, SB_HEADS, HEAD_DIM), new_conv), late


def kernel(x_prompt, x_sample, mem_prompt, state_pool, cache_sb_k, cache_sb_v, cache_mem_k, cache_mem_v, state_ffn_conv, norm_mix_g, w_in, b_gate, pool_w, pool_scale, w_branch_pool, w_branch_sb, w_out, norm_xa_g, norm_mem_g, xa_wq, xa_wkv, xa_wo, norm_ffn_g, ffn_w_up, ffn_conv_w, ffn_conv_b, ffn_w_down, norm_final_g):
    assert norm_mix_g.shape[0] == 1, "single-layer step"
    bp, lp, d = x_prompt.shape
    n_mem = mem_prompt.shape[1]
    xa_width = xa_wq.shape[2]
    row = lambda a: a.reshape(1, -1)
    p = {
        "norm_mix_g": row(norm_mix_g[0]), "w_in": w_in[0].astype(BF16), "b_gate": row(b_gate[0]),
        "pool_w": pool_w[0], "pool_scale": row(pool_scale[0]),
        "w_branch_pool": w_branch_pool[0], "w_branch_sb": w_branch_sb[0],
        "w_out": w_out[0], "norm_xa_g": row(norm_xa_g[0]),
        "xa_wq": xa_wq[0], "xa_wo": xa_wo[0],
        "norm_ffn_g": row(norm_ffn_g[0]), "ffn_w_up": ffn_w_up[0],
        "ffn_conv_w": ffn_conv_w[0], "ffn_conv_b": row(ffn_conv_b[0]),
        "ffn_w_down": ffn_w_down[0], "norm_final_g": row(norm_final_g),
    }

    mem_kv = _norm_matmul(mem_prompt.reshape(bp * n_mem, d), row(norm_mem_g[0]), xa_wkv[0].astype(BF16))
    mem_kv = mem_kv.reshape(bp, n_mem, 2 * xa_width)
    mk_p, mv_p = mem_kv[..., :xa_width], mem_kv[..., xa_width:]

    conv0 = jnp.zeros((bp, CONV_W - 1, ffn_w_up.shape[2]), F32)
    (y_p, pool_p, k_p, v_p, conv_p), late = _layer(x_prompt, None, None, None, mk_p.astype(BF16),
                                                   mv_p.astype(BF16), conv0, p)

    bs = x_sample.shape[0]
    mk_s = cache_mem_k[0].reshape(bs, n_mem, xa_width).astype(BF16)
    mv_s = cache_mem_v[0].reshape(bs, n_mem, xa_width).astype(BF16)
    (y_s, pool_s, k_s, v_s, conv_s), _ = _layer(x_sample, state_pool[0], cache_sb_k[0], cache_sb_v[0], mk_s, mv_s,
                                                state_ffn_conv[0], p, late)

    mem_shape = (1, bp, n_mem, XA_HEADS, HEAD_DIM)
    return (y_p, y_s,
            pool_p[None], k_p[None], v_p[None], mk_p.reshape(mem_shape), mv_p.reshape(mem_shape), conv_p[None],
            pool_s[None], k_s[None], v_s[None], conv_s[None])
```

```python
import functools

import jax
import jax.numpy as jnp
import numpy as np
from jax import lax
from jax.experimental import pallas as pl
from jax.experimental.pallas import tpu as pltpu

F32 = jnp.float32
BF16 = jnp.bfloat16

EPS = 1e-6
POOL_WINDOWS = (2, 4, 8, 16)
POOL_HIST = 16
SB_HEADS = 8
HEAD_DIM = 128
XA_HEADS = 4
CONV_W = 3
CONV_HIST = 8
LANES = 128
FFN_TILE = 512
FFN_TILES_PER_STEP = 2
ROW_CHUNK = 256

SB_DEAD_LOG2 = -127.0
LOG2_E = 1.4426950408889634

VMEM_LIMIT = 56 * 1024 * 1024


def _rmsnorm(xf, g):
    ms = jnp.mean(xf * xf, axis=-1, keepdims=True)
    return xf * lax.rsqrt(ms + EPS) * g


def _const_spec(shape):
    zeros = (0,) * len(shape)
    return pl.BlockSpec(shape, lambda *_: zeros, pipeline_mode=pl.Buffered(1))


def _params(semantics):
    return pltpu.CompilerParams(dimension_semantics=semantics, vmem_limit_bytes=VMEM_LIMIT)


def _in_proj_kernel(x_ref, g_ref, w_ref, b_ref, u_ref, q_ref, kf_ref, kb_ref, vf_ref, vb_ref, gate_ref,
                    *, q_scale, chunk):
    j = pl.program_id(0)
    tm = x_ref.shape[0]
    half = u_ref.shape[1]

    def project(epilogue):
        for r0 in range(0, tm, chunk):
            rows = slice(r0, r0 + chunk)
            xn = _rmsnorm(x_ref[rows, :], g_ref[...]).astype(BF16)
            epilogue(rows, jnp.dot(xn, w_ref[...], preferred_element_type=F32))

    def put_uq(rows, acc):
        u_ref[rows, :] = acc[:, :half]
        q_ref[rows, :] = (acc[:, half:] * q_scale).astype(BF16)

    def put_heads(f32_ref, bf16_ref, rows, acc):
        f32_ref[rows, :, :] = acc.reshape(acc.shape[0], f32_ref.shape[1], f32_ref.shape[2])
        bf16_ref[rows, :] = acc.astype(BF16)

    def put_kv(rows, acc):
        put_heads(kf_ref, kb_ref, rows, acc[:, :half])
        put_heads(vf_ref, vb_ref, rows, acc[:, half:])

    def put_gate(rows, acc):
        gate_ref[rows, :] = jax.nn.sigmoid(acc + b_ref[...]).astype(BF16)

    pl.when(j == 0)(functools.partial(project, put_uq))
    pl.when(j == 1)(functools.partial(project, put_kv))
    pl.when(j >= 2)(functools.partial(project, put_gate))


def _in_proj(x, g, w_in, b_gate):
    n, d = x.shape
    half = d // 2
    n_col = w_in.shape[1] // d
    heads = half // HEAD_DIM
    tm = min(512, n)
    n_tiles = n // tm

    def rows_while(active):
        return lambda j, i: jnp.where(j < active, 0, jnp.where(j == active, i, n_tiles - 1))

    def flat(active):
        tile = rows_while(active)
        return pl.BlockSpec((tm, half), lambda j, i: (tile(j, i), 0))

    def by_head(active):
        tile = rows_while(active)
        return pl.BlockSpec((tm, heads, HEAD_DIM), lambda j, i: (tile(j, i), 0, 0))

    gate_tile = lambda j, i: (jnp.where(j < 2, 0, i), jnp.maximum(j - 2, 0))
    out_shape = (
        jax.ShapeDtypeStruct((n, half), F32),
        jax.ShapeDtypeStruct((n, half), BF16),
        jax.ShapeDtypeStruct((n, heads, HEAD_DIM), F32),
        jax.ShapeDtypeStruct((n, half), BF16),
        jax.ShapeDtypeStruct((n, heads, HEAD_DIM), F32),
        jax.ShapeDtypeStruct((n, half), BF16),
        jax.ShapeDtypeStruct((n, 2 * d), BF16),
    )
    out_specs = (flat(0), flat(0), by_head(1), flat(1), by_head(1), flat(1), pl.BlockSpec((tm, d), gate_tile))
    return pl.pallas_call(
        functools.partial(_in_proj_kernel, q_scale=LOG2_E / np.sqrt(HEAD_DIM), chunk=min(ROW_CHUNK, tm)),
        grid=(n_col, n_tiles),
        in_specs=[
            pl.BlockSpec((tm, d), lambda j, i: (i, 0)),
            pl.BlockSpec((1, d), lambda j, i: (0, 0)),
            pl.BlockSpec((d, d), lambda j, i: (0, j)),
            pl.BlockSpec((1, d), lambda j, i: (0, jnp.maximum(j - 2, 0))),
        ],
        out_specs=out_specs,
        out_shape=out_shape,
        compiler_params=_params(("arbitrary", "arbitrary")),
        name="in_proj",
    )(x, g, w_in, b_gate)


def _norm_matmul_kernel(x_ref, g_ref, w_ref, o_ref):
    xn = _rmsnorm(x_ref[...], g_ref[...]).astype(BF16)
    o_ref[...] = jnp.dot(xn, w_ref[...], preferred_element_type=F32)


def _norm_matmul(x, g, w):
    n, d = x.shape
    tm = min(256, n)
    return pl.pallas_call(
        _norm_matmul_kernel,
        grid=(n // tm,),
        in_specs=[pl.BlockSpec((tm, d), lambda i: (i, 0)), _const_spec((1, d)), _const_spec(w.shape)],
        out_specs=pl.BlockSpec((tm, w.shape[1]), lambda i: (i, 0)),
        out_shape=jax.ShapeDtypeStruct((n, w.shape[1]), F32),
        compiler_params=_params(("arbitrary",)),
        name="mem_kv",
    )(x, g, w)


def _sb_kernel(*refs, blk, n_sub, n_heads, key_block_offset, has_past, side_blocks):
    n_side = len(side_blocks)
    n_in = len(refs) - 1 - n_side
    side_in, side_out = refs[n_in - n_side:n_in], refs[n_in + 1:]
    o_ref = refs[n_in]
    if has_past:
        q_ref, k_ref, v_ref, past_k_ref, past_v_ref, tri1_ref, tri2_ref = refs[:n_in - n_side]
    else:
        q_ref, k_ref, v_ref, tri1_ref, tri2_ref = refs[:n_in - n_side]
    step = pl.program_id(2)

    linear_step = (pl.program_id(0) * pl.num_programs(1) + pl.program_id(1)) * pl.num_programs(2) + step
    for src_ref, dst_ref, n_blocks in zip(side_in, side_out, side_blocks):
        @pl.when(linear_step < n_blocks)
        def _(src_ref=src_ref, dst_ref=dst_ref):
            dst_ref[...] = src_ref[...].astype(BF16)

    first_span = 2
    wide = first_span * blk
    col_minus_row = (lax.broadcasted_iota(jnp.int32, (blk, wide), 1)
                     - lax.broadcasted_iota(jnp.int32, (blk, wide), 0))
    chains = [(sub, h) for sub in range(n_sub) for h in range(n_heads)]
    rows = lambda sub: slice(sub * blk, (sub + 1) * blk)
    lanes = lambda h: slice(h * HEAD_DIM, (h + 1) * HEAD_DIM)
    qs = [q_ref[0, rows(sub), lanes(h)] for sub, h in chains]
    first_diag = step * n_sub + key_block_offset

    def visit(newest, span, accs, runs, first):
        width = span * blk
        tri = (tri1_ref if span == 1 else tri2_ref)[...]
        log_nots, log_betas, vs, masks = [], [], [], []
        runs = list(runs)
        for c, (sub, h) in enumerate(chains):
            kb_new = first_diag + sub - newest
            kb_old = kb_new - (span - 1)
            start_blk = jnp.maximum(kb_old, 0)
            if first:
                masks.append(col_minus_row[:, :width] < (kb_new - start_blk) * blk)
            else:
                runs[c] = jnp.where(kb_old >= 0, runs[c], -1e30)
            start = pl.multiple_of(start_blk * blk, blk)
            if not has_past:
                k = k_ref[0, pl.ds(start, width), lanes(h)]
                v = v_ref[0, pl.ds(start, width), lanes(h)]
            elif first:
                newest_past = slice((key_block_offset - 1) * blk, key_block_offset * blk)
                k = jnp.concatenate([past_k_ref[0, newest_past, h, :].astype(BF16), k_ref[0, :, lanes(h)]], axis=0)
                v = jnp.concatenate([past_v_ref[0, newest_past, h, :].astype(BF16), v_ref[0, :, lanes(h)]], axis=0)
            else:
                k = past_k_ref[0, pl.ds(start, width), h, :].astype(BF16)
                v = past_v_ref[0, pl.ds(start, width), h, :].astype(BF16)
            vs.append(v)
            z = lax.dot_general(qs[c], k, (((1,), (1,)), ((), ())), preferred_element_type=F32)
            t = jnp.log2(1.0 + jnp.exp2(-jnp.abs(z)))
            log_not = -(jnp.maximum(z, 0.0) + t)
            if first:
                log_not = jnp.where(masks[c], log_not, 0.0)
            log_nots.append(log_not)
            log_betas.append(jnp.minimum(z, 0.0) - t)
        stacked = jnp.concatenate(log_nots, axis=0)
        hi = stacked.astype(BF16)
        lo = (stacked - hi.astype(F32)).astype(BF16)
        sums = (jnp.dot(hi, tri, preferred_element_type=F32)
                + jnp.dot(lo, tri, preferred_element_type=F32))
        new_accs, new_runs = [], []
        for c in range(len(chains)):
            part = sums[c * blk:(c + 1) * blk]
            between = part[:, :width]
            total = part[:, width:]
            if first:
                a = jnp.where(masks[c], jnp.exp2(log_betas[c] + between), 0.0)
            else:
                a = jnp.exp2(log_betas[c] + between + runs[c][:, :width])
            new_accs.append(accs[c] + jnp.dot(a.astype(BF16), vs[c], preferred_element_type=F32))
            new_runs.append(runs[c] + total)
        return tuple(new_accs), tuple(new_runs)

    zeros = tuple(jnp.zeros((blk, LANES), F32) for _ in chains)
    accs, runs = visit(0, first_span, zeros, zeros, True)

    def alive(carry):
        newest, _, runs = carry
        live = functools.reduce(jnp.maximum, runs)
        return jnp.logical_and(first_diag + n_sub - 1 - newest >= 0, jnp.max(live) > SB_DEAD_LOG2)

    def older(carry):
        newest, accs, runs = carry
        accs, runs = visit(newest, 1, accs, runs, False)
        return newest + 1, accs, runs

    _, accs, _ = lax.while_loop(alive, older, (jnp.int32(first_span), accs, runs))
    for c, (sub, h) in enumerate(chains):
        o_ref[0, rows(sub), lanes(h)] = accs[c].astype(BF16)


def _suffix_sum_matrix(width):
    jj = np.arange(width)
    return jnp.asarray(np.concatenate([jj[:, None] > jj[None, :], np.ones((width, LANES), bool)], axis=1), BF16)


def _row_blocks(rows, max_blocks):
    units = rows // 16
    assert units * 16 == rows
    return max(nb for nb in range(1, min(units, max_blocks) + 1) if units % nb == 0)


def _sb_attention(q, k, v, past_k=None, past_v=None, side=()):
    b, lq, width = q.shape
    has_past = past_k is not None
    past = past_k.shape[1] if has_past else 0
    lk = past + lq
    blk = min(128, lq)
    assert lq % blk == 0 and past % blk == 0 and lk >= 2 * blk
    assert not has_past or lq == blk, "with a cache the new keys must form one block"
    n_q = lq // blk
    n_chains = 16
    n_sub = min(n_chains, n_q)
    n_heads = min(SB_HEADS, n_chains // n_sub)
    assert n_q % n_sub == 0 and SB_HEADS % n_heads == 0
    tri1, tri2 = _suffix_sum_matrix(blk), _suffix_sum_matrix(2 * blk)
    lane_w = n_heads * HEAD_DIM
    keys = pl.BlockSpec((1, lq, lane_w), lambda bi, h, i: (bi, 0, h))
    cache = [pl.BlockSpec((1, past, n_heads, HEAD_DIM), lambda bi, h, i: (bi, 0, h, 0))] * 2 if has_past else []
    grid = (b, SB_HEADS // n_heads, n_q // n_sub)
    n_steps = grid[0] * grid[1] * grid[2]
    side_blocks = tuple(_row_blocks(a.shape[0], n_steps) for a in side)

    def side_spec(a, n_blocks):
        return pl.BlockSpec((a.shape[0] // n_blocks, a.shape[1]),
                            lambda bi, h, i: (jnp.minimum((bi * grid[1] + h) * grid[2] + i, n_blocks - 1), 0))

    side_specs = [side_spec(a, nb) for a, nb in zip(side, side_blocks)]
    out = pl.pallas_call(
        functools.partial(_sb_kernel, blk=blk, n_sub=n_sub, n_heads=n_heads,
                          key_block_offset=past // blk, has_past=has_past, side_blocks=side_blocks),
        grid=grid,
        in_specs=[
            pl.BlockSpec((1, n_sub * blk, lane_w), lambda bi, h, i: (bi, i, h)),
            keys, keys, *cache,
            _const_spec(tri1.shape), _const_spec(tri2.shape),
            *side_specs,
        ],
        out_specs=(pl.BlockSpec((1, n_sub * blk, lane_w), lambda bi, h, i: (bi, i, h)), *side_specs),
        out_shape=(jax.ShapeDtypeStruct(q.shape, BF16), *(jax.ShapeDtypeStruct(a.shape, BF16) for a in side)),
        compiler_params=_params(("arbitrary", "arbitrary", "arbitrary")),
        name="sb_attn",
    )(q, k, v, *((past_k, past_v) if has_past else ()), tri1, tri2, *side)
    return out[0], out[1:]


def _mix_kernel(x_ref, u_ref, ysb_ref, gp_ref, gs_ref, hist_ref, mk_ref, mv_ref,
                pool_w_ref, pool_scale_ref, wbp_ref, wbs_ref, wout_ref, gxa_ref, wq_ref, wo_ref,
                o_ref, ubuf_ref, *, tm, seg, chunk, tiles_per_seq, pos0):
    i = pl.program_id(0)
    group = u_ref.shape[1] // len(POOL_WINDOWS)
    piece = min(chunk, seg)
    u_row = lambda r: r + POOL_HIST * (r // seg + 1)

    if tiles_per_seq > 1:
        tile_in_seq = i % tiles_per_seq

        @pl.when(tile_in_seq == 0)
        def _():
            ubuf_ref[0:POOL_HIST, :] = hist_ref[0]

        seq_row0 = tile_in_seq * tm
    else:
        for s in range(tm // seg):
            top = u_row(s * seg)
            ubuf_ref[top - POOL_HIST:top, :] = hist_ref[s]
        seq_row0 = 0

    for r0 in range(0, tm, chunk):
        rows = slice(r0, r0 + chunk)
        pieces = range(r0, r0 + chunk, piece)
        for r in pieces:
            ubuf_ref[u_row(r):u_row(r) + piece, :] = u_ref[r:r + piece, :]
        pooled = []
        for g, w in enumerate(POOL_WINDOWS):
            cols = slice(g * group, (g + 1) * group)
            deltas = []
            for r in pieces:
                top = u_row(r)
                u = u_ref[r:r + piece, cols]
                partial = ubuf_ref[top - POOL_HIST:top + piece, cols]
                span = 1
                while span < w:
                    partial = partial + pltpu.roll(partial, span, axis=0)
                    span *= 2
                window_sum = partial[POOL_HIST:]
                pos = pos0 + seq_row0 + r % seg + lax.broadcasted_iota(jnp.int32, (piece, 1), 0)
                count = jnp.minimum(w, pos + 1).astype(F32)
                deltas.append((window_sum / count - u).astype(BF16))
            delta = deltas[0] if len(deltas) == 1 else jnp.concatenate(deltas, axis=0)
            pooled.append(jnp.dot(delta, pool_w_ref[g], preferred_element_type=F32))
        y_pool = (jnp.concatenate(pooled, axis=-1) * pool_scale_ref[...]).astype(BF16)

        branch_pool = jnp.dot(y_pool, wbp_ref[...], preferred_element_type=F32)
        branch_sb = jnp.dot(ysb_ref[rows, :], wbs_ref[...], preferred_element_type=F32)
        merged = gp_ref[rows, :].astype(F32) * branch_pool + gs_ref[rows, :].astype(F32) * branch_sb
        x1 = x_ref[rows, :] + jnp.dot(merged.astype(BF16), wout_ref[...], preferred_element_type=F32)

        xn = _rmsnorm(x1, gxa_ref[...]).astype(BF16)
        q = (jnp.dot(xn, wq_ref[...], preferred_element_type=F32) * (1.0 / np.sqrt(HEAD_DIM))).astype(BF16)
        attended = []
        for r in pieces:
            s = r // seg
            head_cols = [slice(h * HEAD_DIM, (h + 1) * HEAD_DIM) for h in range(XA_HEADS)]
            scores = [lax.dot_general(q[r - r0:r - r0 + piece, cols], mk_ref[s, :, cols], (((1,), (1,)), ((), ())),
                                      preferred_element_type=F32) for cols in head_cols]
            probs = [jnp.exp(sc - jnp.max(sc, axis=-1, keepdims=True)) for sc in scores]
            norms = [jnp.sum(p, axis=-1, keepdims=True) for p in probs]
            heads = [jnp.dot(p.astype(BF16), mv_ref[s, :, cols], preferred_element_type=F32) / z
                     for p, z, cols in zip(probs, norms, head_cols)]
            attended.append(jnp.concatenate(heads, axis=-1).astype(BF16))
        attn = attended[0] if len(attended) == 1 else jnp.concatenate(attended, axis=0)
        o_ref[rows, :] = x1 + jnp.dot(attn, wo_ref[...], preferred_element_type=F32)

    if tiles_per_seq > 1:
        ubuf_ref[0:POOL_HIST, :] = ubuf_ref[tm:tm + POOL_HIST, :]


def _mix(x, seq_len, u, ysb, gates, hist, mk, mv, pool_w, pool_scale, wbp, wbs, wout, gxa, wq, wo, pos0):
    n, d = x.shape
    tm = min(256, n)
    seg = min(seq_len, tm)
    n_seg = tm // seg
    tiles_per_seq = seq_len // seg
    assert n % tm == 0 and tm % seg == 0 and seq_len % seg == 0
    tile = lambda i: (i, 0)
    per_seq = lambda i: (i // tiles_per_seq, 0, 0)
    return pl.pallas_call(
        functools.partial(_mix_kernel, tm=tm, seg=seg, chunk=tm, tiles_per_seq=tiles_per_seq,
                          pos0=pos0),
        grid=(n // tm,),
        in_specs=[
            pl.BlockSpec((tm, d), tile),
            pl.BlockSpec((tm, u.shape[1]), tile),
            pl.BlockSpec((tm, ysb.shape[1]), tile),
            pl.BlockSpec((tm, d), tile),
            pl.BlockSpec((tm, d), lambda i: (i, 1)),
            pl.BlockSpec((n_seg,) + hist.shape[1:], per_seq),
            pl.BlockSpec((n_seg,) + mk.shape[1:], per_seq),
            pl.BlockSpec((n_seg,) + mv.shape[1:], per_seq),
            _const_spec(pool_w.shape), _const_spec(pool_scale.shape), _const_spec(wbp.shape),
            _const_spec(wbs.shape), _const_spec(wout.shape), _const_spec(gxa.shape),
            _const_spec(wq.shape), _const_spec(wo.shape),
        ],
        out_specs=pl.BlockSpec((tm, d), tile),
        out_shape=jax.ShapeDtypeStruct(x.shape, F32),
        scratch_shapes=[pltpu.VMEM((tm + POOL_HIST * n_seg, u.shape[1]), F32)],
        compiler_params=_params(("arbitrary",)),
        name="mix",
    )(x, u, ysb, gates, gates, hist, mk, mv, pool_w, pool_scale, wbp, wbs, wout, gxa, wq, wo)


def _ffn_kernel(*refs, tm, seg, chunk, tiles_per_seq, n_f, per_step):
    x_ref, g_ref = refs[:2]
    wg_refs = refs[2:2 + per_step]
    wv_refs = refs[2 + per_step:2 + 2 * per_step]
    conv_ref = refs[2 + 2 * per_step]
    wd_refs = refs[3 + 2 * per_step:3 + 3 * per_step]
    state_ref, gf_ref, y_ref, tail_ref, xn_ref = refs[3 + 3 * per_step:8 + 3 * per_step]
    h_refs = refs[8 + 3 * per_step:8 + 5 * per_step]
    carry_ref = refs[8 + 5 * per_step]
    i = pl.program_id(0)
    j = pl.program_id(1)
    piece = min(chunk, seg)
    lags = CONV_W - 1
    tf = wg_refs[0].shape[1]
    h_row = lambda r: r + CONV_HIST * (r // seg + 1)

    def lanes_of(half, k):
        return pl.ds(pl.multiple_of((half * n_f + j * per_step + k) * tf, tf), tf)

    def slots(n_tiles):
        out = []
        for k in range(n_tiles):
            for half, w_ref in enumerate((wg_refs[k], wv_refs[k])):
                out.append((w_ref, h_refs[2 * k + half], lanes_of(half, k), (half, j * per_step + k)))
        return out

    def load_history(n_tiles):
        for _, h_ref, lanes, carry_at in slots(n_tiles):
            if tiles_per_seq > 1:
                first_of_seq = (i % tiles_per_seq) == 0

                @pl.when(first_of_seq)
                def _(h_ref=h_ref, lanes=lanes):
                    h_ref[CONV_HIST - lags:CONV_HIST, :] = state_ref[0, :, lanes]

                @pl.when(jnp.logical_not(first_of_seq))
                def _(h_ref=h_ref, carry_at=carry_at):
                    h_ref[0:CONV_HIST, :] = carry_ref[carry_at[0], carry_at[1]]
            else:
                for s in range(tm // seg):
                    top = h_row(s * seg)
                    h_ref[top - lags:top, :] = state_ref[s, :, lanes]

    def up_conv(r0, slot, xn):
        w_ref, h_ref, lanes, _ = slot
        taps = conv_ref[0:CONV_W, lanes]
        bias = conv_ref[CONV_W:CONV_W + 1, lanes]
        h = jnp.dot(xn, w_ref[...], preferred_element_type=F32)
        outs = []
        for p0 in range(0, chunk, piece):
            top = h_row(r0 + p0)
            hp = h[p0:p0 + piece]
            h_ref[top:top + piece, :] = hp
            out = taps[lags:lags + 1, :] * hp
            for tap in range(lags):
                lag = lags - tap
                out = out + taps[tap:tap + 1, :] * h_ref[top - lag:top - lag + piece, :]
            outs.append(out + bias)
            seg_end = r0 + p0 + piece
            if seg_end % seg == 0:
                tail_ref[seg_end // seg - 1, :, lanes] = hp[piece - lags:, :]
        return outs[0] if len(outs) == 1 else jnp.concatenate(outs, axis=0)

    def step(first, last, n_tiles):
        load_history(n_tiles)
        todo = slots(n_tiles)
        acts = []
        for r0 in range(0, tm, chunk):
            rows = slice(r0, r0 + chunk)
            if first:
                xn = _rmsnorm(x_ref[rows, :], g_ref[...]).astype(BF16)
                xn_ref[rows, :] = xn
            else:
                xn = xn_ref[rows, :]
            ups = [up_conv(r0, slot, xn) for slot in todo]
            acts.append((rows, [(ups[2 * k] * jax.nn.sigmoid(ups[2 * k]) * ups[2 * k + 1]).astype(BF16)
                                for k in range(n_tiles)]))
        for rows, chunk_acts in acts:
            partial = sum(jnp.dot(act, wd_refs[k][...], preferred_element_type=F32)
                          for k, act in enumerate(chunk_acts))
            if first:
                y_ref[rows, :] = partial
            elif last:
                y_ref[rows, :] = _rmsnorm(x_ref[rows, :] + (y_ref[rows, :] + partial), gf_ref[...])
            else:
                y_ref[rows, :] += partial
        if tiles_per_seq > 1:
            for _, h_ref, _, carry_at in todo:
                carry_ref[carry_at[0], carry_at[1]] = h_ref[tm:tm + CONV_HIST, :]

    last_j = pl.num_programs(1) - 1
    in_last = n_f - (n_f - 1) // per_step * per_step
    pl.when(j == 0)(functools.partial(step, True, False, per_step))
    pl.when(jnp.logical_and(j > 0, j < last_j))(functools.partial(step, False, False, per_step))
    pl.when(j == last_j)(functools.partial(step, False, True, in_last))


def _conv_ffn(x, seq_len, g, w_up, conv_w, conv_b, w_down, state, g_final):
    n, d = x.shape
    f = w_down.shape[0]
    tm = min(512, n)
    tf = FFN_TILE
    per_step = FFN_TILES_PER_STEP
    seg = min(seq_len, tm)
    n_seg = tm // seg
    tiles_per_seq = seq_len // seg
    assert f % tf == 0 and n % tm == 0 and tm % seg == 0 and seq_len % seg == 0
    n_f = f // tf
    n_steps = pl.cdiv(n_f, per_step)
    assert n_steps >= 2, "first and last step must be different grid steps"
    h_rows = tm + CONV_HIST * n_seg
    conv = jnp.concatenate([conv_w, conv_b], axis=0)
    f_tile = lambda j, k: jnp.minimum(j * per_step + k, n_f - 1)
    up_spec = lambda half, k: pl.BlockSpec((d, tf), lambda i, j: (0, half * n_f + f_tile(j, k)))
    down_spec = lambda k: pl.BlockSpec((tf, d), lambda i, j: (f_tile(j, k), 0))
    y, tails = pl.pallas_call(
        functools.partial(_ffn_kernel, tm=tm, seg=seg, chunk=min(ROW_CHUNK, tm), tiles_per_seq=tiles_per_seq,
                          n_f=n_f, per_step=per_step),
        grid=(n // tm, n_steps),
        in_specs=[
            pl.BlockSpec((tm, d), lambda i, j: (i, 0)),
            _const_spec((1, d)),
            *[up_spec(0, k) for k in range(per_step)], *[up_spec(1, k) for k in range(per_step)],
            _const_spec(conv.shape),
            *[down_spec(k) for k in range(per_step)],
            pl.BlockSpec((n_seg, CONV_W - 1, 2 * f), lambda i, j: (i // tiles_per_seq, 0, 0)),
            _const_spec((1, d)),
        ],
        out_specs=(
            pl.BlockSpec((tm, d), lambda i, j: (i, 0)),
            pl.BlockSpec((n_seg, CONV_W - 1, 2 * f), lambda i, j: (i, 0, 0)),
        ),
        out_shape=(
            jax.ShapeDtypeStruct((n, d), F32),
            jax.ShapeDtypeStruct((n // seg, CONV_W - 1, 2 * f), F32),
        ),
        scratch_shapes=[
            pltpu.VMEM((tm, d), BF16),
            *[pltpu.VMEM((h_rows, tf), F32) for _ in range(2 * per_step)],
            pltpu.VMEM((2, n_f, CONV_HIST, tf), F32),
        ],
        compiler_params=_params(("arbitrary", "arbitrary")),
        name="conv_ffn",
    )(x, g, *[w_up] * (2 * per_step), conv, *[w_down] * per_step, state, g_final)
    return y, tails[tiles_per_seq - 1::tiles_per_seq]


LATE_WEIGHTS = ("pool_w", "w_branch_pool", "w_branch_sb", "w_out", "xa_wq", "xa_wo", "ffn_w_up", "ffn_w_down")


def _layer(x, pool_state, past_k, past_v, mk, mv, conv_state, p, late=None):
    b, l, d = x.shape
    n = b * l
    assert l >= POOL_HIST and l >= CONV_W - 1
    x = x.reshape(n, d)
    u, q, kf, kb, vf, vb, gates = _in_proj(x, p["norm_mix_g"], p["w_in"], p["b_gate"])
    width = u.shape[1]
    if past_k is None:
        past = 0
        hist = jnp.zeros((b, POOL_HIST, width), F32)
    else:
        past = past_k.shape[1]
        hist = jnp.pad(pool_state, ((0, 0), (POOL_HIST - pool_state.shape[1], 0), (0, 0)))
    by_seq = lambda a: a.reshape(b, l, width)
    side = () if late is not None else tuple(p[name].reshape(-1, p[name].shape[-1]) for name in LATE_WEIGHTS)
    y_sb, casted = _sb_attention(by_seq(q), by_seq(kb), by_seq(vb), past_k, past_v, side)
    if late is None:
        late = {name: c.reshape(p[name].shape) for name, c in zip(LATE_WEIGHTS, casted)}

    x2 = _mix(x, l, u, y_sb.reshape(n, width), gates, hist, mk, mv,
              late["pool_w"], p["pool_scale"], late["w_branch_pool"], late["w_branch_sb"], late["w_out"],
              p["norm_xa_g"], late["xa_wq"], late["xa_wo"], past)

    y, new_conv = _conv_ffn(x2, l, p["norm_ffn_g"], late["ffn_w_up"], p["ffn_conv_w"],
                            p["ffn_conv_b"], late["ffn_w_down"], conv_state, p["norm_final_g"])
    new_pool = by_seq(u)[:, l - (POOL_HIST - 1):, :]
    return (y.reshape(b, l, d), new_pool, kf.reshape(b, l, SB_HEADS, HEAD_DIM),
            vf.reshape(b, l, SB_HEADS, HEAD_DIM), new_conv), late


def kernel(x_prompt, x_sample, mem_prompt, state_pool, cache_sb_k, cache_sb_v, cache_mem_k, cache_mem_v, state_ffn_conv, norm_mix_g, w_in, b_gate, pool_w, pool_scale, w_branch_pool, w_branch_sb, w_out, norm_xa_g, norm_mem_g, xa_wq, xa_wkv, xa_wo, norm_ffn_g, ffn_w_up, ffn_conv_w, ffn_conv_b, ffn_w_down, norm_final_g):
    assert norm_mix_g.shape[0] == 1, "single-layer step"
    bp, lp, d = x_prompt.shape
    n_mem = mem_prompt.shape[1]
    xa_width = xa_wq.shape[2]
    row = lambda a: a.reshape(1, -1)
    p = {
        "norm_mix_g": row(norm_mix_g[0]), "w_in": w_in[0].astype(BF16), "b_gate": row(b_gate[0]),
        "pool_w": pool_w[0], "pool_scale": row(pool_scale[0]),
        "w_branch_pool": w_branch_pool[0], "w_branch_sb": w_branch_sb[0],
        "w_out": w_out[0], "norm_xa_g": row(norm_xa_g[0]),
        "xa_wq": xa_wq[0], "xa_wo": xa_wo[0],
        "norm_ffn_g": row(norm_ffn_g[0]), "ffn_w_up": ffn_w_up[0],
        "ffn_conv_w": ffn_conv_w[0], "ffn_conv_b": row(ffn_conv_b[0]),
        "ffn_w_down": ffn_w_down[0], "norm_final_g": row(norm_final_g),
    }

    mem_kv = _norm_matmul(mem_prompt.reshape(bp * n_mem, d), row(norm_mem_g[0]), xa_wkv[0].astype(BF16))
    mem_kv = mem_kv.reshape(bp, n_mem, 2 * xa_width)
    mk_p, mv_p = mem_kv[..., :xa_width], mem_kv[..., xa_width:]

    conv0 = jnp.zeros((bp, CONV_W - 1, ffn_w_up.shape[2]), F32)
    (y_p, pool_p, k_p, v_p, conv_p), late = _layer(x_prompt, None, None, None, mk_p.astype(BF16),
                                                   mv_p.astype(BF16), conv0, p)

    bs = x_sample.shape[0]
    mk_s = cache_mem_k[0].reshape(bs, n_mem, xa_width).astype(BF16)
    mv_s = cache_mem_v[0].reshape(bs, n_mem, xa_width).astype(BF16)
    (y_s, pool_s, k_s, v_s, conv_s), _ = _layer(x_sample, state_pool[0], cache_sb_k[0], cache_sb_v[0], mk_s, mv_s,
                                                state_ffn_conv[0], p, late)

    mem_shape = (1, bp, n_mem, XA_HEADS, HEAD_DIM)
    return (y_p, y_s,
            pool_p[None], k_p[None], v_p[None], mk_p.reshape(mem_shape), mv_p.reshape(mem_shape), conv_p[None],
            pool_s[None], k_s[None], v_s[None], conv_s[None])
```

```python
import functools

import jax
import jax.numpy as jnp
import numpy as np
from jax import lax
from jax.experimental import pallas as pl
from jax.experimental.pallas import tpu as pltpu

F32 = jnp.float32
BF16 = jnp.bfloat16

EPS = 1e-6
POOL_WINDOWS = (2, 4, 8, 16)
POOL_HIST = 16
SB_HEADS = 8
HEAD_DIM = 128
XA_HEADS = 4
CONV_W = 3
CONV_HIST = 8
LANES = 128
FFN_TILE = 512
FFN_TILES_PER_STEP = 2
ROW_CHUNK = 256

SB_DEAD_LOG2 = -127.0
LOG2_E = 1.4426950408889634
SB_WALK_GROUP = 8

VMEM_LIMIT = 56 * 1024 * 1024


def _rmsnorm(xf, g):
    ms = jnp.mean(xf * xf, axis=-1, keepdims=True)
    return xf * lax.rsqrt(ms + EPS) * g


def _const_spec(shape):
    zeros = (0,) * len(shape)
    return pl.BlockSpec(shape, lambda *_: zeros, pipeline_mode=pl.Buffered(1))


def _params(semantics):
    return pltpu.CompilerParams(dimension_semantics=semantics, vmem_limit_bytes=VMEM_LIMIT)


def _in_proj_kernel(x_ref, g_ref, w_ref, b_ref, u_ref, q_ref, kf_ref, kb_ref, vf_ref, vb_ref, gate_ref,
                    *, q_scale, chunk):
    j = pl.program_id(0)
    tm = x_ref.shape[0]
    half = u_ref.shape[1]

    def project(epilogue):
        for r0 in range(0, tm, chunk):
            rows = slice(r0, r0 + chunk)
            xn = _rmsnorm(x_ref[rows, :], g_ref[...]).astype(BF16)
            epilogue(rows, jnp.dot(xn, w_ref[...], preferred_element_type=F32))

    def put_uq(rows, acc):
        u_ref[rows, :] = acc[:, :half]
        q_ref[rows, :] = (acc[:, half:] * q_scale).astype(BF16)

    def put_heads(f32_ref, bf16_ref, rows, acc):
        f32_ref[rows, :, :] = acc.reshape(acc.shape[0], f32_ref.shape[1], f32_ref.shape[2])
        bf16_ref[rows, :] = acc.astype(BF16)

    def put_kv(rows, acc):
        put_heads(kf_ref, kb_ref, rows, acc[:, :half])
        put_heads(vf_ref, vb_ref, rows, acc[:, half:])

    def put_gate(rows, acc):
        gate_ref[rows, :] = jax.nn.sigmoid(acc + b_ref[...]).astype(BF16)

    pl.when(j == 0)(functools.partial(project, put_uq))
    pl.when(j == 1)(functools.partial(project, put_kv))
    pl.when(j >= 2)(functools.partial(project, put_gate))


def _in_proj(x, g, w_in, b_gate):
    n, d = x.shape
    half = d // 2
    n_col = w_in.shape[1] // d
    heads = half // HEAD_DIM
    tm = min(512, n)
    n_tiles = n // tm

    def rows_while(active):
        return lambda j, i: jnp.where(j < active, 0, jnp.where(j == active, i, n_tiles - 1))

    def flat(active):
        tile = rows_while(active)
        return pl.BlockSpec((tm, half), lambda j, i: (tile(j, i), 0))

    def by_head(active):
        tile = rows_while(active)
        return pl.BlockSpec((tm, heads, HEAD_DIM), lambda j, i: (tile(j, i), 0, 0))

    gate_tile = lambda j, i: (jnp.where(j < 2, 0, i), jnp.maximum(j - 2, 0))
    out_shape = (
        jax.ShapeDtypeStruct((n, half), F32),
        jax.ShapeDtypeStruct((n, half), BF16),
        jax.ShapeDtypeStruct((n, heads, HEAD_DIM), F32),
        jax.ShapeDtypeStruct((n, half), BF16),
        jax.ShapeDtypeStruct((n, heads, HEAD_DIM), F32),
        jax.ShapeDtypeStruct((n, half), BF16),
        jax.ShapeDtypeStruct((n, 2 * d), BF16),
    )
    out_specs = (flat(0), flat(0), by_head(1), flat(1), by_head(1), flat(1), pl.BlockSpec((tm, d), gate_tile))
    return pl.pallas_call(
        functools.partial(_in_proj_kernel, q_scale=LOG2_E / np.sqrt(HEAD_DIM), chunk=min(ROW_CHUNK, tm)),
        grid=(n_col, n_tiles),
        in_specs=[
            pl.BlockSpec((tm, d), lambda j, i: (i, 0)),
            pl.BlockSpec((1, d), lambda j, i: (0, 0)),
            pl.BlockSpec((d, d), lambda j, i: (0, j)),
            pl.BlockSpec((1, d), lambda j, i: (0, jnp.maximum(j - 2, 0))),
        ],
        out_specs=out_specs,
        out_shape=out_shape,
        compiler_params=_params(("arbitrary", "arbitrary")),
        name="in_proj",
    )(x, g, w_in, b_gate)


def _norm_matmul_kernel(x_ref, g_ref, w_ref, o_ref):
    xn = _rmsnorm(x_ref[...], g_ref[...]).astype(BF16)
    o_ref[...] = jnp.dot(xn, w_ref[...], preferred_element_type=F32)


def _norm_matmul(x, g, w):
    n, d = x.shape
    tm = min(256, n)
    return pl.pallas_call(
        _norm_matmul_kernel,
        grid=(n // tm,),
        in_specs=[pl.BlockSpec((tm, d), lambda i: (i, 0)), _const_spec((1, d)), _const_spec(w.shape)],
        out_specs=pl.BlockSpec((tm, w.shape[1]), lambda i: (i, 0)),
        out_shape=jax.ShapeDtypeStruct((n, w.shape[1]), F32),
        compiler_params=_params(("arbitrary",)),
        name="mem_kv",
    )(x, g, w)


def _sb_kernel(*refs, blk, n_sub, n_heads, key_block_offset, has_past, side_blocks):
    n_side = len(side_blocks)
    n_in = len(refs) - 1 - n_side
    side_in, side_out = refs[n_in - n_side:n_in], refs[n_in + 1:]
    o_ref = refs[n_in]
    if has_past:
        q_ref, k_ref, v_ref, past_k_ref, past_v_ref, tri1_ref, tri2_ref = refs[:n_in - n_side]
    else:
        q_ref, k_ref, v_ref, tri1_ref, tri2_ref = refs[:n_in - n_side]
    step = pl.program_id(2)

    linear_step = (pl.program_id(0) * pl.num_programs(1) + pl.program_id(1)) * pl.num_programs(2) + step
    for src_ref, dst_ref, n_blocks in zip(side_in, side_out, side_blocks):
        @pl.when(linear_step < n_blocks)
        def _(src_ref=src_ref, dst_ref=dst_ref):
            dst_ref[...] = src_ref[...].astype(BF16)

    first_span = 2
    wide = first_span * blk
    col_minus_row = (lax.broadcasted_iota(jnp.int32, (blk, wide), 1)
                     - lax.broadcasted_iota(jnp.int32, (blk, wide), 0))
    chains = [(sub, h) for sub in range(n_sub) for h in range(n_heads)]
    rows = lambda sub: slice(sub * blk, (sub + 1) * blk)
    lanes = lambda h: slice(h * HEAD_DIM, (h + 1) * HEAD_DIM)
    qs = [q_ref[0, rows(sub), lanes(h)] for sub, h in chains]
    first_diag = step * n_sub + key_block_offset

    def visit(ids, newest, span, accs, runs, first):
        width = span * blk
        tri = (tri1_ref if span == 1 else tri2_ref)[...]
        log_nots, log_betas, vs, masks = [], [], [], []
        runs = list(runs)
        for n, c in enumerate(ids):
            sub, h = chains[c]
            kb_new = first_diag + sub - newest
            kb_old = kb_new - (span - 1)
            start_blk = jnp.maximum(kb_old, 0)
            if first:
                masks.append(col_minus_row[:, :width] < (kb_new - start_blk) * blk)
            else:
                runs[n] = jnp.where(kb_old >= 0, runs[n], -1e30)
            start = pl.multiple_of(start_blk * blk, blk)
            if not has_past:
                k = k_ref[0, pl.ds(start, width), lanes(h)]
                v = v_ref[0, pl.ds(start, width), lanes(h)]
            elif first:
                newest_past = slice((key_block_offset - 1) * blk, key_block_offset * blk)
                k = jnp.concatenate([past_k_ref[0, newest_past, h, :].astype(BF16), k_ref[0, :, lanes(h)]], axis=0)
                v = jnp.concatenate([past_v_ref[0, newest_past, h, :].astype(BF16), v_ref[0, :, lanes(h)]], axis=0)
            else:
                k = past_k_ref[0, pl.ds(start, width), h, :].astype(BF16)
                v = past_v_ref[0, pl.ds(start, width), h, :].astype(BF16)
            vs.append(v)
            z = lax.dot_general(qs[c], k, (((1,), (1,)), ((), ())), preferred_element_type=F32)
            t = jnp.log2(1.0 + jnp.exp2(-jnp.abs(z)))
            log_not = -(jnp.maximum(z, 0.0) + t)
            if first:
                log_not = jnp.where(masks[n], log_not, 0.0)
            log_nots.append(log_not)
            log_betas.append(jnp.minimum(z, 0.0) - t)
        stacked = jnp.concatenate(log_nots, axis=0)
        hi = stacked.astype(BF16)
        lo = (stacked - hi.astype(F32)).astype(BF16)
        sums = (jnp.dot(hi, tri, preferred_element_type=F32)
                + jnp.dot(lo, tri, preferred_element_type=F32))
        new_accs, new_runs = [], []
        for n in range(len(ids)):
            part = sums[n * blk:(n + 1) * blk]
            between = part[:, :width]
            total = part[:, width:]
            if first:
                a = jnp.where(masks[n], jnp.exp2(log_betas[n] + between), 0.0)
            else:
                a = jnp.exp2(log_betas[n] + between + runs[n][:, :width])
            new_accs.append(accs[n] + jnp.dot(a.astype(BF16), vs[n], preferred_element_type=F32))
            new_runs.append(runs[n] + total)
        return tuple(new_accs), tuple(new_runs)

    everyone = range(len(chains))
    zeros = tuple(jnp.zeros((blk, LANES), F32) for _ in chains)
    accs, runs = visit(everyone, 0, first_span, zeros, zeros, True)

    finished = []
    for g0 in range(0, len(chains), SB_WALK_GROUP):
        ids = range(g0, min(g0 + SB_WALK_GROUP, len(chains)))
        last_sub = chains[ids[-1]][0]

        def alive(carry, last_sub=last_sub):
            newest, _, runs = carry
            live = functools.reduce(jnp.maximum, runs)
            return jnp.logical_and(first_diag + last_sub - newest >= 0, jnp.max(live) > SB_DEAD_LOG2)

        def older(carry, ids=ids):
            newest, accs, runs = carry
            accs, runs = visit(ids, newest, 1, accs, runs, False)
            return newest + 1, accs, runs

        start = (jnp.int32(first_span), accs[g0:ids[-1] + 1], runs[g0:ids[-1] + 1])
        finished.extend(lax.while_loop(alive, older, start)[1])
    accs = finished
    for c, (sub, h) in enumerate(chains):
        o_ref[0, rows(sub), lanes(h)] = accs[c].astype(BF16)


def _suffix_sum_matrix(width):
    jj = np.arange(width)
    return jnp.asarray(np.concatenate([jj[:, None] > jj[None, :], np.ones((width, LANES), bool)], axis=1), BF16)


def _row_blocks(rows, max_blocks):
    units = rows // 16
    assert units * 16 == rows
    return max(nb for nb in range(1, min(units, max_blocks) + 1) if units % nb == 0)


def _sb_attention(q, k, v, past_k=None, past_v=None, side=()):
    b, lq, width = q.shape
    has_past = past_k is not None
    past = past_k.shape[1] if has_past else 0
    lk = past + lq
    blk = min(128, lq)
    assert lq % blk == 0 and past % blk == 0 and lk >= 2 * blk
    assert not has_past or lq == blk, "with a cache the new keys must form one block"
    n_q = lq // blk
    n_chains = 16
    n_sub = min(n_chains, n_q)
    n_heads = min(SB_HEADS, n_chains // n_sub)
    assert n_q % n_sub == 0 and SB_HEADS % n_heads == 0
    tri1, tri2 = _suffix_sum_matrix(blk), _suffix_sum_matrix(2 * blk)
    lane_w = n_heads * HEAD_DIM
    keys = pl.BlockSpec((1, lq, lane_w), lambda bi, h, i: (bi, 0, h))
    cache = [pl.BlockSpec((1, past, n_heads, HEAD_DIM), lambda bi, h, i: (bi, 0, h, 0))] * 2 if has_past else []
    grid = (b, SB_HEADS // n_heads, n_q // n_sub)
    n_steps = grid[0] * grid[1] * grid[2]
    side_blocks = tuple(_row_blocks(a.shape[0], n_steps) for a in side)

    def side_spec(a, n_blocks):
        return pl.BlockSpec((a.shape[0] // n_blocks, a.shape[1]),
                            lambda bi, h, i: (jnp.minimum((bi * grid[1] + h) * grid[2] + i, n_blocks - 1), 0))

    side_specs = [side_spec(a, nb) for a, nb in zip(side, side_blocks)]
    out = pl.pallas_call(
        functools.partial(_sb_kernel, blk=blk, n_sub=n_sub, n_heads=n_heads,
                          key_block_offset=past // blk, has_past=has_past, side_blocks=side_blocks),
        grid=grid,
        in_specs=[
            pl.BlockSpec((1, n_sub * blk, lane_w), lambda bi, h, i: (bi, i, h)),
            keys, keys, *cache,
            _const_spec(tri1.shape), _const_spec(tri2.shape),
            *side_specs,
        ],
        out_specs=(pl.BlockSpec((1, n_sub * blk, lane_w), lambda bi, h, i: (bi, i, h)), *side_specs),
        out_shape=(jax.ShapeDtypeStruct(q.shape, BF16), *(jax.ShapeDtypeStruct(a.shape, BF16) for a in side)),
        compiler_params=_params(("arbitrary", "arbitrary", "arbitrary")),
        name="sb_attn",
    )(q, k, v, *((past_k, past_v) if has_past else ()), tri1, tri2, *side)
    return out[0], out[1:]


def _mix_kernel(x_ref, u_ref, ysb_ref, gp_ref, gs_ref, hist_ref, mk_ref, mv_ref,
                pool_w_ref, pool_scale_ref, wbp_ref, wbs_ref, wout_ref, gxa_ref, wq_ref, wo_ref,
                o_ref, ubuf_ref, *, tm, seg, chunk, tiles_per_seq, pos0):
    i = pl.program_id(0)
    group = u_ref.shape[1] // len(POOL_WINDOWS)
    piece = min(chunk, seg)
    u_row = lambda r: r + POOL_HIST * (r // seg + 1)

    if tiles_per_seq > 1:
        tile_in_seq = i % tiles_per_seq

        @pl.when(tile_in_seq == 0)
        def _():
            ubuf_ref[0:POOL_HIST, :] = hist_ref[0]

        seq_row0 = tile_in_seq * tm
    else:
        for s in range(tm // seg):
            top = u_row(s * seg)
            ubuf_ref[top - POOL_HIST:top, :] = hist_ref[s]
        seq_row0 = 0

    for r0 in range(0, tm, chunk):
        rows = slice(r0, r0 + chunk)
        pieces = range(r0, r0 + chunk, piece)
        for r in pieces:
            ubuf_ref[u_row(r):u_row(r) + piece, :] = u_ref[r:r + piece, :]
        pooled = []
        for g, w in enumerate(POOL_WINDOWS):
            cols = slice(g * group, (g + 1) * group)
            deltas = []
            for r in pieces:
                top = u_row(r)
                u = u_ref[r:r + piece, cols]
                partial = ubuf_ref[top - POOL_HIST:top + piece, cols]
                span = 1
                while span < w:
                    partial = partial + pltpu.roll(partial, span, axis=0)
                    span *= 2
                window_sum = partial[POOL_HIST:]
                pos = pos0 + seq_row0 + r % seg + lax.broadcasted_iota(jnp.int32, (piece, 1), 0)
                count = jnp.minimum(w, pos + 1).astype(F32)
                deltas.append((window_sum / count - u).astype(BF16))
            delta = deltas[0] if len(deltas) == 1 else jnp.concatenate(deltas, axis=0)
            pooled.append(jnp.dot(delta, pool_w_ref[g], preferred_element_type=F32))
        y_pool = (jnp.concatenate(pooled, axis=-1) * pool_scale_ref[...]).astype(BF16)

        branch_pool = jnp.dot(y_pool, wbp_ref[...], preferred_element_type=F32)
        branch_sb = jnp.dot(ysb_ref[rows, :], wbs_ref[...], preferred_element_type=F32)
        merged = gp_ref[rows, :].astype(F32) * branch_pool + gs_ref[rows, :].astype(F32) * branch_sb
        x1 = x_ref[rows, :] + jnp.dot(merged.astype(BF16), wout_ref[...], preferred_element_type=F32)

        xn = _rmsnorm(x1, gxa_ref[...]).astype(BF16)
        q = (jnp.dot(xn, wq_ref[...], preferred_element_type=F32) * (1.0 / np.sqrt(HEAD_DIM))).astype(BF16)
        attended = []
        for r in pieces:
            s = r // seg
            head_cols = [slice(h * HEAD_DIM, (h + 1) * HEAD_DIM) for h in range(XA_HEADS)]
            scores = [lax.dot_general(q[r - r0:r - r0 + piece, cols], mk_ref[s, :, cols], (((1,), (1,)), ((), ())),
                                      preferred_element_type=F32) for cols in head_cols]
            probs = [jnp.exp(sc - jnp.max(sc, axis=-1, keepdims=True)) for sc in scores]
            norms = [jnp.sum(p, axis=-1, keepdims=True) for p in probs]
            heads = [jnp.dot(p.astype(BF16), mv_ref[s, :, cols], preferred_element_type=F32) / z
                     for p, z, cols in zip(probs, norms, head_cols)]
            attended.append(jnp.concatenate(heads, axis=-1).astype(BF16))
        attn = attended[0] if len(attended) == 1 else jnp.concatenate(attended, axis=0)
        o_ref[rows, :] = x1 + jnp.dot(attn, wo_ref[...], preferred_element_type=F32)

    if tiles_per_seq > 1:
        ubuf_ref[0:POOL_HIST, :] = ubuf_ref[tm:tm + POOL_HIST, :]


def _mix(x, seq_len, u, ysb, gates, hist, mk, mv, pool_w, pool_scale, wbp, wbs, wout, gxa, wq, wo, pos0):
    n, d = x.shape
    tm = min(256, n)
    seg = min(seq_len, tm)
    n_seg = tm // seg
    tiles_per_seq = seq_len // seg
    assert n % tm == 0 and tm % seg == 0 and seq_len % seg == 0
    tile = lambda i: (i, 0)
    per_seq = lambda i: (i // tiles_per_seq, 0, 0)
    return pl.pallas_call(
        functools.partial(_mix_kernel, tm=tm, seg=seg, chunk=tm, tiles_per_seq=tiles_per_seq,
                          pos0=pos0),
        grid=(n // tm,),
        in_specs=[
            pl.BlockSpec((tm, d), tile),
            pl.BlockSpec((tm, u.shape[1]), tile),
            pl.BlockSpec((tm, ysb.shape[1]), tile),
            pl.BlockSpec((tm, d), tile),
            pl.BlockSpec((tm, d), lambda i: (i, 1)),
            pl.BlockSpec((n_seg,) + hist.shape[1:], per_seq),
            pl.BlockSpec((n_seg,) + mk.shape[1:], per_seq),
            pl.BlockSpec((n_seg,) + mv.shape[1:], per_seq),
            _const_spec(pool_w.shape), _const_spec(pool_scale.shape), _const_spec(wbp.shape),
            _const_spec(wbs.shape), _const_spec(wout.shape), _const_spec(gxa.shape),
            _const_spec(wq.shape), _const_spec(wo.shape),
        ],
        out_specs=pl.BlockSpec((tm, d), tile),
        out_shape=jax.ShapeDtypeStruct(x.shape, F32),
        scratch_shapes=[pltpu.VMEM((tm + POOL_HIST * n_seg, u.shape[1]), F32)],
        compiler_params=_params(("arbitrary",)),
        name="mix",
    )(x, u, ysb, gates, gates, hist, mk, mv, pool_w, pool_scale, wbp, wbs, wout, gxa, wq, wo)


def _ffn_kernel(*refs, tm, seg, chunk, tiles_per_seq, n_f, per_step):
    x_ref, g_ref = refs[:2]
    wg_refs = refs[2:2 + per_step]
    wv_refs = refs[2 + per_step:2 + 2 * per_step]
    conv_ref = refs[2 + 2 * per_step]
    wd_refs = refs[3 + 2 * per_step:3 + 3 * per_step]
    state_ref, gf_ref, y_ref, tail_ref, xn_ref = refs[3 + 3 * per_step:8 + 3 * per_step]
    h_refs = refs[8 + 3 * per_step:8 + 5 * per_step]
    carry_ref = refs[8 + 5 * per_step]
    i = pl.program_id(0)
    j = pl.program_id(1)
    piece = min(chunk, seg)
    lags = CONV_W - 1
    tf = wg_refs[0].shape[1]
    h_row = lambda r: r + CONV_HIST * (r // seg + 1)

    def lanes_of(half, k):
        return pl.ds(pl.multiple_of((half * n_f + j * per_step + k) * tf, tf), tf)

    def slots(n_tiles):
        out = []
        for k in range(n_tiles):
            for half, w_ref in enumerate((wg_refs[k], wv_refs[k])):
                out.append((w_ref, h_refs[2 * k + half], lanes_of(half, k), (half, j * per_step + k)))
        return out

    def load_history(n_tiles):
        for _, h_ref, lanes, carry_at in slots(n_tiles):
            if tiles_per_seq > 1:
                first_of_seq = (i % tiles_per_seq) == 0

                @pl.when(first_of_seq)
                def _(h_ref=h_ref, lanes=lanes):
                    h_ref[CONV_HIST - lags:CONV_HIST, :] = state_ref[0, :, lanes]

                @pl.when(jnp.logical_not(first_of_seq))
                def _(h_ref=h_ref, carry_at=carry_at):
                    h_ref[0:CONV_HIST, :] = carry_ref[carry_at[0], carry_at[1]]
            else:
                for s in range(tm // seg):
                    top = h_row(s * seg)
                    h_ref[top - lags:top, :] = state_ref[s, :, lanes]

    def up_conv(r0, slot, xn):
        w_ref, h_ref, lanes, _ = slot
        taps = conv_ref[0:CONV_W, lanes]
        bias = conv_ref[CONV_W:CONV_W + 1, lanes]
        h = jnp.dot(xn, w_ref[...], preferred_element_type=F32)
        outs = []
        for p0 in range(0, chunk, piece):
            top = h_row(r0 + p0)
            hp = h[p0:p0 + piece]
            h_ref[top:top + piece, :] = hp
            out = taps[lags:lags + 1, :] * hp
            for tap in range(lags):
                lag = lags - tap
                out = out + taps[tap:tap + 1, :] * h_ref[top - lag:top - lag + piece, :]
            outs.append(out + bias)
            seg_end = r0 + p0 + piece
            if seg_end % seg == 0:
                tail_ref[seg_end // seg - 1, :, lanes] = hp[piece - lags:, :]
        return outs[0] if len(outs) == 1 else jnp.concatenate(outs, axis=0)

    def step(first, last, n_tiles):
        load_history(n_tiles)
        todo = slots(n_tiles)
        acts = []
        for r0 in range(0, tm, chunk):
            rows = slice(r0, r0 + chunk)
            if first:
                xn = _rmsnorm(x_ref[rows, :], g_ref[...]).astype(BF16)
                xn_ref[rows, :] = xn
            else:
                xn = xn_ref[rows, :]
            ups = [up_conv(r0, slot, xn) for slot in todo]
            acts.append((rows, [(ups[2 * k] * jax.nn.sigmoid(ups[2 * k]) * ups[2 * k + 1]).astype(BF16)
                                for k in range(n_tiles)]))
        for rows, chunk_acts in acts:
            partial = sum(jnp.dot(act, wd_refs[k][...], preferred_element_type=F32)
                          for k, act in enumerate(chunk_acts))
            if first:
                y_ref[rows, :] = partial
            elif last:
                y_ref[rows, :] = _rmsnorm(x_ref[rows, :] + (y_ref[rows, :] + partial), gf_ref[...])
            else:
                y_ref[rows, :] += partial
        if tiles_per_seq > 1:
            for _, h_ref, _, carry_at in todo:
                carry_ref[carry_at[0], carry_at[1]] = h_ref[tm:tm + CONV_HIST, :]

    last_j = pl.num_programs(1) - 1
    in_last = n_f - (n_f - 1) // per_step * per_step
    pl.when(j == 0)(functools.partial(step, True, False, per_step))
    pl.when(jnp.logical_and(j > 0, j < last_j))(functools.partial(step, False, False, per_step))
    pl.when(j == last_j)(functools.partial(step, False, True, in_last))


def _conv_ffn(x, seq_len, g, w_up, conv_w, conv_b, w_down, state, g_final):
    n, d = x.shape
    f = w_down.shape[0]
    tm = min(512, n)
    tf = FFN_TILE
    per_step = FFN_TILES_PER_STEP
    seg = min(seq_len, tm)
    n_seg = tm // seg
    tiles_per_seq = seq_len // seg
    assert f % tf == 0 and n % tm == 0 and tm % seg == 0 and seq_len % seg == 0
    n_f = f // tf
    n_steps = pl.cdiv(n_f, per_step)
    assert n_steps >= 2, "first and last step must be different grid steps"
    h_rows = tm + CONV_HIST * n_seg
    conv = jnp.concatenate([conv_w, conv_b], axis=0)
    f_tile = lambda j, k: jnp.minimum(j * per_step + k, n_f - 1)
    up_spec = lambda half, k: pl.BlockSpec((d, tf), lambda i, j: (0, half * n_f + f_tile(j, k)))
    down_spec = lambda k: pl.BlockSpec((tf, d), lambda i, j: (f_tile(j, k), 0))
    y, tails = pl.pallas_call(
        functools.partial(_ffn_kernel, tm=tm, seg=seg, chunk=min(ROW_CHUNK, tm), tiles_per_seq=tiles_per_seq,
                          n_f=n_f, per_step=per_step),
        grid=(n // tm, n_steps),
        in_specs=[
            pl.BlockSpec((tm, d), lambda i, j: (i, 0)),
            _const_spec((1, d)),
            *[up_spec(0, k) for k in range(per_step)], *[up_spec(1, k) for k in range(per_step)],
            _const_spec(conv.shape),
            *[down_spec(k) for k in range(per_step)],
            pl.BlockSpec((n_seg, CONV_W - 1, 2 * f), lambda i, j: (i // tiles_per_seq, 0, 0)),
            _const_spec((1, d)),
        ],
        out_specs=(
            pl.BlockSpec((tm, d), lambda i, j: (i, 0)),
            pl.BlockSpec((n_seg, CONV_W - 1, 2 * f), lambda i, j: (i, 0, 0)),
        ),
        out_shape=(
            jax.ShapeDtypeStruct((n, d), F32),
            jax.ShapeDtypeStruct((n // seg, CONV_W - 1, 2 * f), F32),
        ),
        scratch_shapes=[
            pltpu.VMEM((tm, d), BF16),
            *[pltpu.VMEM((h_rows, tf), F32) for _ in range(2 * per_step)],
            pltpu.VMEM((2, n_f, CONV_HIST, tf), F32),
        ],
        compiler_params=_params(("arbitrary", "arbitrary")),
        name="conv_ffn",
    )(x, g, *[w_up] * (2 * per_step), conv, *[w_down] * per_step, state, g_final)
    return y, tails[tiles_per_seq - 1::tiles_per_seq]


LATE_WEIGHTS = ("pool_w", "w_branch_pool", "w_branch_sb", "w_out", "xa_wq", "xa_wo", "ffn_w_up", "ffn_w_down")


def _layer(x, pool_state, past_k, past_v, mk, mv, conv_state, p, late=None):
    b, l, d = x.shape
    n = b * l
    assert l >= POOL_HIST and l >= CONV_W - 1
    x = x.reshape(n, d)
    u, q, kf, kb, vf, vb, gates = _in_proj(x, p["norm_mix_g"], p["w_in"], p["b_gate"])
    width = u.shape[1]
    if past_k is None:
        past = 0
        hist = jnp.zeros((b, POOL_HIST, width), F32)
    else:
        past = past_k.shape[1]
        hist = jnp.pad(pool_state, ((0, 0), (POOL_HIST - pool_state.shape[1], 0), (0, 0)))
    by_seq = lambda a: a.reshape(b, l, width)
    side = () if late is not None else tuple(p[name].reshape(-1, p[name].shape[-1]) for name in LATE_WEIGHTS)
    y_sb, casted = _sb_attention(by_seq(q), by_seq(kb), by_seq(vb), past_k, past_v, side)
    if late is None:
        late = {name: c.reshape(p[name].shape) for name, c in zip(LATE_WEIGHTS, casted)}

    x2 = _mix(x, l, u, y_sb.reshape(n, width), gates, hist, mk, mv,
              late["pool_w"], p["pool_scale"], late["w_branch_pool"], late["w_branch_sb"], late["w_out"],
              p["norm_xa_g"], late["xa_wq"], late["xa_wo"], past)

    y, new_conv = _conv_ffn(x2, l, p["norm_ffn_g"], late["ffn_w_up"], p["ffn_conv_w"],
                            p["ffn_conv_b"], late["ffn_w_down"], conv_state, p["norm_final_g"])
    new_pool = by_seq(u)[:, l - (POOL_HIST - 1):, :]
    return (y.reshape(b, l, d), new_pool, kf.reshape(b, l, SB_HEADS, HEAD_DIM),
            vf.reshape(b, l, SB_HEADS, HEAD_DIM), new_conv), late


def kernel(x_prompt, x_sample, mem_prompt, state_pool, cache_sb_k, cache_sb_v, cache_mem_k, cache_mem_v, state_ffn_conv, norm_mix_g, w_in, b_gate, pool_w, pool_scale, w_branch_pool, w_branch_sb, w_out, norm_xa_g, norm_mem_g, xa_wq, xa_wkv, xa_wo, norm_ffn_g, ffn_w_up, ffn_conv_w, ffn_conv_b, ffn_w_down, norm_final_g):
    assert norm_mix_g.shape[0] == 1, "single-layer step"
    bp, lp, d = x_prompt.shape
    n_mem = mem_prompt.shape[1]
    xa_width = xa_wq.shape[2]
    row = lambda a: a.reshape(1, -1)
    p = {
        "norm_mix_g": row(norm_mix_g[0]), "w_in": w_in[0].astype(BF16), "b_gate": row(b_gate[0]),
        "pool_w": pool_w[0], "pool_scale": row(pool_scale[0]),
        "w_branch_pool": w_branch_pool[0], "w_branch_sb": w_branch_sb[0],
        "w_out": w_out[0], "norm_xa_g": row(norm_xa_g[0]),
        "xa_wq": xa_wq[0], "xa_wo": xa_wo[0],
        "norm_ffn_g": row(norm_ffn_g[0]), "ffn_w_up": ffn_w_up[0],
        "ffn_conv_w": ffn_conv_w[0], "ffn_conv_b": row(ffn_conv_b[0]),
        "ffn_w_down": ffn_w_down[0], "norm_final_g": row(norm_final_g),
    }

    mem_kv = _norm_matmul(mem_prompt.reshape(bp * n_mem, d), row(norm_mem_g[0]), xa_wkv[0].astype(BF16))
    mem_kv = mem_kv.reshape(bp, n_mem, 2 * xa_width)
    mk_p, mv_p = mem_kv[..., :xa_width], mem_kv[..., xa_width:]

    conv0 = jnp.zeros((bp, CONV_W - 1, ffn_w_up.shape[2]), F32)
    (y_p, pool_p, k_p, v_p, conv_p), late = _layer(x_prompt, None, None, None, mk_p.astype(BF16),
                                                   mv_p.astype(BF16), conv0, p)

    bs = x_sample.shape[0]
    mk_s = cache_mem_k[0].reshape(bs, n_mem, xa_width).astype(BF16)
    mv_s = cache_mem_v[0].reshape(bs, n_mem, xa_width).astype(BF16)
    (y_s, pool_s, k_s, v_s, conv_s), _ = _layer(x_sample, state_pool[0], cache_sb_k[0], cache_sb_v[0], mk_s, mv_s,
                                                state_ffn_conv[0], p, late)

    mem_shape = (1, bp, n_mem, XA_HEADS, HEAD_DIM)
    return (y_p, y_s,
            pool_p[None], k_p[None], v_p[None], mk_p.reshape(mem_shape), mv_p.reshape(mem_shape), conv_p[None],
            pool_s[None], k_s[None], v_s[None], conv_s[None])
```

```python
import functools

import jax
import jax.numpy as jnp
import numpy as np
from jax import lax
from jax.experimental import pallas as pl
from jax.experimental.pallas import tpu as pltpu

F32 = jnp.float32
BF16 = jnp.bfloat16

EPS = 1e-6
POOL_WINDOWS = (2, 4, 8, 16)
POOL_HIST = 16
SB_HEADS = 8
HEAD_DIM = 128
XA_HEADS = 4
CONV_W = 3
CONV_HIST = 8
LANES = 128
FFN_TILE = 512
FFN_TILES_PER_STEP = 2
ROW_CHUNK = 256

SB_DEAD_LOG2 = -127.0
LOG2_E = 1.4426950408889634
SB_WALK_GROUP = 8

VMEM_LIMIT = 56 * 1024 * 1024


def _rmsnorm(xf, g):
    ms = jnp.mean(xf * xf, axis=-1, keepdims=True)
    return xf * lax.rsqrt(ms + EPS) * g


def _const_spec(shape):
    zeros = (0,) * len(shape)
    return pl.BlockSpec(shape, lambda *_: zeros, pipeline_mode=pl.Buffered(1))


def _params(semantics):
    return pltpu.CompilerParams(dimension_semantics=semantics, vmem_limit_bytes=VMEM_LIMIT)


def _in_proj_kernel(x_ref, g_ref, w_ref, b_ref, u_ref, q_ref, kf_ref, kb_ref, vf_ref, vb_ref, gate_ref,
                    *, q_scale, chunk):
    j = pl.program_id(0)
    tm = x_ref.shape[0]
    half = u_ref.shape[1]

    def project(epilogue):
        for r0 in range(0, tm, chunk):
            rows = slice(r0, r0 + chunk)
            xn = _rmsnorm(x_ref[rows, :], g_ref[...]).astype(BF16)
            epilogue(rows, jnp.dot(xn, w_ref[...], preferred_element_type=F32))

    def put_uq(rows, acc):
        u_ref[rows, :] = acc[:, :half]
        q_ref[rows, :] = (acc[:, half:] * q_scale).astype(BF16)

    def put_heads(f32_ref, bf16_ref, rows, acc):
        f32_ref[rows, :, :] = acc.reshape(acc.shape[0], f32_ref.shape[1], f32_ref.shape[2])
        bf16_ref[rows, :] = acc.astype(BF16)

    def put_kv(rows, acc):
        put_heads(kf_ref, kb_ref, rows, acc[:, :half])
        put_heads(vf_ref, vb_ref, rows, acc[:, half:])

    def put_gate(rows, acc):
        gate_ref[rows, :] = jax.nn.sigmoid(acc + b_ref[...]).astype(BF16)

    pl.when(j == 0)(functools.partial(project, put_uq))
    pl.when(j == 1)(functools.partial(project, put_kv))
    pl.when(j >= 2)(functools.partial(project, put_gate))


def _in_proj(x, g, w_in, b_gate):
    n, d = x.shape
    half = d // 2
    n_col = w_in.shape[1] // d
    heads = half // HEAD_DIM
    tm = min(512, n)
    n_tiles = n // tm

    def rows_while(active):
        return lambda j, i: jnp.where(j < active, 0, jnp.where(j == active, i, n_tiles - 1))

    def flat(active):
        tile = rows_while(active)
        return pl.BlockSpec((tm, half), lambda j, i: (tile(j, i), 0))

    def by_head(active):
        tile = rows_while(active)
        return pl.BlockSpec((tm, heads, HEAD_DIM), lambda j, i: (tile(j, i), 0, 0))

    gate_tile = lambda j, i: (jnp.where(j < 2, 0, i), jnp.maximum(j - 2, 0))
    out_shape = (
        jax.ShapeDtypeStruct((n, half), F32),
        jax.ShapeDtypeStruct((n, half), BF16),
        jax.ShapeDtypeStruct((n, heads, HEAD_DIM), F32),
        jax.ShapeDtypeStruct((n, half), BF16),
        jax.ShapeDtypeStruct((n, heads, HEAD_DIM), F32),
        jax.ShapeDtypeStruct((n, half), BF16),
        jax.ShapeDtypeStruct((n, 2 * d), BF16),
    )
    out_specs = (flat(0), flat(0), by_head(1), flat(1), by_head(1), flat(1), pl.BlockSpec((tm, d), gate_tile))
    return pl.pallas_call(
        functools.partial(_in_proj_kernel, q_scale=LOG2_E / np.sqrt(HEAD_DIM), chunk=min(ROW_CHUNK, tm)),
        grid=(n_col, n_tiles),
        in_specs=[
            pl.BlockSpec((tm, d), lambda j, i: (i, 0)),
            pl.BlockSpec((1, d), lambda j, i: (0, 0)),
            pl.BlockSpec((d, d), lambda j, i: (0, j)),
            pl.BlockSpec((1, d), lambda j, i: (0, jnp.maximum(j - 2, 0))),
        ],
        out_specs=out_specs,
        out_shape=out_shape,
        compiler_params=_params(("arbitrary", "arbitrary")),
        name="in_proj",
    )(x, g, w_in, b_gate)


def _norm_matmul_kernel(x_ref, g_ref, w_ref, o_ref):
    xn = _rmsnorm(x_ref[...], g_ref[...]).astype(BF16)
    o_ref[...] = jnp.dot(xn, w_ref[...], preferred_element_type=F32)


def _norm_matmul(x, g, w):
    n, d = x.shape
    tm = min(256, n)
    return pl.pallas_call(
        _norm_matmul_kernel,
        grid=(n // tm,),
        in_specs=[pl.BlockSpec((tm, d), lambda i: (i, 0)), _const_spec((1, d)), _const_spec(w.shape)],
        out_specs=pl.BlockSpec((tm, w.shape[1]), lambda i: (i, 0)),
        out_shape=jax.ShapeDtypeStruct((n, w.shape[1]), F32),
        compiler_params=_params(("arbitrary",)),
        name="mem_kv",
    )(x, g, w)


def _sb_kernel(*refs, blk, n_sub, n_heads, key_block_offset, has_past, side_blocks):
    n_side = len(side_blocks)
    n_in = len(refs) - 1 - n_side
    side_in, side_out = refs[n_in - n_side:n_in], refs[n_in + 1:]
    o_ref = refs[n_in]
    if has_past:
        q_ref, k_ref, v_ref, past_k_ref, past_v_ref, tri1_ref, tri2_ref = refs[:n_in - n_side]
    else:
        q_ref, k_ref, v_ref, tri1_ref, tri2_ref = refs[:n_in - n_side]
    step = pl.program_id(2)

    linear_step = (pl.program_id(0) * pl.num_programs(1) + pl.program_id(1)) * pl.num_programs(2) + step
    for src_ref, dst_ref, n_blocks in zip(side_in, side_out, side_blocks):
        @pl.when(linear_step < n_blocks)
        def _(src_ref=src_ref, dst_ref=dst_ref):
            dst_ref[...] = src_ref[...].astype(BF16)

    first_span = 2
    wide = first_span * blk
    col_minus_row = (lax.broadcasted_iota(jnp.int32, (blk, wide), 1)
                     - lax.broadcasted_iota(jnp.int32, (blk, wide), 0))
    chains = [(sub, h) for sub in range(n_sub) for h in range(n_heads)]
    rows = lambda sub: slice(sub * blk, (sub + 1) * blk)
    lanes = lambda h: slice(h * HEAD_DIM, (h + 1) * HEAD_DIM)
    qs = [q_ref[0, rows(sub), lanes(h)] for sub, h in chains]
    first_diag = step * n_sub + key_block_offset

    def visit(ids, newest, span, accs, runs, first):
        width = span * blk
        tri = (tri1_ref if span == 1 else tri2_ref)[...]
        log_nots, log_betas, vs, masks = [], [], [], []
        runs = list(runs)
        for n, c in enumerate(ids):
            sub, h = chains[c]
            kb_new = first_diag + sub - newest
            kb_old = kb_new - (span - 1)
            start_blk = jnp.maximum(kb_old, 0)
            if first:
                masks.append(col_minus_row[:, :width] < (kb_new - start_blk) * blk)
            else:
                runs[n] = jnp.where(kb_old >= 0, runs[n], -1e30)
            start = pl.multiple_of(start_blk * blk, blk)
            if not has_past:
                k = k_ref[0, pl.ds(start, width), lanes(h)]
                v = v_ref[0, pl.ds(start, width), lanes(h)]
            elif first:
                newest_past = slice((key_block_offset - 1) * blk, key_block_offset * blk)
                k = jnp.concatenate([past_k_ref[0, newest_past, h, :].astype(BF16), k_ref[0, :, lanes(h)]], axis=0)
                v = jnp.concatenate([past_v_ref[0, newest_past, h, :].astype(BF16), v_ref[0, :, lanes(h)]], axis=0)
            else:
                k = past_k_ref[0, pl.ds(start, width), h, :].astype(BF16)
                v = past_v_ref[0, pl.ds(start, width), h, :].astype(BF16)
            vs.append(v)
            z = lax.dot_general(qs[c], k, (((1,), (1,)), ((), ())), preferred_element_type=F32)
            t = jnp.log2(1.0 + jnp.exp2(-jnp.abs(z)))
            log_not = -(jnp.maximum(z, 0.0) + t)
            if first:
                log_not = jnp.where(masks[n], log_not, 0.0)
            log_nots.append(log_not)
            log_betas.append(jnp.minimum(z, 0.0) - t)
        stacked = jnp.concatenate(log_nots, axis=0)
        hi = stacked.astype(BF16)
        lo = (stacked - hi.astype(F32)).astype(BF16)
        sums = (jnp.dot(hi, tri, preferred_element_type=F32)
                + jnp.dot(lo, tri, preferred_element_type=F32))
        new_accs, new_runs = [], []
        for n in range(len(ids)):
            part = sums[n * blk:(n + 1) * blk]
            between = part[:, :width]
            total = part[:, width:]
            if first:
                a = jnp.where(masks[n], jnp.exp2(log_betas[n] + between), 0.0)
            else:
                a = jnp.exp2(log_betas[n] + between + runs[n][:, :width])
            new_accs.append(accs[n] + jnp.dot(a.astype(BF16), vs[n], preferred_element_type=F32))
            new_runs.append(runs[n] + total)
        return tuple(new_accs), tuple(new_runs)

    everyone = range(len(chains))
    zeros = tuple(jnp.zeros((blk, LANES), F32) for _ in chains)
    accs, runs = visit(everyone, 0, first_span, zeros, zeros, True)

    finished = []
    for g0 in range(0, len(chains), SB_WALK_GROUP):
        ids = range(g0, min(g0 + SB_WALK_GROUP, len(chains)))
        last_sub = chains[ids[-1]][0]

        def alive(carry, last_sub=last_sub):
            newest, _, runs = carry
            live = functools.reduce(jnp.maximum, runs)
            return jnp.logical_and(first_diag + last_sub - newest >= 0, jnp.max(live) > SB_DEAD_LOG2)

        def older(carry, ids=ids):
            newest, accs, runs = carry
            accs, runs = visit(ids, newest, 1, accs, runs, False)
            return newest + 1, accs, runs

        start = (jnp.int32(first_span), accs[g0:ids[-1] + 1], runs[g0:ids[-1] + 1])
        finished.extend(lax.while_loop(alive, older, start)[1])
    accs = finished
    for c, (sub, h) in enumerate(chains):
        o_ref[0, rows(sub), lanes(h)] = accs[c].astype(BF16)


def _suffix_sum_matrix(width):
    jj = np.arange(width)
    return jnp.asarray(np.concatenate([jj[:, None] > jj[None, :], np.ones((width, LANES), bool)], axis=1), BF16)


def _row_blocks(rows, max_blocks):
    units = rows // 16
    assert units * 16 == rows
    return max(nb for nb in range(1, min(units, max_blocks) + 1) if units % nb == 0)


def _sb_attention(q, k, v, past_k=None, past_v=None, side=()):
    b, lq, width = q.shape
    has_past = past_k is not None
    past = past_k.shape[1] if has_past else 0
    lk = past + lq
    blk = min(128, lq)
    assert lq % blk == 0 and past % blk == 0 and lk >= 2 * blk
    assert not has_past or lq == blk, "with a cache the new keys must form one block"
    n_q = lq // blk
    n_chains = 32
    n_sub = min(n_chains, n_q)
    n_heads = min(SB_HEADS, n_chains // n_sub)
    assert n_q % n_sub == 0 and SB_HEADS % n_heads == 0
    tri1, tri2 = _suffix_sum_matrix(blk), _suffix_sum_matrix(2 * blk)
    lane_w = n_heads * HEAD_DIM
    keys = pl.BlockSpec((1, lq, lane_w), lambda bi, h, i: (bi, 0, h))
    cache = [pl.BlockSpec((1, past, n_heads, HEAD_DIM), lambda bi, h, i: (bi, 0, h, 0))] * 2 if has_past else []
    grid = (b, SB_HEADS // n_heads, n_q // n_sub)
    n_steps = grid[0] * grid[1] * grid[2]
    side_blocks = tuple(_row_blocks(a.shape[0], n_steps) for a in side)

    def side_spec(a, n_blocks):
        return pl.BlockSpec((a.shape[0] // n_blocks, a.shape[1]),
                            lambda bi, h, i: (jnp.minimum((bi * grid[1] + h) * grid[2] + i, n_blocks - 1), 0))

    side_specs = [side_spec(a, nb) for a, nb in zip(side, side_blocks)]
    out = pl.pallas_call(
        functools.partial(_sb_kernel, blk=blk, n_sub=n_sub, n_heads=n_heads,
                          key_block_offset=past // blk, has_past=has_past, side_blocks=side_blocks),
        grid=grid,
        in_specs=[
            pl.BlockSpec((1, n_sub * blk, lane_w), lambda bi, h, i: (bi, i, h)),
            keys, keys, *cache,
            _const_spec(tri1.shape), _const_spec(tri2.shape),
            *side_specs,
        ],
        out_specs=(pl.BlockSpec((1, n_sub * blk, lane_w), lambda bi, h, i: (bi, i, h)), *side_specs),
        out_shape=(jax.ShapeDtypeStruct(q.shape, BF16), *(jax.ShapeDtypeStruct(a.shape, BF16) for a in side)),
        compiler_params=_params(("arbitrary", "arbitrary", "arbitrary")),
        name="sb_attn",
    )(q, k, v, *((past_k, past_v) if has_past else ()), tri1, tri2, *side)
    return out[0], out[1:]


def _mix_kernel(x_ref, u_ref, ysb_ref, gp_ref, gs_ref, hist_ref, mk_ref, mv_ref,
                pool_w_ref, pool_scale_ref, wbp_ref, wbs_ref, wout_ref, gxa_ref, wq_ref, wo_ref,
                o_ref, ubuf_ref, *, tm, seg, chunk, tiles_per_seq, pos0):
    i = pl.program_id(0)
    group = u_ref.shape[1] // len(POOL_WINDOWS)
    piece = min(chunk, seg)
    u_row = lambda r: r + POOL_HIST * (r // seg + 1)

    if tiles_per_seq > 1:
        tile_in_seq = i % tiles_per_seq

        @pl.when(tile_in_seq == 0)
        def _():
            ubuf_ref[0:POOL_HIST, :] = hist_ref[0]

        seq_row0 = tile_in_seq * tm
    else:
        for s in range(tm // seg):
            top = u_row(s * seg)
            ubuf_ref[top - POOL_HIST:top, :] = hist_ref[s]
        seq_row0 = 0

    for r0 in range(0, tm, chunk):
        rows = slice(r0, r0 + chunk)
        pieces = range(r0, r0 + chunk, piece)
        for r in pieces:
            ubuf_ref[u_row(r):u_row(r) + piece, :] = u_ref[r:r + piece, :]
        pooled = []
        for g, w in enumerate(POOL_WINDOWS):
            cols = slice(g * group, (g + 1) * group)
            deltas = []
            for r in pieces:
                top = u_row(r)
                u = u_ref[r:r + piece, cols]
                partial = ubuf_ref[top - POOL_HIST:top + piece, cols]
                span = 1
                while span < w:
                    partial = partial + pltpu.roll(partial, span, axis=0)
                    span *= 2
                window_sum = partial[POOL_HIST:]
                pos = pos0 + seq_row0 + r % seg + lax.broadcasted_iota(jnp.int32, (piece, 1), 0)
                count = jnp.minimum(w, pos + 1).astype(F32)
                deltas.append((window_sum / count - u).astype(BF16))
            delta = deltas[0] if len(deltas) == 1 else jnp.concatenate(deltas, axis=0)
            pooled.append(jnp.dot(delta, pool_w_ref[g], preferred_element_type=F32))
        y_pool = (jnp.concatenate(pooled, axis=-1) * pool_scale_ref[...]).astype(BF16)

        branch_pool = jnp.dot(y_pool, wbp_ref[...], preferred_element_type=F32)
        branch_sb = jnp.dot(ysb_ref[rows, :], wbs_ref[...], preferred_element_type=F32)
        merged = gp_ref[rows, :].astype(F32) * branch_pool + gs_ref[rows, :].astype(F32) * branch_sb
        x1 = x_ref[rows, :] + jnp.dot(merged.astype(BF16), wout_ref[...], preferred_element_type=F32)

        xn = _rmsnorm(x1, gxa_ref[...]).astype(BF16)
        q = (jnp.dot(xn, wq_ref[...], preferred_element_type=F32) * (1.0 / np.sqrt(HEAD_DIM))).astype(BF16)
        attended = []
        for r in pieces:
            s = r // seg
            head_cols = [slice(h * HEAD_DIM, (h + 1) * HEAD_DIM) for h in range(XA_HEADS)]
            scores = [lax.dot_general(q[r - r0:r - r0 + piece, cols], mk_ref[s, :, cols], (((1,), (1,)), ((), ())),
                                      preferred_element_type=F32) for cols in head_cols]
            probs = [jnp.exp(sc - jnp.max(sc, axis=-1, keepdims=True)) for sc in scores]
            norms = [jnp.sum(p, axis=-1, keepdims=True) for p in probs]
            heads = [jnp.dot(p.astype(BF16), mv_ref[s, :, cols], preferred_element_type=F32) / z
                     for p, z, cols in zip(probs, norms, head_cols)]
            attended.append(jnp.concatenate(heads, axis=-1).astype(BF16))
        attn = attended[0] if len(attended) == 1 else jnp.concatenate(attended, axis=0)
        o_ref[rows, :] = x1 + jnp.dot(attn, wo_ref[...], preferred_element_type=F32)

    if tiles_per_seq > 1:
        ubuf_ref[0:POOL_HIST, :] = ubuf_ref[tm:tm + POOL_HIST, :]


def _mix(x, seq_len, u, ysb, gates, hist, mk, mv, pool_w, pool_scale, wbp, wbs, wout, gxa, wq, wo, pos0):
    n, d = x.shape
    tm = min(256, n)
    seg = min(seq_len, tm)
    n_seg = tm // seg
    tiles_per_seq = seq_len // seg
    assert n % tm == 0 and tm % seg == 0 and seq_len % seg == 0
    tile = lambda i: (i, 0)
    per_seq = lambda i: (i // tiles_per_seq, 0, 0)
    return pl.pallas_call(
        functools.partial(_mix_kernel, tm=tm, seg=seg, chunk=tm, tiles_per_seq=tiles_per_seq,
                          pos0=pos0),
        grid=(n // tm,),
        in_specs=[
            pl.BlockSpec((tm, d), tile),
            pl.BlockSpec((tm, u.shape[1]), tile),
            pl.BlockSpec((tm, ysb.shape[1]), tile),
            pl.BlockSpec((tm, d), tile),
            pl.BlockSpec((tm, d), lambda i: (i, 1)),
            pl.BlockSpec((n_seg,) + hist.shape[1:], per_seq),
            pl.BlockSpec((n_seg,) + mk.shape[1:], per_seq),
            pl.BlockSpec((n_seg,) + mv.shape[1:], per_seq),
            _const_spec(pool_w.shape), _const_spec(pool_scale.shape), _const_spec(wbp.shape),
            _const_spec(wbs.shape), _const_spec(wout.shape), _const_spec(gxa.shape),
            _const_spec(wq.shape), _const_spec(wo.shape),
        ],
        out_specs=pl.BlockSpec((tm, d), tile),
        out_shape=jax.ShapeDtypeStruct(x.shape, F32),
        scratch_shapes=[pltpu.VMEM((tm + POOL_HIST * n_seg, u.shape[1]), F32)],
        compiler_params=_params(("arbitrary",)),
        name="mix",
    )(x, u, ysb, gates, gates, hist, mk, mv, pool_w, pool_scale, wbp, wbs, wout, gxa, wq, wo)


def _ffn_kernel(*refs, tm, seg, chunk, tiles_per_seq, n_f, per_step):
    x_ref, g_ref = refs[:2]
    wg_refs = refs[2:2 + per_step]
    wv_refs = refs[2 + per_step:2 + 2 * per_step]
    conv_ref = refs[2 + 2 * per_step]
    wd_refs = refs[3 + 2 * per_step:3 + 3 * per_step]
    state_ref, gf_ref, y_ref, tail_ref, xn_ref = refs[3 + 3 * per_step:8 + 3 * per_step]
    h_refs = refs[8 + 3 * per_step:8 + 5 * per_step]
    carry_ref = refs[8 + 5 * per_step]
    i = pl.program_id(0)
    j = pl.program_id(1)
    piece = min(chunk, seg)
    lags = CONV_W - 1
    tf = wg_refs[0].shape[1]
    h_row = lambda r: r + CONV_HIST * (r // seg + 1)

    def lanes_of(half, k):
        return pl.ds(pl.multiple_of((half * n_f + j * per_step + k) * tf, tf), tf)

    def slots(n_tiles):
        out = []
        for k in range(n_tiles):
            for half, w_ref in enumerate((wg_refs[k], wv_refs[k])):
                out.append((w_ref, h_refs[2 * k + half], lanes_of(half, k), (half, j * per_step + k)))
        return out

    def load_history(n_tiles):
        for _, h_ref, lanes, carry_at in slots(n_tiles):
            if tiles_per_seq > 1:
                first_of_seq = (i % tiles_per_seq) == 0

                @pl.when(first_of_seq)
                def _(h_ref=h_ref, lanes=lanes):
                    h_ref[CONV_HIST - lags:CONV_HIST, :] = state_ref[0, :, lanes]

                @pl.when(jnp.logical_not(first_of_seq))
                def _(h_ref=h_ref, carry_at=carry_at):
                    h_ref[0:CONV_HIST, :] = carry_ref[carry_at[0], carry_at[1]]
            else:
                for s in range(tm // seg):
                    top = h_row(s * seg)
                    h_ref[top - lags:top, :] = state_ref[s, :, lanes]

    def up_conv(r0, slot, xn):
        w_ref, h_ref, lanes, _ = slot
        taps = conv_ref[0:CONV_W, lanes]
        bias = conv_ref[CONV_W:CONV_W + 1, lanes]
        h = jnp.dot(xn, w_ref[...], preferred_element_type=F32)
        outs = []
        for p0 in range(0, chunk, piece):
            top = h_row(r0 + p0)
            hp = h[p0:p0 + piece]
            h_ref[top:top + piece, :] = hp
            out = taps[lags:lags + 1, :] * hp
            for tap in range(lags):
                lag = lags - tap
                out = out + taps[tap:tap + 1, :] * h_ref[top - lag:top - lag + piece, :]
            outs.append(out + bias)
            seg_end = r0 + p0 + piece
            if seg_end % seg == 0:
                tail_ref[seg_end // seg - 1, :, lanes] = hp[piece - lags:, :]
        return outs[0] if len(outs) == 1 else jnp.concatenate(outs, axis=0)

    def step(first, last, n_tiles):
        load_history(n_tiles)
        todo = slots(n_tiles)
        acts = []
        for r0 in range(0, tm, chunk):
            rows = slice(r0, r0 + chunk)
            if first:
                xn = _rmsnorm(x_ref[rows, :], g_ref[...]).astype(BF16)
                xn_ref[rows, :] = xn
            else:
                xn = xn_ref[rows, :]
            ups = [up_conv(r0, slot, xn) for slot in todo]
            acts.append((rows, [(ups[2 * k] * jax.nn.sigmoid(ups[2 * k]) * ups[2 * k + 1]).astype(BF16)
                                for k in range(n_tiles)]))
        for rows, chunk_acts in acts:
            partial = sum(jnp.dot(act, wd_refs[k][...], preferred_element_type=F32)
                          for k, act in enumerate(chunk_acts))
            if first:
                y_ref[rows, :] = partial
            elif last:
                y_ref[rows, :] = _rmsnorm(x_ref[rows, :] + (y_ref[rows, :] + partial), gf_ref[...])
            else:
                y_ref[rows, :] += partial
        if tiles_per_seq > 1:
            for _, h_ref, _, carry_at in todo:
                carry_ref[carry_at[0], carry_at[1]] = h_ref[tm:tm + CONV_HIST, :]

    last_j = pl.num_programs(1) - 1
    in_last = n_f - (n_f - 1) // per_step * per_step
    pl.when(j == 0)(functools.partial(step, True, False, per_step))
    pl.when(jnp.logical_and(j > 0, j < last_j))(functools.partial(step, False, False, per_step))
    pl.when(j == last_j)(functools.partial(step, False, True, in_last))


def _conv_ffn(x, seq_len, g, w_up, conv_w, conv_b, w_down, state, g_final):
    n, d = x.shape
    f = w_down.shape[0]
    tm = min(512, n)
    tf = FFN_TILE
    per_step = FFN_TILES_PER_STEP
    seg = min(seq_len, tm)
    n_seg = tm // seg
    tiles_per_seq = seq_len // seg
    assert f % tf == 0 and n % tm == 0 and tm % seg == 0 and seq_len % seg == 0
    n_f = f // tf
    n_steps = pl.cdiv(n_f, per_step)
    assert n_steps >= 2, "first and last step must be different grid steps"
    h_rows = tm + CONV_HIST * n_seg
    conv = jnp.concatenate([conv_w, conv_b], axis=0)
    f_tile = lambda j, k: jnp.minimum(j * per_step + k, n_f - 1)
    up_spec = lambda half, k: pl.BlockSpec((d, tf), lambda i, j: (0, half * n_f + f_tile(j, k)))
    down_spec = lambda k: pl.BlockSpec((tf, d), lambda i, j: (f_tile(j, k), 0))
    y, tails = pl.pallas_call(
        functools.partial(_ffn_kernel, tm=tm, seg=seg, chunk=min(ROW_CHUNK, tm), tiles_per_seq=tiles_per_seq,
                          n_f=n_f, per_step=per_step),
        grid=(n // tm, n_steps),
        in_specs=[
            pl.BlockSpec((tm, d), lambda i, j: (i, 0)),
            _const_spec((1, d)),
            *[up_spec(0, k) for k in range(per_step)], *[up_spec(1, k) for k in range(per_step)],
            _const_spec(conv.shape),
            *[down_spec(k) for k in range(per_step)],
            pl.BlockSpec((n_seg, CONV_W - 1, 2 * f), lambda i, j: (i // tiles_per_seq, 0, 0)),
            _const_spec((1, d)),
        ],
        out_specs=(
            pl.BlockSpec((tm, d), lambda i, j: (i, 0)),
            pl.BlockSpec((n_seg, CONV_W - 1, 2 * f), lambda i, j: (i, 0, 0)),
        ),
        out_shape=(
            jax.ShapeDtypeStruct((n, d), F32),
            jax.ShapeDtypeStruct((n // seg, CONV_W - 1, 2 * f), F32),
        ),
        scratch_shapes=[
            pltpu.VMEM((tm, d), BF16),
            *[pltpu.VMEM((h_rows, tf), F32) for _ in range(2 * per_step)],
            pltpu.VMEM((2, n_f, CONV_HIST, tf), F32),
        ],
        compiler_params=_params(("arbitrary", "arbitrary")),
        name="conv_ffn",
    )(x, g, *[w_up] * (2 * per_step), conv, *[w_down] * per_step, state, g_final)
    return y, tails[tiles_per_seq - 1::tiles_per_seq]


LATE_WEIGHTS = ("pool_w", "w_branch_pool", "w_branch_sb", "w_out", "xa_wq", "xa_wo", "ffn_w_up", "ffn_w_down")


def _layer(x, pool_state, past_k, past_v, mk, mv, conv_state, p, late=None):
    b, l, d = x.shape
    n = b * l
    assert l >= POOL_HIST and l >= CONV_W - 1
    x = x.reshape(n, d)
    u, q, kf, kb, vf, vb, gates = _in_proj(x, p["norm_mix_g"], p["w_in"], p["b_gate"])
    width = u.shape[1]
    if past_k is None:
        past = 0
        hist = jnp.zeros((b, POOL_HIST, width), F32)
    else:
        past = past_k.shape[1]
        hist = jnp.pad(pool_state, ((0, 0), (POOL_HIST - pool_state.shape[1], 0), (0, 0)))
    by_seq = lambda a: a.reshape(b, l, width)
    side = () if late is not None else tuple(p[name].reshape(-1, p[name].shape[-1]) for name in LATE_WEIGHTS)
    y_sb, casted = _sb_attention(by_seq(q), by_seq(kb), by_seq(vb), past_k, past_v, side)
    if late is None:
        late = {name: c.reshape(p[name].shape) for name, c in zip(LATE_WEIGHTS, casted)}

    x2 = _mix(x, l, u, y_sb.reshape(n, width), gates, hist, mk, mv,
              late["pool_w"], p["pool_scale"], late["w_branch_pool"], late["w_branch_sb"], late["w_out"],
              p["norm_xa_g"], late["xa_wq"], late["xa_wo"], past)

    y, new_conv = _conv_ffn(x2, l, p["norm_ffn_g"], late["ffn_w_up"], p["ffn_conv_w"],
                            p["ffn_conv_b"], late["ffn_w_down"], conv_state, p["norm_final_g"])
    new_pool = by_seq(u)[:, l - (POOL_HIST - 1):, :]
    return (y.reshape(b, l, d), new_pool, kf.reshape(b, l, SB_HEADS, HEAD_DIM),
            vf.reshape(b, l, SB_HEADS, HEAD_DIM), new_conv), late


def kernel(x_prompt, x_sample, mem_prompt, state_pool, cache_sb_k, cache_sb_v, cache_mem_k, cache_mem_v, state_ffn_conv, norm_mix_g, w_in, b_gate, pool_w, pool_scale, w_branch_pool, w_branch_sb, w_out, norm_xa_g, norm_mem_g, xa_wq, xa_wkv, xa_wo, norm_ffn_g, ffn_w_up, ffn_conv_w, ffn_conv_b, ffn_w_down, norm_final_g):
    assert norm_mix_g.shape[0] == 1, "single-layer step"
    bp, lp, d = x_prompt.shape
    n_mem = mem_prompt.shape[1]
    xa_width = xa_wq.shape[2]
    row = lambda a: a.reshape(1, -1)
    p = {
        "norm_mix_g": row(norm_mix_g[0]), "w_in": w_in[0].astype(BF16), "b_gate": row(b_gate[0]),
        "pool_w": pool_w[0], "pool_scale": row(pool_scale[0]),
        "w_branch_pool": w_branch_pool[0], "w_branch_sb": w_branch_sb[0],
        "w_out": w_out[0], "norm_xa_g": row(norm_xa_g[0]),
        "xa_wq": xa_wq[0], "xa_wo": xa_wo[0],
        "norm_ffn_g": row(norm_ffn_g[0]), "ffn_w_up": ffn_w_up[0],
        "ffn_conv_w": ffn_conv_w[0], "ffn_conv_b": row(ffn_conv_b[0]),
        "ffn_w_down": ffn_w_down[0], "norm_final_g": row(norm_final_g),
    }

    mem_kv = _norm_matmul(mem_prompt.reshape(bp * n_mem, d), row(norm_mem_g[0]), xa_wkv[0].astype(BF16))
    mem_kv = mem_kv.reshape(bp, n_mem, 2 * xa_width)
    mk_p, mv_p = mem_kv[..., :xa_width], mem_kv[..., xa_width:]

    conv0 = jnp.zeros((bp, CONV_W - 1, ffn_w_up.shape[2]), F32)
    (y_p, pool_p, k_p, v_p, conv_p), late = _layer(x_prompt, None, None, None, mk_p.astype(BF16),
                                                   mv_p.astype(BF16), conv0, p)

    bs = x_sample.shape[0]
    mk_s = cache_mem_k[0].reshape(bs, n_mem, xa_width).astype(BF16)
    mv_s = cache_mem_v[0].reshape(bs, n_mem, xa_width).astype(BF16)
    (y_s, pool_s, k_s, v_s, conv_s), _ = _layer(x_sample, state_pool[0], cache_sb_k[0], cache_sb_v[0], mk_s, mv_s,
                                                state_ffn_conv[0], p, late)

    mem_shape = (1, bp, n_mem, XA_HEADS, HEAD_DIM)
    return (y_p, y_s,
            pool_p[None], k_p[None], v_p[None], mk_p.reshape(mem_shape), mv_p.reshape(mem_shape), conv_p[None],
            pool_s[None], k_s[None], v_s[None], conv_s[None])
```
